```python
import math
import jax
import jax.numpy as jnp
from jax import lax
import numpy as np

D_MODEL = 1024
BATCH = 16
SEQ = 256
DEPTH = 2
DEC_BATCH = 8
DEC_SEQ = 1024
PAST_LEN = 512

GRID_W = 64
MIX_W = D_MODEL
NA_HEADS = 8
NA_HD = MIX_W // 2 // NA_HEADS
NA_W = NA_HEADS * NA_HD
NA_KH = 8
NA_KW = 16
NA_QCB = 16
NA_KCB = 32
CTX_QBLK = 128
GDN_HEADS = 4
GDN_DK = MIX_W // 2 // GDN_HEADS
GDN_DV = GDN_DK
GDN_W = GDN_HEADS * GDN_DV
CONV_K = 5
GDN_CHUNK = 64
ROPE_BASE = 10000.0
N_GROUPS = 4
EXP_PER_GROUP = 4
N_EXPERTS = N_GROUPS * EXP_PER_GROUP
TOP_K = 2
EXPERT_FF = D_MODEL // 2
EPS = 1e-6
N_IN = 3 * NA_W + 4 * GDN_W + 4 * GDN_HEADS
SPLITS = (NA_W, 2 * NA_W, 3 * NA_W, 3 * NA_W + 3 * GDN_W, 3 * NA_W + 4 * GDN_W, 3 * NA_W + 4 * GDN_W + 2 * GDN_HEADS)

kernel_name = "hybrid_na_gdn_hmoe_diffusion_step"

F32 = jnp.float32


def rms_norm(x, g):
    xf = x.astype(F32)
    y = xf * lax.rsqrt(jnp.mean(xf * xf, axis=-1, keepdims=True) + EPS)
    return (y * g.astype(F32)).astype(x.dtype)


def l2_normalize(x):
    return x * lax.rsqrt(jnp.sum(x * x, axis=-1, keepdims=True) + EPS)


def modulation(cvec, lw):
    m = jax.nn.silu(cvec) @ lw["ada_w"] + lw["ada_b"]
    return jnp.split(m[..., None, :], 6, axis=-1)


def modulate(h, shift, scale):
    return h * (1.0 + scale) + shift


def axial_rope(x):
    T = x.shape[1]
    t = jnp.arange(T)
    half = x.shape[-1] // 2
    inv_freq = ROPE_BASE ** (-jnp.arange(0, half, 2, dtype=F32) / half)

    def rot(xa, pos):
        ang = pos.astype(F32)[:, None] * inv_freq
        cos = jnp.cos(ang)[None, :, None, :]
        sin = jnp.sin(ang)[None, :, None, :]
        x1, x2 = jnp.split(xa, 2, axis=-1)
        return jnp.concatenate([x1 * cos - x2 * sin, x1 * sin + x2 * cos], axis=-1)

    return jnp.concatenate([rot(x[..., :half], t // GRID_W), rot(x[..., half:], t % GRID_W)], axis=-1)


def dw_conv(x, w):
    return lax.conv_general_dilated(
        x, w[:, None, :].astype(x.dtype), window_strides=(1,),
        padding=[(CONV_K // 2, CONV_K // 2)],
        dimension_numbers=("NWC", "WIO", "NWC"), feature_group_count=x.shape[-1])


def dense_context_attention(q, k, v):
    B, S, H, d = q.shape
    qb = q.reshape(B, S // CTX_QBLK, CTX_QBLK, H, d).transpose(1, 0, 2, 3, 4)

    def one_block(qblk):
        s = jnp.einsum("bqhd,bshd->bhqs", qblk, k).astype(F32)
        p = jax.nn.softmax(s, axis=-1).astype(v.dtype)
        return jnp.einsum("bhqs,bshd->bqhd", p, v)

    o = lax.map(one_block, qb)
    return o.transpose(1, 0, 2, 3, 4).reshape(B, S, H, d)


def neighbourhood_attention(q, k, v, k_ctx, v_ctx, rpb):
    B, T, H, d = q.shape
    rows = T // GRID_W
    kh = min(NA_KH, rows)
    ncb = GRID_W // NA_QCB
    r = np.arange(rows)
    row_idx = np.clip(r - kh // 2, 0, rows - kh)[:, None] + np.arange(kh)
    qcol = np.arange(GRID_W).reshape(ncb, NA_QCB)
    win_start = np.clip(qcol - NA_KW // 2, 0, GRID_W - NA_KW)
    kcol = np.clip(np.arange(ncb) * NA_QCB - NA_KW // 2, 0, GRID_W - NA_KCB)[:, None] + np.arange(NA_KCB)
    valid = (kcol[:, None, :] >= win_start[:, :, None]) & (kcol[:, None, :] < win_start[:, :, None] + NA_KW)
    dr = row_idx - r[:, None] + NA_KH - 1
    dc = np.clip(kcol[:, None, :] - qcol[:, :, None] + NA_KW - 1, 0, 2 * NA_KW - 2)
    bias = rpb[:, dr[:, None, None, :, None], dc[None, :, :, None, :]].astype(F32)
    bias = jnp.where(valid[None, None, :, :, None, :], bias, -jnp.inf)
    kg = k.reshape(B, rows, GRID_W, H, d)[:, row_idx][:, :, :, kcol]
    vg = v.reshape(B, rows, GRID_W, H, d)[:, row_idx][:, :, :, kcol]
    qb = q.reshape(B, rows, ncb, NA_QCB, H, d)
    s_loc = jnp.einsum("brjqhd,brijkhd->bhrjqik", qb, kg).astype(F32) + bias
    s_loc = s_loc.reshape(B, H, rows, ncb, NA_QCB, kh * NA_KCB)
    s_ctx = jnp.einsum("brjqhd,bshd->bhrjqs", qb, k_ctx).astype(F32)
    p = jax.nn.softmax(jnp.concatenate([s_loc, s_ctx], axis=-1), axis=-1).astype(v.dtype)
    p_loc = p[..., :kh * NA_KCB].reshape(B, H, rows, ncb, NA_QCB, kh, NA_KCB)
    p_ctx = p[..., kh * NA_KCB:]
    o = jnp.einsum("bhrjqik,brijkhd->brjqhd", p_loc, vg) + jnp.einsum("bhrjqs,bshd->brjqhd", p_ctx, v_ctx)
    return o.reshape(B, T, H, d)


def gated_delta_chunked(q, k, v, g, beta, s0):
    B, T, H, DK = q.shape
    n = T // GDN_CHUNK

    def blocks(a):
        a = a.reshape(B, n, GDN_CHUNK, H, *a.shape[3:])
        return jnp.moveaxis(a, 3, 1)

    q, k, v, g, beta = blocks(q), blocks(k), blocks(v), blocks(g), blocks(beta)
    gc = jnp.cumsum(g, axis=-1)
    i = np.arange(GDN_CHUNK)
    incl = i[:, None] >= i[None, :]
    strict = i[:, None] > i[None, :]
    decay = jnp.exp(jnp.where(incl, gc[..., :, None] - gc[..., None, :], -jnp.inf))
    kb = k * beta[..., None]
    a_mat = jnp.where(strict, jnp.einsum("bhnid,bhnjd->bhnij", kb, k) * decay, 0.0)
    eye = jnp.eye(GDN_CHUNK, dtype=F32)
    t_inv = lax.linalg.triangular_solve(eye + a_mat, jnp.broadcast_to(eye, a_mat.shape),
                                        left_side=True, lower=True, unit_diagonal=True)
    u = t_inv @ (v * beta[..., None])
    w = t_inv @ (kb * jnp.exp(gc)[..., None])
    qk = jnp.where(incl, jnp.einsum("bhnid,bhnjd->bhnij", q, k) * decay, 0.0)
    q_dec = q * jnp.exp(gc)[..., None]
    k_dec = k * jnp.exp(gc[..., -1:] - gc)[..., None]
    g_last = jnp.exp(gc[..., -1])
    xs = (jnp.moveaxis(u, 2, 0), jnp.moveaxis(w, 2, 0), jnp.moveaxis(qk, 2, 0),
          jnp.moveaxis(q_dec, 2, 0), jnp.moveaxis(k_dec, 2, 0), jnp.moveaxis(g_last, 2, 0))

    def step(s, inp):
        u_n, w_n, qk_n, qd_n, kd_n, gl_n = inp
        v_new = u_n - w_n @ s
        o_n = qd_n @ s + qk_n @ v_new
        s = s * gl_n[..., None, None] + jnp.einsum("bhck,bhcv->bhkv", kd_n, v_new)
        return s, o_n

    s_fin, o = lax.scan(step, s0, xs)
    o = o.transpose(1, 0, 3, 2, 4).reshape(B, T, H, v.shape[-1])
    return o, s_fin


def gdn_inputs(zqkv, zb, za, lw, axial):
    B, T, _ = zqkv.shape
    qkv = jax.nn.silu(dw_conv(zqkv, lw["gdn_conv_w"])).astype(F32)
    q, k, v = jnp.split(qkv, 3, axis=-1)
    q = l2_normalize(q.reshape(B, T, GDN_HEADS, GDN_DK))
    k = l2_normalize(k.reshape(B, T, GDN_HEADS, GDN_DK))
    v = v.reshape(B, T, GDN_HEADS, GDN_DV)
    if axial:
        q, k = axial_rope(q), axial_rope(k)
    q = q * GDN_DK ** -0.5
    beta = jax.nn.sigmoid(zb.astype(F32)).reshape(B, T, 2, GDN_HEADS)
    g = -jnp.exp(lw["gdn_a_log"].astype(F32)) * jax.nn.softplus(
        za.astype(F32).reshape(B, T, 2, GDN_HEADS) + lw["gdn_dt_bias"].astype(F32))
    return q, k, v, g, beta


def gdn_bidirectional(q, k, v, g, beta, s0):
    o_f, s_f = gated_delta_chunked(q, k, v, g[:, :, 0], beta[:, :, 0], s0[:, 0])
    fl = lambda a: jnp.flip(a, axis=1)
    o_b, s_b = gated_delta_chunked(fl(q), fl(k), fl(v), fl(g[:, :, 1]), fl(beta[:, :, 1]), s0[:, 1])
    return o_f + fl(o_b), jnp.stack([s_f, s_b], axis=1)


def gdn_output(o, zg, norm_g, dtype):
    B, T = o.shape[:2]
    o = rms_norm(o, norm_g) * jax.nn.silu(zg.astype(F32).reshape(B, T, GDN_HEADS, GDN_DV))
    return o.reshape(B, T, GDN_W).astype(dtype)


def na_heads(zq, zk, zv, lw):
    B, T, _ = zq.shape
    q = rms_norm(zq.reshape(B, T, NA_HEADS, NA_HD), lw["na_qn_g"]) * NA_HD ** -0.5
    k = rms_norm(zk.reshape(B, T, NA_HEADS, NA_HD), lw["na_kn_g"])
    v = zv.reshape(B, T, NA_HEADS, NA_HD)
    return q, k, v


def hier_moe(h, lw):
    B, T, D = h.shape
    t = h.reshape(-1, D)
    gp = jax.nn.softmax((t @ lw["moe_w_rg"] + lw["moe_b_rg"]).astype(F32), axis=-1)
    g_w, g_idx = lax.top_k(gp, 1)
    el = (t @ lw["moe_w_re"] + lw["moe_b_re"]).astype(F32).reshape(-1, N_GROUPS, EXP_PER_GROUP)
    el_sel = jnp.take_along_axis(el, g_idx[:, :, None], axis=1)[:, 0]
    ep = jax.nn.softmax(el_sel, axis=-1)
    e_w, e_idx = lax.top_k(ep, TOP_K)
    e_w = e_w / jnp.sum(e_w, axis=-1, keepdims=True)
    within = jnp.sum(jax.nn.one_hot(e_idx, EXP_PER_GROUP, dtype=F32) * e_w[..., None], axis=1)
    gate = jax.nn.one_hot(g_idx[:, 0], N_GROUPS, dtype=F32)[:, :, None] * (g_w[:, :, None] * within[:, None, :])
    gate = gate.reshape(-1, N_EXPERTS).astype(h.dtype)
    a = jnp.einsum("nd,edf->nef", t, lw["moe_w1"])
    b = jnp.einsum("nd,edf->nef", t, lw["moe_w3"])
    y = jnp.einsum("nef,efd->nd", jax.nn.silu(a) * b * gate[..., None], lw["moe_w2"])
    return y.reshape(B, T, D)


def context_layer(x, c_ctx, lw):
    B, S, _ = x.shape
    sh1, sc1, g1, sh2, sc2, g2 = modulation(c_ctx, lw)
    h = modulate(rms_norm(x, lw["norm1_g"]), sh1, sc1)
    zq, zk, zv, zqkv, zg, zb, za = jnp.split(h @ lw["w_in"], SPLITS, axis=-1)
    q, k, v = na_heads(zq, zk, zv, lw)
    o_na = dense_context_attention(q, k, v)
    gq, gk, gv, gg, gbeta = gdn_inputs(zqkv, zb, za, lw, axial=False)
    s0 = jnp.zeros((B, 2, GDN_HEADS, GDN_DK, GDN_DV), F32)
    o_g, s_fin = gdn_bidirectional(gq, gk, gv, gg, gbeta, s0)
    o_mix = jnp.concatenate([o_na.reshape(B, S, NA_W), gdn_output(o_g, zg, lw["gdn_norm_g"], x.dtype)], axis=-1)
    x = x + g1 * (o_mix @ lw["w_out"])
    x = x + g2 * hier_moe(modulate(rms_norm(x, lw["norm2_g"]), sh2, sc2), lw)
    return x, k, v, s_fin


def latent_layer(x, c, k_ctx, v_ctx, s_ctx, lw):
    B, T, _ = x.shape
    sh1, sc1, g1, sh2, sc2, g2 = modulation(c, lw)
    h = modulate(rms_norm(x, lw["norm1_g"]), sh1, sc1)
    zq, zk, zv, zqkv, zg, zb, za = jnp.split(h @ lw["w_in"], SPLITS, axis=-1)
    q, k, v = na_heads(zq, zk, zv, lw)
    o_na = neighbourhood_attention(q, k, v, k_ctx.astype(k.dtype), v_ctx.astype(v.dtype), lw["na_rpb"])
    gq, gk, gv, gg, gbeta = gdn_inputs(zqkv, zb, za, lw, axial=True)
    o_g, _ = gdn_bidirectional(gq, gk, gv, gg, gbeta, s_ctx.astype(F32))
    o_mix = jnp.concatenate([o_na.reshape(B, T, NA_W), gdn_output(o_g, zg, lw["gdn_norm_g"], x.dtype)], axis=-1)
    x = x + g1 * (o_mix @ lw["w_out"])
    x = x + g2 * hier_moe(modulate(rms_norm(x, lw["norm2_g"]), sh2, sc2), lw)
    return x


def setup_inputs(seed: int = 0) -> dict:
    key = jax.random.key(seed)
    ks = iter(jax.random.split(key, 32))

    def nrm(shape, scale):
        return scale * jax.random.normal(next(ks), shape, F32)

    def gain(shape):
        return 1.0 + nrm(shape, 0.05)

    a_log = jnp.log(jax.random.uniform(next(ks), (DEPTH, 2, GDN_HEADS), F32, minval=1.0, maxval=16.0))
    dt = jnp.exp(jax.random.uniform(next(ks), (DEPTH, 2, GDN_HEADS), F32,
                                    minval=math.log(1e-3), maxval=math.log(1e-1)))
    dt_bias = dt + jnp.log(-jnp.expm1(-dt))
    return {
        "x_prompt": nrm((BATCH, SEQ, D_MODEL), 1.0),
        "x_sample": nrm((DEC_BATCH, DEC_SEQ, D_MODEL), 1.0),
        "c": nrm((DEC_BATCH, D_MODEL), 1.0),
        "cache_k": nrm((DEC_BATCH, DEPTH, PAST_LEN, NA_HEADS, NA_HD), 1.0),
        "cache_v": nrm((DEC_BATCH, DEPTH, PAST_LEN, NA_HEADS, NA_HD), 1.0),
        "state_ssm": nrm((DEC_BATCH, DEPTH, 2, GDN_HEADS, GDN_DK, GDN_DV), GDN_DK ** -0.5),
        "c_ctx": nrm((D_MODEL,), 1.0),
        "ada_w": nrm((DEPTH, D_MODEL, 6 * D_MODEL), 0.5 * D_MODEL ** -0.5),
        "ada_b": nrm((DEPTH, 6 * D_MODEL), 0.02),
        "norm1_g": gain((DEPTH, D_MODEL)),
        "norm2_g": gain((DEPTH, D_MODEL)),
        "w_in": nrm((DEPTH, D_MODEL, N_IN), D_MODEL ** -0.5),
        "w_out": nrm((DEPTH, MIX_W, D_MODEL), MIX_W ** -0.5),
        "na_qn_g": gain((DEPTH, NA_HD)),
        "na_kn_g": gain((DEPTH, NA_HD)),
        "na_rpb": nrm((DEPTH, NA_HEADS, 2 * NA_KH - 1, 2 * NA_KW - 1), 0.1),
        "gdn_conv_w": nrm((DEPTH, CONV_K, 3 * GDN_W), CONV_K ** -0.5),
        "gdn_a_log": a_log,
        "gdn_dt_bias": dt_bias,
        "gdn_norm_g": gain((DEPTH, GDN_DV)),
        "moe_w_rg": nrm((DEPTH, D_MODEL, N_GROUPS), D_MODEL ** -0.5),
        "moe_b_rg": nrm((DEPTH, N_GROUPS), 0.01),
        "moe_w_re": nrm((DEPTH, D_MODEL, N_EXPERTS), D_MODEL ** -0.5),
        "moe_b_re": nrm((DEPTH, N_EXPERTS), 0.01),
        "moe_w1": nrm((DEPTH, N_EXPERTS, D_MODEL, EXPERT_FF), D_MODEL ** -0.5),
        "moe_w3": nrm((DEPTH, N_EXPERTS, D_MODEL, EXPERT_FF), D_MODEL ** -0.5),
        "moe_w2": nrm((DEPTH, N_EXPERTS, EXPERT_FF, D_MODEL), EXPERT_FF ** -0.5),
    }


def reference(x_prompt, x_sample, c, cache_k, cache_v, state_ssm, c_ctx, ada_w, ada_b, norm1_g, norm2_g,
              w_in, w_out, na_qn_g, na_kn_g, na_rpb, gdn_conv_w, gdn_a_log, gdn_dt_bias, gdn_norm_g,
              moe_w_rg, moe_b_rg, moe_w_re, moe_b_re, moe_w1, moe_w3, moe_w2):
    xp = x_prompt
    xs = x_sample
    ks, vs, ss = [], [], []
    for l in range(DEPTH):
        lw = {
            "ada_w": ada_w[l], "ada_b": ada_b[l], "norm1_g": norm1_g[l], "norm2_g": norm2_g[l],
            "w_in": w_in[l], "w_out": w_out[l], "na_qn_g": na_qn_g[l], "na_kn_g": na_kn_g[l],
            "na_rpb": na_rpb[l], "gdn_conv_w": gdn_conv_w[l], "gdn_a_log": gdn_a_log[l],
            "gdn_dt_bias": gdn_dt_bias[l], "gdn_norm_g": gdn_norm_g[l],
            "moe_w_rg": moe_w_rg[l], "moe_b_rg": moe_b_rg[l], "moe_w_re": moe_w_re[l], "moe_b_re": moe_b_re[l],
            "moe_w1": moe_w1[l], "moe_w3": moe_w3[l], "moe_w2": moe_w2[l],
        }
        xp, k_l, v_l, s_l = context_layer(xp, c_ctx, lw)
        ks.append(k_l)
        vs.append(v_l)
        ss.append(s_l)
        xs = latent_layer(xs, c, cache_k[:, l], cache_v[:, l], state_ssm[:, l], lw)
    y_prompt = xp
    y_sample = xs
    new_cache_k = jnp.stack(ks, axis=1)
    new_cache_v = jnp.stack(vs, axis=1)
    new_state_ssm = jnp.stack(ss, axis=1)
    return (y_prompt, y_sample, new_cache_k, new_cache_v, new_state_ssm)
```

```python
import functools

import jax
import jax.numpy as jnp
import numpy as np
from jax import lax
from jax.experimental import pallas as pl
from jax.experimental.pallas import tpu as pltpu

F32 = jnp.float32
BF16 = jnp.bfloat16

D_MODEL = 1024
BATCH = 16
SEQ = 256
DEPTH = 2
DEC_BATCH = 8
DEC_SEQ = 1024
PAST_LEN = 512
GRID_W = 64
GRID_ROWS = DEC_SEQ // GRID_W
NA_HEADS = 8
NA_HD = 64
NA_W = NA_HEADS * NA_HD
NA_KH = 8
NA_KW = 16
GDN_HEADS = 4
GDN_DK = 128
GDN_W = GDN_HEADS * GDN_DK
CONV_K = 5
CHUNK = 64
ROPE_BASE = 10000.0
N_GROUPS = 4
EXP_PER_GROUP = 4
N_EXPERTS = 16
EXPERT_FF = 512
EPS = 1e-6
N_IN = 3 * NA_W + 4 * GDN_W + 4 * GDN_HEADS

N_CTX = BATCH * SEQ
N_LAT = DEC_BATCH * DEC_SEQ
N_TOK = N_CTX + N_LAT
N_MOD = 1 + DEC_BATCH
N_MOD_PAD = 16
LANE = 128
N_MAIN = 3 * NA_W + 4 * GDN_W
TM = 256
TM_MOE = 1024
VMEM_LIMIT = 56 * 1024 * 1024


def _dot(a, b):
    return jnp.dot(a, b, preferred_element_type=F32)


def _dot_nt(a, b):
    return lax.dot_general(a, b, (((1,), (1,)), ((), ())), preferred_element_type=F32)


def _dot_tn(a, b):
    return lax.dot_general(a, b, (((0,), (0,)), ((), ())), preferred_element_type=F32)


def _split2(x):
    hi = x.astype(BF16)
    lo = (x - hi.astype(F32)).astype(BF16)
    return hi, lo


def _split3(x):
    hi = x.astype(BF16)
    r = x - hi.astype(F32)
    mid = r.astype(BF16)
    lo = (r - mid.astype(F32)).astype(BF16)
    return hi, mid, lo


def _dot3(a, b):
    ah, al = _split2(a)
    bh, bl = _split2(b)
    return _dot(ah, bh) + (_dot(ah, bl) + _dot(al, bh))


def _dot_exact_lhs(mask_bf16, x):
    hi, mid, lo = _split3(x)
    return _dot(mask_bf16, hi) + (_dot(mask_bf16, mid) + _dot(mask_bf16, lo))


def _silu(x):
    return x * (1.0 / (1.0 + jnp.exp(-x)))


def _params(sem):
    return pltpu.CompilerParams(dimension_semantics=sem, vmem_limit_bytes=VMEM_LIMIT)


def _mod_kernel(cv_ref, w_ref, b_ref, o_ref):
    cv = cv_ref[...]
    s = _silu(cv)
    o_ref[...] = jnp.dot(s, w_ref[...], preferred_element_type=F32,
                         precision=lax.Precision.HIGHEST) + b_ref[...]


def _modulation(cv, ada_w, ada_b):
    nblk = 512
    return pl.pallas_call(
        _mod_kernel,
        out_shape=jax.ShapeDtypeStruct((DEPTH, N_MOD_PAD, 6 * D_MODEL), F32),
        grid=(DEPTH, 6 * D_MODEL // nblk),
        in_specs=[
            pl.BlockSpec((N_MOD_PAD, D_MODEL), lambda l, j: (0, 0)),
            pl.BlockSpec((None, D_MODEL, nblk), lambda l, j: (l, 0, j)),
            pl.BlockSpec((None, 1, nblk), lambda l, j: (l, 0, j)),
        ],
        out_specs=pl.BlockSpec((None, N_MOD_PAD, nblk), lambda l, j: (l, 0, j)),
        compiler_params=_params(("parallel", "parallel")),
        name="modulation",
    )(cv, ada_w, ada_b.reshape(DEPTH, 1, 6 * D_MODEL))


def _inproj_kernel(idx_ref, x_ref, mod_ref, n1g_ref, win_ref, wba_ref, ones_ref, qg_ref, kg_ref,
                   q_ref, k_ref, v_ref, zqkv_ref, zg_ref, zba_ref):
    del idx_ref
    x = x_ref[...]
    y = x * lax.rsqrt(jnp.mean(x * x, axis=-1, keepdims=True) + EPS) * n1g_ref[...]
    m = mod_ref[...]
    h = (y * (1.0 + m[1:2, :]) + m[0:1, :]).astype(BF16)
    z = _dot(h, win_ref[...])
    zba_ref[...] = _dot(h, wba_ref[...])
    ones = ones_ref[...]

    def head_rms(zz, g):
        hi, lo = _split2(zz * zz)
        ms = (_dot(hi, ones) + _dot(lo, ones)) * (1.0 / NA_HD)
        return zz * lax.rsqrt(ms + EPS) * g

    q_ref[...] = head_rms(z[:, 0:NA_W], qg_ref[...]) * (NA_HD ** -0.5)
    k_ref[...] = head_rms(z[:, NA_W:2 * NA_W], kg_ref[...])
    v_ref[...] = z[:, 2 * NA_W:3 * NA_W]
    zqkv_ref[...] = z[:, 3 * NA_W:3 * NA_W + 3 * GDN_W]
    zg_ref[...] = z[:, 3 * NA_W + 3 * GDN_W:N_MAIN]


def _inproj(tile_mod, x, mod, n1g, w_main, w_ba, ones_bd, qg, kg):
    tok = lambda w: pl.BlockSpec((TM, w), lambda t, idx: (t, 0))
    full = lambda a: pl.BlockSpec(a.shape, lambda t, idx: (0,) * a.ndim)
    out_w = (NA_W, NA_W, NA_W, 3 * GDN_W, GDN_W, LANE)
    return pl.pallas_call(
        _inproj_kernel,
        out_shape=[jax.ShapeDtypeStruct((N_TOK, w), F32) for w in out_w],
        grid_spec=pltpu.PrefetchScalarGridSpec(
            num_scalar_prefetch=1,
            grid=(N_TOK // TM,),
            in_specs=[
                tok(D_MODEL),
                pl.BlockSpec((None, 6, D_MODEL), lambda t, idx: (idx[t], 0, 0)),
                full(n1g), full(w_main), full(w_ba), full(ones_bd), full(qg), full(kg),
            ],
            out_specs=[tok(w) for w in out_w],
        ),
        compiler_params=_params(("parallel",)),
        name="inproj",
    )(tile_mod, x, mod, n1g, w_main, w_ba, ones_bd, qg, kg)


def _pair_masks(shape):
    lane = lax.broadcasted_iota(jnp.int32, shape, 1)
    return lane < NA_HD


def _ctx_attn_kernel(q_ref, k_ref, v_ref, o_ref):
    for p in range(NA_HEADS // 2):
        cols = slice(p * LANE, (p + 1) * LANE)
        q = q_ref[:, cols]
        kb = k_ref[:, cols].astype(BF16)
        vb = v_ref[:, cols].astype(BF16)
        first = _pair_masks(q.shape)
        outs = []
        for hm in (first, jnp.logical_not(first)):
            s = _dot_nt(jnp.where(hm, q, 0.0).astype(BF16), kb)
            e = jnp.exp(s - jnp.max(s, axis=-1, keepdims=True))
            o = _dot(e.astype(BF16), vb)
            outs.append(o / jnp.sum(e, axis=-1, keepdims=True))
        o_ref[:, cols] = jnp.where(first, outs[0], outs[1])


def _ctx_attention(q, k, v):
    blk = pl.BlockSpec((SEQ, NA_W), lambda b: (b, 0))
    return pl.pallas_call(
        _ctx_attn_kernel,
        out_shape=jax.ShapeDtypeStruct((N_CTX, NA_W), F32),
        grid=(BATCH,),
        in_specs=[blk, blk, blk],
        out_specs=blk,
        compiler_params=_params(("parallel",)),
        name="ctx_attention",
    )(q, k, v)


def _na_kernel(q_ref, k_ref, v_ref, kc_ref, vc_ref, bias_ref, o_ref):
    r = pl.program_id(1)
    base = jnp.clip(r - NA_KH // 2, 0, GRID_ROWS - NA_KH)
    dr0 = base - r + (NA_KH - 1)
    row0 = pl.multiple_of(base * GRID_W, GRID_W)
    nwin = NA_KH * GRID_W
    for p in range(NA_HEADS // 2):
        cols = slice(p * LANE, (p + 1) * LANE)
        q = q_ref[:, cols]
        kw = k_ref[pl.ds(row0, nwin), cols].astype(BF16)
        vw = v_ref[pl.ds(row0, nwin), cols].astype(BF16)
        kc = kc_ref[:, cols].astype(BF16)
        vc = vc_ref[:, cols].astype(BF16)
        first = _pair_masks(q.shape)
        outs = []
        for j, hm in enumerate((first, jnp.logical_not(first))):
            qh = jnp.where(hm, q, 0.0).astype(BF16)
            s_loc = _dot_nt(qh, kw) + bias_ref[2 * p + j, dr0]
            s_ctx = _dot_nt(qh, kc)
            mx = jnp.maximum(jnp.max(s_loc, axis=-1, keepdims=True),
                             jnp.max(s_ctx, axis=-1, keepdims=True))
            e_loc = jnp.exp(s_loc - mx)
            e_ctx = jnp.exp(s_ctx - mx)
            den = jnp.sum(e_loc, axis=-1, keepdims=True) + jnp.sum(e_ctx, axis=-1, keepdims=True)
            o = _dot(e_loc.astype(BF16), vw) + _dot(e_ctx.astype(BF16), vc)
            outs.append(o / den)
        o_ref[:, cols] = jnp.where(first, outs[0], outs[1])


def _na_attention(layer, q, k, v, cache_k, cache_v, bias):
    lat0 = N_CTX // DEC_SEQ
    qblk = pl.BlockSpec((GRID_W, NA_W), lambda b, r: (N_CTX // GRID_W + b * GRID_ROWS + r, 0))
    kvblk = pl.BlockSpec((DEC_SEQ, NA_W), lambda b, r: (lat0 + b, 0))
    cblk = pl.BlockSpec((None, None, PAST_LEN, NA_W), lambda b, r: (b, layer, 0, 0))
    return pl.pallas_call(
        _na_kernel,
        out_shape=jax.ShapeDtypeStruct((N_LAT, NA_W), F32),
        grid=(DEC_BATCH, GRID_ROWS),
        in_specs=[qblk, kvblk, kvblk, cblk, cblk,
                  pl.BlockSpec(bias.shape, lambda b, r: (0, 0, 0, 0))],
        out_specs=pl.BlockSpec((GRID_W, NA_W), lambda b, r: (b * GRID_ROWS + r, 0)),
        compiler_params=_params(("parallel", "arbitrary")),
        name="na_attention",
    )(q, k, v, cache_k, cache_v, bias)


def _na_bias_table(rpb):
    qcol = np.arange(GRID_W)[:, None]
    kcol = np.arange(GRID_W)[None, :]
    ws = np.clip(qcol - NA_KW // 2, 0, GRID_W - NA_KW)
    valid = (kcol >= ws) & (kcol < ws + NA_KW)
    dc = np.clip(kcol - qcol + NA_KW - 1, 0, 2 * NA_KW - 2)
    dr = np.arange(NA_KH)[:, None] + np.arange(NA_KH)[None, :]
    t = rpb[:, dr[:, :, None, None], dc[None, None, :, :]]
    t = jnp.where(valid[None, None, None], t, -jnp.inf)
    return t.transpose(0, 1, 3, 2, 4).reshape(NA_HEADS, NA_KH, GRID_W, NA_KH * GRID_W)


def _rope_tables():
    t = np.arange(DEC_SEQ)
    half = GDN_DK // 2
    inv_freq = (np.float32(ROPE_BASE) ** (-np.arange(0, half, 2, dtype=np.float32) / np.float32(half)))
    ang_r = (t // GRID_W).astype(np.float32)[:, None] * inv_freq
    ang_c = (t % GRID_W).astype(np.float32)[:, None] * inv_freq
    cr, sr, cc, sc = np.cos(ang_r), np.sin(ang_r), np.cos(ang_c), np.sin(ang_c)
    z = np.zeros_like(sr)
    cos = np.concatenate([cr, cr, cc, cc], axis=1)
    s_up = np.concatenate([-sr, z, -sc, z], axis=1)
    s_dn = np.concatenate([z, sr, z, sc], axis=1)
    return (jnp.asarray(cos, F32), jnp.asarray(s_up, F32), jnp.asarray(s_dn, F32))


PREP_ROWS = 64
HALO = 8


def _gdn_prep_kernel(axial, seq, *refs):
    if axial:
        (zqkv_ref, zba_ref, cw_ref, nega_ref, dtb_ref, cos_ref, sup_ref, sdn_ref,
         q_ref, k_ref, v_ref, bg_ref, xs_ref) = refs
    else:
        (zqkv_ref, zba_ref, cw_ref, nega_ref, dtb_ref,
         q_ref, k_ref, v_ref, bg_ref, xs_ref) = refs
    width = 3 * GDN_W
    xs_ref[0:HALO, :] = jnp.zeros((HALO, width), F32)
    xs_ref[HALO + seq:HALO + seq + HALO, :] = jnp.zeros((HALO, width), F32)
    xs_ref[HALO:HALO + seq, :] = zqkv_ref[...]

    zba = zba_ref[...]
    lane = lax.broadcasted_iota(jnp.int32, zba.shape, 1)
    beta = 1.0 / (1.0 + jnp.exp(-zba))
    xa = zba + dtb_ref[...]
    softplus = jnp.maximum(xa, 0.0) + jnp.log(1.0 + jnp.exp(-jnp.abs(xa)))
    g = -jnp.exp(nega_ref[...]) * softplus
    bg_ref[...] = jnp.where(lane < 2 * GDN_HEADS, beta, g)

    def body(i, carry):
        r0 = pl.multiple_of(i * PREP_ROWS, PREP_ROWS)
        win = xs_ref[pl.ds(r0, PREP_ROWS + 2 * HALO), :]
        acc = None
        for j in range(CONV_K):
            lo = HALO + j - CONV_K // 2
            tap = win[lo:lo + PREP_ROWS, :] * cw_ref[j:j + 1, :]
            acc = tap if acc is None else acc + tap
        a = _silu(acc)
        rows = pl.ds(r0, PREP_ROWS)
        for hd in range(GDN_HEADS):
            for which, ref in ((0, q_ref), (1, k_ref)):
                c0 = which * GDN_W + hd * GDN_DK
                xh = a[:, c0:c0 + GDN_DK]
                xh = xh * lax.rsqrt(jnp.sum(xh * xh, axis=-1, keepdims=True) + EPS)
                if axial:
                    xh = (xh * cos_ref[rows, :]
                          + pltpu.roll(xh, GDN_DK - GDN_DK // 4, 1) * sup_ref[rows, :]
                          + pltpu.roll(xh, GDN_DK // 4, 1) * sdn_ref[rows, :])
                if which == 0:
                    xh = xh * (GDN_DK ** -0.5)
                ref[rows, hd * GDN_DK:(hd + 1) * GDN_DK] = xh
        v_ref[rows, :] = a[:, 2 * GDN_W:3 * GDN_W]
        return carry

    lax.fori_loop(0, seq // PREP_ROWS, body, 0)


def _gdn_prep(axial, zqkv, zba, conv_w, neg_a_log, dt_bias, rope):
    seq = DEC_SEQ if axial else SEQ
    nseq = DEC_BATCH if axial else BATCH
    blk0 = N_CTX // DEC_SEQ if axial else 0
    tok = lambda w: pl.BlockSpec((seq, w), lambda b: (blk0 + b, 0))
    full = lambda a: pl.BlockSpec(a.shape, lambda b: (0,) * a.ndim)
    ins = [zqkv, zba, conv_w, neg_a_log, dt_bias]
    specs = [tok(3 * GDN_W), tok(LANE), full(conv_w), full(neg_a_log), full(dt_bias)]
    if axial:
        ins += list(rope)
        specs += [full(t) for t in rope]
    out_blk = lambda w: pl.BlockSpec((seq, w), lambda b: (b, 0))
    out_w = (GDN_W, GDN_W, GDN_W, LANE)
    return pl.pallas_call(
        functools.partial(_gdn_prep_kernel, axial, seq),
        out_shape=[jax.ShapeDtypeStruct((nseq * seq, w), F32) for w in out_w],
        grid=(nseq,),
        in_specs=specs,
        out_specs=[out_blk(w) for w in out_w],
        scratch_shapes=[pltpu.VMEM((seq + 2 * HALO, 3 * GDN_W), F32)],
        compiler_params=_params(("parallel",)),
        name="gdn_prep_lat" if axial else "gdn_prep_ctx",
    )(*ins)


def _tri_masks(backward):
    i = lax.broadcasted_iota(jnp.int32, (CHUNK, CHUNK), 0)
    j = lax.broadcasted_iota(jnp.int32, (CHUNK, CHUNK), 1)
    if backward:
        i, j = j, i
    return i, j


def _unit_tri_inverse(a, i, j):
    eye = (i == j).astype(F32)
    t = eye - jnp.where((i >> 1) == (j >> 1), a, 0.0)
    lg = 1
    while (1 << lg) < CHUNK:
        half = 1 << lg
        off = ((i >> (lg + 1)) == (j >> (lg + 1))) & ((i & half) != 0) & ((j & half) == 0)
        x = _dot3(jnp.where(off, a, 0.0), t)
        t = t - _dot3(t, x)
        lg += 1
    return t


def _gdn_kernel(nchunk, q_ref, k_ref, v_ref, bg_ref, s0_ref, o_ref, sfin_ref, state_ref):
    state_ref[...] = s0_ref[...]
    o_ref[...] = jnp.zeros(o_ref.shape, F32)
    masks = [_tri_masks(False), _tri_masks(True)]

    def body(c, carry):
        for d in range(2):
            i, j = masks[d]
            incl = i >= j
            strict = i > j
            cc = (nchunk - 1 - c) if d else c
            rows = pl.ds(pl.multiple_of(cc * CHUNK, CHUNK), CHUNK)
            bg = bg_ref[rows, :]
            gcum = _dot_exact_lhs(incl.astype(BF16), bg)
            gcum_t = gcum.T
            last = gcum[0:1, :] if d else gcum[CHUNK - 1:CHUNK, :]
            e_in = jnp.exp(gcum)
            e_out = jnp.exp(last - gcum)
            e_all = jnp.exp(last)
            for hd in range(GDN_HEADS):
                sidx = d * GDN_HEADS + hd
                cb = sidx
                cg = 2 * GDN_HEADS + sidx
                cols = slice(hd * GDN_DK, (hd + 1) * GDN_DK)
                q = q_ref[rows, cols]
                k = k_ref[rows, cols]
                v = v_ref[rows, cols]
                beta = bg[:, cb:cb + 1]
                eg = e_in[:, cg:cg + 1]
                el = e_out[:, cg:cg + 1]
                gl = e_all[:, cg:cg + 1]
                diff = gcum[:, cg:cg + 1] - gcum_t[cg:cg + 1, :]
                decay = jnp.where(incl, jnp.exp(jnp.where(incl, diff, 0.0)), 0.0)
                kb = k * beta
                kbf = k.astype(BF16)
                a = jnp.where(strict, _dot_nt(kb.astype(BF16), kbf) * decay, 0.0)
                tinv = _unit_tri_inverse(a, i, j).astype(BF16)
                u = _dot(tinv, (v * beta).astype(BF16))
                w = _dot(tinv, (kb * eg).astype(BF16))
                qk = jnp.where(incl, _dot_nt(q.astype(BF16), kbf) * decay, 0.0)
                s = state_ref[sidx]
                sb = s.astype(BF16)
                v_new = u - _dot(w.astype(BF16), sb)
                vnb = v_new.astype(BF16)
                o = _dot((q * eg).astype(BF16), sb) + _dot(qk.astype(BF16), vnb)
                state_ref[sidx] = s * gl + _dot_tn((k * el).astype(BF16), vnb)
                o_ref[rows, cols] += o
        return carry

    lax.fori_loop(0, nchunk, body, 0)
    sfin_ref[...] = state_ref[...]


def _gdn(seq, nseq, q, k, v, bg, s0):
    tok = lambda w: pl.BlockSpec((seq, w), lambda b: (b, 0))
    sblk = pl.BlockSpec((None, 2 * GDN_HEADS, GDN_DK, GDN_DK), lambda b: (b, 0, 0, 0))
    return pl.pallas_call(
        functools.partial(_gdn_kernel, seq // CHUNK),
        out_shape=[jax.ShapeDtypeStruct((nseq * seq, GDN_W), F32),
                   jax.ShapeDtypeStruct((nseq, 2 * GDN_HEADS, GDN_DK, GDN_DK), F32)],
        grid=(nseq,),
        in_specs=[tok(GDN_W), tok(GDN_W), tok(GDN_W), tok(LANE), sblk],
        out_specs=[tok(GDN_W), sblk],
        scratch_shapes=[pltpu.VMEM((2 * GDN_HEADS, GDN_DK, GDN_DK), F32)],
        compiler_params=_params(("parallel",)),
        name="gdn_lat" if seq == DEC_SEQ else "gdn_ctx",
    )(q, k, v, bg, s0)


def _first_max(vals):
    sel = []
    taken = None
    for a, va in enumerate(vals):
        is_max = None
        for b, vb in enumerate(vals):
            if a == b:
                continue
            c = va >= vb
            is_max = c if is_max is None else (is_max & c)
        if taken is not None:
            is_max = is_max & jnp.logical_not(taken)
        sel.append(is_max)
        taken = is_max if taken is None else (taken | is_max)
    return sel


def _gate_rows(lt):
    gl = [lt[a:a + 1, :] for a in range(N_GROUPS)]
    gsel = _first_max(gl)
    gmax = functools.reduce(jnp.maximum, gl)
    gden = functools.reduce(lambda x, y: x + y, [jnp.exp(x - gmax) for x in gl])
    g_w = 1.0 / gden
    el = []
    for kx in range(EXP_PER_GROUP):
        acc = None
        for a in range(N_GROUPS):
            row = N_GROUPS + a * EXP_PER_GROUP + kx
            term = jnp.where(gsel[a], lt[row:row + 1, :], 0.0)
            acc = term if acc is None else acc + term
        el.append(acc)
    emax = functools.reduce(jnp.maximum, el)
    ee = [jnp.exp(x - emax) for x in el]
    eden = functools.reduce(lambda x, y: x + y, ee)
    ep = [x / eden for x in ee]
    top1 = _first_max(ep)
    ep2 = [jnp.where(top1[kx], -1.0, ep[kx]) for kx in range(EXP_PER_GROUP)]
    top2 = _first_max(ep2)
    chosen = [top1[kx] | top2[kx] for kx in range(EXP_PER_GROUP)]
    wsum = functools.reduce(lambda x, y: x + y,
                            [jnp.where(chosen[kx], ep[kx], 0.0) for kx in range(EXP_PER_GROUP)])
    within = [jnp.where(chosen[kx], ep[kx] / wsum, 0.0) for kx in range(EXP_PER_GROUP)]
    rows = []
    for a in range(N_GROUPS):
        for kx in range(EXP_PER_GROUP):
            rows.append(jnp.where(gsel[a], g_w * within[kx], 0.0))
    return rows


def _outproj_kernel(idx_ref, x_ref, ona_ref, og_ref, zg_ref, mod_ref, wo_ref, gng_ref, n2g_ref,
                    wr_ref, br_ref, x1_ref, h2_ref, gate_ref):
    del idx_ref
    m = mod_ref[...]
    og = og_ref[...]
    zg = zg_ref[...]
    parts = []
    for hd in range(GDN_HEADS):
        cols = slice(hd * GDN_DK, (hd + 1) * GDN_DK)
        oh = og[:, cols]
        oh = oh * lax.rsqrt(jnp.mean(oh * oh, axis=-1, keepdims=True) + EPS) * gng_ref[...]
        parts.append((oh * _silu(zg[:, cols])).astype(BF16))
    mix = _dot(ona_ref[...].astype(BF16), wo_ref[0:NA_W, :])
    for hd in range(GDN_HEADS):
        r0 = NA_W + hd * GDN_DK
        mix = mix + _dot(parts[hd], wo_ref[r0:r0 + GDN_DK, :])
    x1 = x_ref[...] + m[2:3, :] * mix
    x1_ref[...] = x1
    y = x1 * lax.rsqrt(jnp.mean(x1 * x1, axis=-1, keepdims=True) + EPS) * n2g_ref[...]
    h2 = y * (1.0 + m[4:5, :]) + m[3:4, :]
    h2_ref[...] = h2.astype(BF16)
    logits = _dot3(h2, wr_ref[...]) + br_ref[...]
    rows = _gate_rows(logits.T)
    rid = lax.broadcasted_iota(jnp.int32, (N_EXPERTS, TM), 0)
    gate_t = jnp.zeros((N_EXPERTS, TM), F32)
    for ex in range(N_EXPERTS):
        gate_t = jnp.where(rid == ex, rows[ex], gate_t)
    gate_t = jnp.concatenate([gate_t, jnp.zeros((LANE - N_EXPERTS, TM), F32)], axis=0)
    gate_ref[...] = gate_t.T


def _outproj(tile_mod, x, o_na, o_gdn, zg, mod, w_out, gng, n2g, w_r, b_r):
    tok = lambda w: pl.BlockSpec((TM, w), lambda t, idx: (t, 0))
    full = lambda a: pl.BlockSpec(a.shape, lambda t, idx: (0,) * a.ndim)
    return pl.pallas_call(
        _outproj_kernel,
        out_shape=[jax.ShapeDtypeStruct((N_TOK, D_MODEL), F32),
                   jax.ShapeDtypeStruct((N_TOK, D_MODEL), BF16),
                   jax.ShapeDtypeStruct((N_TOK, LANE), F32)],
        grid_spec=pltpu.PrefetchScalarGridSpec(
            num_scalar_prefetch=1,
            grid=(N_TOK // TM,),
            in_specs=[
                tok(D_MODEL), tok(NA_W), tok(GDN_W), tok(GDN_W),
                pl.BlockSpec((None, 6, D_MODEL), lambda t, idx: (idx[t], 0, 0)),
                full(w_out), full(gng), full(n2g), full(w_r), full(b_r),
            ],
            out_specs=[tok(D_MODEL), tok(D_MODEL), tok(LANE)],
        ),
        compiler_params=_params(("parallel",)),
        name="outproj",
    )(tile_mod, x, o_na, o_gdn, zg, mod, w_out, gng, n2g, w_r, b_r)


def _moe_kernel(idx_ref, x1_ref, h2_ref, gate_ref, mod_ref, w1_ref, w3_ref, w2_ref, o_ref, acc_ref):
    del idx_ref
    e = pl.program_id(1)

    @pl.when(e == 0)
    def _():
        acc_ref[...] = jnp.zeros(acc_ref.shape, F32)

    h = h2_ref[...]
    a = _dot(h, w1_ref[...].astype(BF16))
    b = _dot(h, w3_ref[...].astype(BF16))
    gate = gate_ref[...]
    lane = lax.broadcasted_iota(jnp.int32, gate.shape, 1)
    gcol = jnp.sum(jnp.where(lane == e, gate, 0.0), axis=-1, keepdims=True)
    s = (_silu(a) * b * gcol).astype(BF16)
    acc_ref[...] += _dot(s, w2_ref[...].astype(BF16))

    @pl.when(e == N_EXPERTS - 1)
    def _():
        o_ref[...] = x1_ref[...] + mod_ref[5:6, :] * acc_ref[...]


def _moe(layer, tile_mod, x1, h2, gate, mod, w1, w3, w2):
    tok = lambda w: pl.BlockSpec((TM_MOE, w), lambda t, e, idx: (t, 0))
    return pl.pallas_call(
        _moe_kernel,
        out_shape=jax.ShapeDtypeStruct((N_TOK, D_MODEL), F32),
        grid_spec=pltpu.PrefetchScalarGridSpec(
            num_scalar_prefetch=1,
            grid=(N_TOK // TM_MOE, N_EXPERTS),
            in_specs=[
                tok(D_MODEL), tok(D_MODEL), tok(LANE),
                pl.BlockSpec((None, 6, D_MODEL), lambda t, e, idx: (idx[t], 0, 0)),
                pl.BlockSpec((None, None, D_MODEL, EXPERT_FF), lambda t, e, idx: (layer, e, 0, 0)),
                pl.BlockSpec((None, None, D_MODEL, EXPERT_FF), lambda t, e, idx: (layer, e, 0, 0)),
                pl.BlockSpec((None, None, EXPERT_FF, D_MODEL), lambda t, e, idx: (layer, e, 0, 0)),
            ],
            out_specs=tok(D_MODEL),
            scratch_shapes=[pltpu.VMEM((TM_MOE, D_MODEL), F32)],
        ),
        compiler_params=_params(("parallel", "arbitrary")),
        name="moe",
    )(tile_mod, x1, h2, gate, mod, w1, w3, w2)


def _tile_mod_index(tile):
    t = np.arange(N_TOK // tile) * tile
    return jnp.asarray(np.where(t < N_CTX, 0, 1 + (t - N_CTX) // DEC_SEQ), jnp.int32)


def kernel(x_prompt, x_sample, c, cache_k, cache_v, state_ssm, c_ctx, ada_w, ada_b, norm1_g, norm2_g, w_in, w_out, na_qn_g, na_kn_g, na_rpb, gdn_conv_w, gdn_a_log, gdn_dt_bias, gdn_norm_g, moe_w_rg, moe_b_rg, moe_w_re, moe_b_re, moe_w1, moe_w3, moe_w2):
    x = jnp.concatenate([x_prompt.reshape(N_CTX, D_MODEL), x_sample.reshape(N_LAT, D_MODEL)], axis=0)
    cv = jnp.concatenate([c_ctx[None, :], c, jnp.zeros((N_MOD_PAD - N_MOD, D_MODEL), F32)], axis=0)
    mod_all = _modulation(cv, ada_w, ada_b).reshape(DEPTH, N_MOD_PAD, 6, D_MODEL)

    idx_tm = _tile_mod_index(TM)
    idx_moe = _tile_mod_index(TM_MOE)
    hh = np.arange(NA_W) // NA_HD
    ones_bd = jnp.asarray(hh[:, None] == hh[None, :], BF16)
    rope = _rope_tables()
    cache_k4 = cache_k.reshape(DEC_BATCH, DEPTH, PAST_LEN, NA_W)
    cache_v4 = cache_v.reshape(DEC_BATCH, DEPTH, PAST_LEN, NA_W)
    lane_pad = lambda a, at: jnp.zeros((1, LANE), F32).at[0, at:at + a.size].set(a.reshape(-1))
    zeros_state = jnp.zeros((BATCH, 2 * GDN_HEADS, GDN_DK, GDN_DK), F32)

    ks, vs, ss = [], [], []
    for l in range(DEPTH):
        mod = mod_all[l]
        w_main = w_in[l, :, :N_MAIN].astype(BF16)
        w_ba = jnp.zeros((D_MODEL, LANE), BF16).at[:, :N_IN - N_MAIN].set(w_in[l, :, N_MAIN:].astype(BF16))
        qg = jnp.tile(na_qn_g[l], NA_HEADS)[None, :]
        kg = jnp.tile(na_kn_g[l], NA_HEADS)[None, :]
        q, k, v, zqkv, zg, zba = _inproj(idx_tm, x, mod, norm1_g[l][None, :], w_main, w_ba, ones_bd, qg, kg)

        o_ctx = _ctx_attention(q, k, v)
        bias = _na_bias_table(na_rpb[l])
        o_lat = _na_attention(l, q, k, v, cache_k4, cache_v4, bias)
        o_na = jnp.concatenate([o_ctx, o_lat], axis=0)

        conv_w = jnp.zeros((8, 3 * GDN_W), F32).at[:CONV_K].set(gdn_conv_w[l])
        a_row = lane_pad(gdn_a_log[l], 2 * GDN_HEADS)
        dt_row = lane_pad(gdn_dt_bias[l], 2 * GDN_HEADS)
        gq_c, gk_c, gv_c, bg_c = _gdn_prep(False, zqkv, zba, conv_w, a_row, dt_row, None)
        gq_l, gk_l, gv_l, bg_l = _gdn_prep(True, zqkv, zba, conv_w, a_row, dt_row, rope)
        og_c, s_ctx = _gdn(SEQ, BATCH, gq_c, gk_c, gv_c, bg_c, zeros_state)
        s0_lat = state_ssm[:, l].reshape(DEC_BATCH, 2 * GDN_HEADS, GDN_DK, GDN_DK)
        og_l, _ = _gdn(DEC_SEQ, DEC_BATCH, gq_l, gk_l, gv_l, bg_l, s0_lat)
        o_gdn = jnp.concatenate([og_c, og_l], axis=0)

        w_r = jnp.zeros((D_MODEL, LANE), F32).at[:, :N_GROUPS].set(moe_w_rg[l])
        w_r = w_r.at[:, N_GROUPS:N_GROUPS + N_EXPERTS].set(moe_w_re[l])
        b_r = lane_pad(jnp.concatenate([moe_b_rg[l], moe_b_re[l]]), 0)
        x1, h2, gate = _outproj(idx_tm, x, o_na, o_gdn, zg, mod, w_out[l].astype(BF16),
                                gdn_norm_g[l][None, :], norm2_g[l][None, :], w_r, b_r)
        x = _moe(l, idx_moe, x1, h2, gate, mod, moe_w1, moe_w3, moe_w2)

        ks.append(k[:N_CTX].reshape(BATCH, SEQ, NA_HEADS, NA_HD))
        vs.append(v[:N_CTX].reshape(BATCH, SEQ, NA_HEADS, NA_HD))
        ss.append(s_ctx.reshape(BATCH, 2, GDN_HEADS, GDN_DK, GDN_DK))

    y_prompt = x[:N_CTX].reshape(BATCH, SEQ, D_MODEL)
    y_sample = x[N_CTX:].reshape(DEC_BATCH, DEC_SEQ, D_MODEL)
    return (y_prompt, y_sample, jnp.stack(ks, axis=1), jnp.stack(vs, axis=1), jnp.stack(ss, axis=1))
```

```python
import functools

import jax
import jax.numpy as jnp
import numpy as np
from jax import lax
from jax.experimental import pallas as pl
from jax.experimental.pallas import tpu as pltpu

F32 = jnp.float32
BF16 = jnp.bfloat16

D_MODEL = 1024
BATCH = 16
SEQ = 256
DEPTH = 2
DEC_BATCH = 8
DEC_SEQ = 1024
PAST_LEN = 512
GRID_W = 64
GRID_ROWS = DEC_SEQ // GRID_W
NA_HEADS = 8
NA_HD = 64
NA_W = NA_HEADS * NA_HD
NA_KH = 8
NA_KW = 16
GDN_HEADS = 4
GDN_DK = 128
GDN_W = GDN_HEADS * GDN_DK
CONV_K = 5
CHUNK = 64
ROPE_BASE = 10000.0
N_GROUPS = 4
EXP_PER_GROUP = 4
N_EXPERTS = 16
EXPERT_FF = 512
EPS = 1e-6
N_IN = 3 * NA_W + 4 * GDN_W + 4 * GDN_HEADS

N_CTX = BATCH * SEQ
N_LAT = DEC_BATCH * DEC_SEQ
N_TOK = N_CTX + N_LAT
N_MOD = 1 + DEC_BATCH
N_MOD_PAD = 16
LANE = 128
N_MAIN = 3 * NA_W + 4 * GDN_W
TM = 256
TM_MOE = 1024
VMEM_LIMIT = 56 * 1024 * 1024


def _dot(a, b):
    return jnp.dot(a, b, preferred_element_type=F32)


def _dot_nt(a, b):
    return lax.dot_general(a, b, (((1,), (1,)), ((), ())), preferred_element_type=F32)


def _dot_tn(a, b):
    return lax.dot_general(a, b, (((0,), (0,)), ((), ())), preferred_element_type=F32)


def _split2(x):
    hi = x.astype(BF16)
    lo = (x - hi.astype(F32)).astype(BF16)
    return hi, lo


def _split3(x):
    hi = x.astype(BF16)
    r = x - hi.astype(F32)
    mid = r.astype(BF16)
    lo = (r - mid.astype(F32)).astype(BF16)
    return hi, mid, lo


def _dot3(a, b):
    ah, al = _split2(a)
    bh, bl = _split2(b)
    return _dot(ah, bh) + (_dot(ah, bl) + _dot(al, bh))


def _dot_exact_lhs(mask_bf16, x):
    hi, mid, lo = _split3(x)
    return _dot(mask_bf16, hi) + (_dot(mask_bf16, mid) + _dot(mask_bf16, lo))


def _silu(x):
    return x * (1.0 / (1.0 + jnp.exp(-x)))


def _params(sem):
    return pltpu.CompilerParams(dimension_semantics=sem, vmem_limit_bytes=VMEM_LIMIT)


def _mod_kernel(cv_ref, w_ref, b_ref, o_ref):
    cv = cv_ref[...]
    s = _silu(cv)
    o_ref[...] = jnp.dot(s, w_ref[...], preferred_element_type=F32,
                         precision=lax.Precision.HIGHEST) + b_ref[...]


def _modulation(cv, ada_w, ada_b):
    nblk = 512
    return pl.pallas_call(
        _mod_kernel,
        out_shape=jax.ShapeDtypeStruct((DEPTH, N_MOD_PAD, 6 * D_MODEL), F32),
        grid=(DEPTH, 6 * D_MODEL // nblk),
        in_specs=[
            pl.BlockSpec((N_MOD_PAD, D_MODEL), lambda l, j: (0, 0)),
            pl.BlockSpec((None, D_MODEL, nblk), lambda l, j: (l, 0, j)),
            pl.BlockSpec((None, 1, nblk), lambda l, j: (l, 0, j)),
        ],
        out_specs=pl.BlockSpec((None, N_MOD_PAD, nblk), lambda l, j: (l, 0, j)),
        compiler_params=_params(("parallel", "parallel")),
        name="modulation",
    )(cv, ada_w, ada_b.reshape(DEPTH, 1, 6 * D_MODEL))


def _inproj_kernel(idx_ref, x_ref, mod_ref, n1g_ref, win_ref, wba_ref, ones_ref, qg_ref, kg_ref,
                   q_ref, k_ref, v_ref, zqkv_ref, zg_ref, zba_ref):
    del idx_ref
    x = x_ref[...]
    y = x * lax.rsqrt(jnp.mean(x * x, axis=-1, keepdims=True) + EPS) * n1g_ref[...]
    m = mod_ref[...]
    h = (y * (1.0 + m[1:2, :]) + m[0:1, :]).astype(BF16)
    z = _dot(h, win_ref[...])
    zba_ref[...] = _dot(h, wba_ref[...])
    ones = ones_ref[...]

    def head_rms(zz, g):
        hi, lo = _split2(zz * zz)
        ms = (_dot(hi, ones) + _dot(lo, ones)) * (1.0 / NA_HD)
        return zz * lax.rsqrt(ms + EPS) * g

    q_ref[...] = head_rms(z[:, 0:NA_W], qg_ref[...]) * (NA_HD ** -0.5)
    k_ref[...] = head_rms(z[:, NA_W:2 * NA_W], kg_ref[...])
    v_ref[...] = z[:, 2 * NA_W:3 * NA_W]
    zqkv_ref[...] = z[:, 3 * NA_W:3 * NA_W + 3 * GDN_W]
    zg_ref[...] = z[:, 3 * NA_W + 3 * GDN_W:N_MAIN]


def _inproj(tile_mod, x, mod, n1g, w_main, w_ba, ones_bd, qg, kg):
    tok = lambda w: pl.BlockSpec((TM, w), lambda t, idx: (t, 0))
    full = lambda a: pl.BlockSpec(a.shape, lambda t, idx: (0,) * a.ndim)
    out_w = (NA_W, NA_W, NA_W, 3 * GDN_W, GDN_W, LANE)
    return pl.pallas_call(
        _inproj_kernel,
        out_shape=[jax.ShapeDtypeStruct((N_TOK, w), F32) for w in out_w],
        grid_spec=pltpu.PrefetchScalarGridSpec(
            num_scalar_prefetch=1,
            grid=(N_TOK // TM,),
            in_specs=[
                tok(D_MODEL),
                pl.BlockSpec((None, 6, D_MODEL), lambda t, idx: (idx[t], 0, 0)),
                full(n1g), full(w_main), full(w_ba), full(ones_bd), full(qg), full(kg),
            ],
            out_specs=[tok(w) for w in out_w],
        ),
        compiler_params=_params(("parallel",)),
        name="inproj",
    )(tile_mod, x, mod, n1g, w_main, w_ba, ones_bd, qg, kg)


def _pair_masks(shape):
    lane = lax.broadcasted_iota(jnp.int32, shape, 1)
    return lane < NA_HD


def _ctx_attn_kernel(q_ref, k_ref, v_ref, o_ref):
    for p in range(NA_HEADS // 2):
        cols = slice(p * LANE, (p + 1) * LANE)
        q = q_ref[:, cols]
        kb = k_ref[:, cols].astype(BF16)
        vb = v_ref[:, cols].astype(BF16)
        first = _pair_masks(q.shape)
        outs = []
        for hm in (first, jnp.logical_not(first)):
            s = _dot_nt(jnp.where(hm, q, 0.0).astype(BF16), kb)
            e = jnp.exp(s - jnp.max(s, axis=-1, keepdims=True))
            o = _dot(e.astype(BF16), vb)
            outs.append(o / jnp.sum(e, axis=-1, keepdims=True))
        o_ref[:, cols] = jnp.where(first, outs[0], outs[1])


def _ctx_attention(q, k, v):
    blk = pl.BlockSpec((SEQ, NA_W), lambda b: (b, 0))
    return pl.pallas_call(
        _ctx_attn_kernel,
        out_shape=jax.ShapeDtypeStruct((N_CTX, NA_W), F32),
        grid=(BATCH,),
        in_specs=[blk, blk, blk],
        out_specs=blk,
        compiler_params=_params(("parallel",)),
        name="ctx_attention",
    )(q, k, v)


def _na_kernel(q_ref, k_ref, v_ref, kc_ref, vc_ref, bias_ref, o_ref):
    r = pl.program_id(1)
    base = jnp.clip(r - NA_KH // 2, 0, GRID_ROWS - NA_KH)
    dr0 = base - r + (NA_KH - 1)
    row0 = pl.multiple_of(base * GRID_W, GRID_W)
    nwin = NA_KH * GRID_W
    for p in range(NA_HEADS // 2):
        cols = slice(p * LANE, (p + 1) * LANE)
        q = q_ref[:, cols]
        kw = k_ref[pl.ds(row0, nwin), cols].astype(BF16)
        vw = v_ref[pl.ds(row0, nwin), cols].astype(BF16)
        kc = kc_ref[:, cols].astype(BF16)
        vc = vc_ref[:, cols].astype(BF16)
        first = _pair_masks(q.shape)
        outs = []
        for j, hm in enumerate((first, jnp.logical_not(first))):
            qh = jnp.where(hm, q, 0.0).astype(BF16)
            s_loc = _dot_nt(qh, kw) + bias_ref[2 * p + j, dr0]
            s_ctx = _dot_nt(qh, kc)
            mx = jnp.maximum(jnp.max(s_loc, axis=-1, keepdims=True),
                             jnp.max(s_ctx, axis=-1, keepdims=True))
            e_loc = jnp.exp(s_loc - mx)
            e_ctx = jnp.exp(s_ctx - mx)
            den = jnp.sum(e_loc, axis=-1, keepdims=True) + jnp.sum(e_ctx, axis=-1, keepdims=True)
            o = _dot(e_loc.astype(BF16), vw) + _dot(e_ctx.astype(BF16), vc)
            outs.append(o / den)
        o_ref[:, cols] = jnp.where(first, outs[0], outs[1])


def _na_attention(layer, q, k, v, cache_k, cache_v, bias):
    lat0 = N_CTX // DEC_SEQ
    qblk = pl.BlockSpec((GRID_W, NA_W), lambda b, r: (N_CTX // GRID_W + b * GRID_ROWS + r, 0))
    kvblk = pl.BlockSpec((DEC_SEQ, NA_W), lambda b, r: (lat0 + b, 0))
    cblk = pl.BlockSpec((None, None, PAST_LEN, NA_W), lambda b, r: (b, layer, 0, 0))
    return pl.pallas_call(
        _na_kernel,
        out_shape=jax.ShapeDtypeStruct((N_LAT, NA_W), F32),
        grid=(DEC_BATCH, GRID_ROWS),
        in_specs=[qblk, kvblk, kvblk, cblk, cblk,
                  pl.BlockSpec(bias.shape, lambda b, r: (0, 0, 0, 0))],
        out_specs=pl.BlockSpec((GRID_W, NA_W), lambda b, r: (b * GRID_ROWS + r, 0)),
        compiler_params=_params(("parallel", "arbitrary")),
        name="na_attention",
    )(q, k, v, cache_k, cache_v, bias)


N_DC = 2 * NA_KW - 1


def _na_bias_kernel(r_ref, o_ref):
    shape = (GRID_W, NA_KH * GRID_W)
    qc = lax.broadcasted_iota(jnp.int32, shape, 0)
    kc = lax.broadcasted_iota(jnp.int32, shape, 1) & (GRID_W - 1)
    ws = jnp.clip(qc - NA_KW // 2, 0, GRID_W - NA_KW)
    valid = (kc >= ws) & (kc < ws + NA_KW)
    dc = jnp.where(valid, kc - qc + (NA_KW - 1), -1)
    acc = jnp.full(shape, -jnp.inf, F32)
    for d in range(N_DC):
        acc = jnp.where(dc == d, r_ref[d:d + 1, :], acc)
    o_ref[...] = acc


def _na_bias_table(rpb):
    win = jnp.stack([rpb[:, d0:d0 + NA_KH, :] for d0 in range(NA_KH)], axis=1)
    rexp = jnp.repeat(win.transpose(0, 1, 3, 2), GRID_W, axis=-1)
    rexp = jnp.pad(rexp, ((0, 0), (0, 0), (0, 32 - N_DC), (0, 0)))
    blk = lambda rows: pl.BlockSpec((None, None, rows, NA_KH * GRID_W), lambda h, d: (h, d, 0, 0))
    return pl.pallas_call(
        _na_bias_kernel,
        out_shape=jax.ShapeDtypeStruct((NA_HEADS, NA_KH, GRID_W, NA_KH * GRID_W), F32),
        grid=(NA_HEADS, NA_KH),
        in_specs=[blk(32)],
        out_specs=blk(GRID_W),
        compiler_params=_params(("parallel", "parallel")),
        name="na_bias",
    )(rexp)


def _rope_tables():
    t = np.arange(DEC_SEQ)
    half = GDN_DK // 2
    inv_freq = (np.float32(ROPE_BASE) ** (-np.arange(0, half, 2, dtype=np.float32) / np.float32(half)))
    ang_r = (t // GRID_W).astype(np.float32)[:, None] * inv_freq
    ang_c = (t % GRID_W).astype(np.float32)[:, None] * inv_freq
    cr, sr, cc, sc = np.cos(ang_r), np.sin(ang_r), np.cos(ang_c), np.sin(ang_c)
    z = np.zeros_like(sr)
    cos = np.concatenate([cr, cr, cc, cc], axis=1)
    s_up = np.concatenate([-sr, z, -sc, z], axis=1)
    s_dn = np.concatenate([z, sr, z, sc], axis=1)
    return (jnp.asarray(cos, F32), jnp.asarray(s_up, F32), jnp.asarray(s_dn, F32))


HALO = 8
PREP_CHUNKS = 4
N_STREAM = 2 * GDN_HEADS
LOG_COL = N_STREAM


def _gdn_prep_kernel(axial, seq, *refs):
    if axial:
        (zqkv_ref, zba_ref, cw_ref, alog_ref, dtb_ref, cos_ref, sup_ref, sdn_ref,
         a_ref, qk_ref, r_ref, qd_ref, kd_ref, gl_ref, xs_ref) = refs
    else:
        (zqkv_ref, zba_ref, cw_ref, alog_ref, dtb_ref,
         a_ref, qk_ref, r_ref, qd_ref, kd_ref, gl_ref, xs_ref) = refs
    width = 3 * GDN_W
    part = pl.program_id(1)

    @pl.when(part == 0)
    def _():
        xs_ref[0:HALO, :] = jnp.zeros((HALO, width), F32)
        xs_ref[HALO + seq:HALO + seq + HALO, :] = jnp.zeros((HALO, width), F32)
        xs_ref[HALO:HALO + seq, :] = zqkv_ref[...]

    ri = lax.broadcasted_iota(jnp.int32, (CHUNK, CHUNK), 0)
    ci = lax.broadcasted_iota(jnp.int32, (CHUNK, CHUNK), 1)
    incl = (ri >= ci, ri <= ci)
    strict = (ri > ci, ri < ci)
    lane = lax.broadcasted_iota(jnp.int32, (CHUNK, LANE), 1)

    def body(i, carry):
        r0 = pl.multiple_of((part * PREP_CHUNKS + i) * CHUNK, CHUNK)
        rows = pl.ds(r0, CHUNK)
        win = xs_ref[pl.ds(r0, CHUNK + 2 * HALO), :]
        acc = None
        for j in range(CONV_K):
            lo = HALO + j - CONV_K // 2
            tap = win[lo:lo + CHUNK, :] * cw_ref[j:j + 1, :]
            acc = tap if acc is None else acc + tap
        act = _silu(acc)

        zba = zba_ref[rows, :]
        beta_all = 1.0 / (1.0 + jnp.exp(-zba))
        xa = zba + dtb_ref[...]
        softplus = jnp.maximum(xa, 0.0) + jnp.log(1.0 + jnp.exp(-jnp.abs(xa)))
        bg = jnp.where(lane < LOG_COL, beta_all, -jnp.exp(alog_ref[...]) * softplus)

        heads = []
        for hd in range(GDN_HEADS):
            qk_pair = []
            for which in range(2):
                c0 = which * GDN_W + hd * GDN_DK
                xh = act[:, c0:c0 + GDN_DK]
                xh = xh * lax.rsqrt(jnp.sum(xh * xh, axis=-1, keepdims=True) + EPS)
                if axial:
                    xh = (xh * cos_ref[rows, :]
                          + pltpu.roll(xh, GDN_DK - GDN_DK // 4, 1) * sup_ref[rows, :]
                          + pltpu.roll(xh, GDN_DK // 4, 1) * sdn_ref[rows, :])
                qk_pair.append(xh)
            q = qk_pair[0] * (GDN_DK ** -0.5)
            k = qk_pair[1]
            v = act[:, 2 * GDN_W + hd * GDN_DK:2 * GDN_W + (hd + 1) * GDN_DK]
            kbf = k.astype(BF16)
            heads.append((q, k, v, _dot_nt(kbf, kbf), _dot_nt(q.astype(BF16), kbf)))

        for d in range(2):
            gcum = _dot_exact_lhs(incl[d].astype(BF16), bg)
            gcum_t = gcum.T
            last = gcum[0:1, :] if d else gcum[CHUNK - 1:CHUNK, :]
            e_in = jnp.exp(gcum)
            e_out = jnp.exp(last - gcum)
            gl_ref[i, d:d + 1, :] = jnp.exp(last)
            for hd in range(GDN_HEADS):
                q, k, v, kk, qk = heads[hd]
                cb = d * GDN_HEADS + hd
                cg = LOG_COL + cb
                beta = bg[:, cb:cb + 1]
                eg = e_in[:, cg:cg + 1]
                el = e_out[:, cg:cg + 1]
                diff = gcum[:, cg:cg + 1] - gcum_t[cg:cg + 1, :]
                decay = jnp.where(incl[d], jnp.exp(jnp.where(incl[d], diff, 0.0)), 0.0)
                a_ref[d, i, hd] = jnp.where(strict[d], beta * kk * decay, 0.0)
                qk_ref[d, i, hd] = (qk * decay).astype(BF16)
                kb = k * beta
                r_ref[d, i, hd, :, 0:GDN_DK] = (v * beta).astype(BF16)
                r_ref[d, i, hd, :, GDN_DK:2 * GDN_DK] = (kb * eg).astype(BF16)
                qd_ref[d, i, hd] = (q * eg).astype(BF16)
                kd_ref[d, i, hd] = (k * el).astype(BF16)
        return carry

    lax.fori_loop(0, PREP_CHUNKS, body, 0)


def _gdn_prep(axial, zqkv, zba, conv_w, a_log, dt_bias, rope):
    seq = DEC_SEQ if axial else SEQ
    nseq = DEC_BATCH if axial else BATCH
    nparts = seq // (PREP_CHUNKS * CHUNK)
    nchunk = nseq * seq // CHUNK
    blk0 = N_CTX // DEC_SEQ if axial else 0
    tok = lambda w: pl.BlockSpec((seq, w), lambda b, p: (blk0 + b, 0))
    full = lambda a: pl.BlockSpec(a.shape, lambda b, p: (0,) * a.ndim)
    ins = [zqkv, zba, conv_w, a_log, dt_bias]
    specs = [tok(3 * GDN_W), tok(LANE), full(conv_w), full(a_log), full(dt_bias)]
    if axial:
        ins += list(rope)
        specs += [full(t) for t in rope]
    tile = lambda w: pl.BlockSpec((2, PREP_CHUNKS, GDN_HEADS, CHUNK, w), lambda b, p: (0, b * nparts + p, 0, 0, 0))
    shape = lambda w, dt: jax.ShapeDtypeStruct((2, nchunk, GDN_HEADS, CHUNK, w), dt)
    return pl.pallas_call(
        functools.partial(_gdn_prep_kernel, axial, seq),
        out_shape=[shape(CHUNK, F32), shape(CHUNK, BF16), shape(2 * GDN_DK, BF16),
                   shape(GDN_DK, BF16), shape(GDN_DK, BF16),
                   jax.ShapeDtypeStruct((nchunk, 2, LANE), F32)],
        grid=(nseq, nparts),
        in_specs=specs,
        out_specs=[tile(CHUNK), tile(CHUNK), tile(2 * GDN_DK), tile(GDN_DK), tile(GDN_DK),
                   pl.BlockSpec((PREP_CHUNKS, 2, LANE), lambda b, p: (b * nparts + p, 0, 0))],
        scratch_shapes=[pltpu.VMEM((seq + 2 * HALO, 3 * GDN_W), F32)],
        compiler_params=_params(("parallel", "arbitrary")),
        name="gdn_prep_lat" if axial else "gdn_prep_ctx",
    )(*ins)


TRI_BLK = 8
TRI_ELEMS = CHUNK * CHUNK


def _tri_inverse_kernel(groups_per_dir, a_ref, o_ref, at_ref, tt_ref):
    backward = pl.program_id(0) >= groups_per_dir
    for blk in range(TRI_ELEMS // LANE):
        cols = slice(blk * LANE, (blk + 1) * LANE)
        at_ref[cols, :] = a_ref[:, cols].T
    tt_ref[...] = jnp.zeros(tt_ref.shape, F32)
    cidx = lax.broadcasted_iota(jnp.int32, (CHUNK, LANE), 0)

    def phys(idx):
        return jnp.where(backward, CHUNK - 1 - idx, idx)

    for ib in range(CHUNK // TRI_BLK):
        def row_body(ii, carry, ib=ib):
            i = phys(ib * TRI_BLK + ii)
            acc = (cidx == i).astype(F32)
            for jl in range((ib + 1) * TRI_BLK):
                j = phys(jl)
                arow = at_ref[pl.ds(i * CHUNK + j, 1), :]
                acc = acc - arow * tt_ref[pl.ds(pl.multiple_of(j * CHUNK, CHUNK), CHUNK), :]
            tt_ref[pl.ds(pl.multiple_of(i * CHUNK, CHUNK), CHUNK), :] = acc
            return carry

        lax.fori_loop(0, TRI_BLK, row_body, 0)

    for blk in range(TRI_ELEMS // LANE):
        cols = slice(blk * LANE, (blk + 1) * LANE)
        o_ref[:, cols] = tt_ref[cols, :].T.astype(BF16)


def _tri_inverse(a):
    nprob = a.shape[1] * a.shape[2]
    groups_per_dir = nprob // LANE
    blk = pl.BlockSpec((LANE, TRI_ELEMS), lambda g: (g, 0))
    out = pl.pallas_call(
        functools.partial(_tri_inverse_kernel, groups_per_dir),
        out_shape=jax.ShapeDtypeStruct((2 * nprob, TRI_ELEMS), BF16),
        grid=(2 * groups_per_dir,),
        in_specs=[blk],
        out_specs=blk,
        scratch_shapes=[pltpu.VMEM((TRI_ELEMS, LANE), F32), pltpu.VMEM((TRI_ELEMS, LANE), F32)],
        compiler_params=_params(("parallel",)),
        name="tri_inverse",
    )(a.reshape(2 * nprob, TRI_ELEMS))
    return out.reshape(a.shape)


def _gdn_scan_kernel(nchunk, t_ref, qk_ref, r_ref, qd_ref, kd_ref, gl_ref, s0_ref, o_ref, sfin_ref, state_ref):
    state_ref[...] = s0_ref[...]
    o_ref[...] = jnp.zeros(o_ref.shape, F32)

    def body(c, carry):
        for d in range(2):
            cc = (nchunk - 1 - c) if d else c
            rows = pl.ds(pl.multiple_of(cc * CHUNK, CHUNK), CHUNK)
            gl_rows = gl_ref[cc]
            for hd in range(GDN_HEADS):
                sidx = d * GDN_HEADS + hd
                cols = slice(hd * GDN_DK, (hd + 1) * GDN_DK)
                uw = _dot(t_ref[d, cc, hd], r_ref[d, cc, hd]).astype(BF16)
                mb = _dot_tn(kd_ref[d, cc, hd], uw)
                qo = _dot(qk_ref[d, cc, hd], uw)
                s = state_ref[sidx]
                sb = s.astype(BF16)
                qp = (qd_ref[d, cc, hd].astype(F32) - qo[:, GDN_DK:]).astype(BF16)
                o_ref[rows, cols] += _dot(qp, sb) + qo[:, :GDN_DK]
                gl = gl_rows[d:d + 1, LOG_COL + sidx:LOG_COL + sidx + 1]
                state_ref[sidx] = s * gl - _dot(mb[:, GDN_DK:].astype(BF16), sb) + mb[:, :GDN_DK]
        return carry

    lax.fori_loop(0, nchunk, body, 0)
    sfin_ref[...] = state_ref[...]


def _gdn_scan(seq, nseq, tinv, qk, r, qd, kd, gl, s0):
    nc = seq // CHUNK
    tile = lambda w: pl.BlockSpec((2, nc, GDN_HEADS, CHUNK, w), lambda b: (0, b, 0, 0, 0))
    sblk = pl.BlockSpec((None, N_STREAM, GDN_DK, GDN_DK), lambda b: (b, 0, 0, 0))
    oblk = pl.BlockSpec((seq, GDN_W), lambda b: (b, 0))
    return pl.pallas_call(
        functools.partial(_gdn_scan_kernel, nc),
        out_shape=[jax.ShapeDtypeStruct((nseq * seq, GDN_W), F32),
                   jax.ShapeDtypeStruct((nseq, N_STREAM, GDN_DK, GDN_DK), F32)],
        grid=(nseq,),
        in_specs=[tile(CHUNK), tile(CHUNK), tile(2 * GDN_DK), tile(GDN_DK), tile(GDN_DK),
                  pl.BlockSpec((nc, 2, LANE), lambda b: (b, 0, 0)), sblk],
        out_specs=[oblk, sblk],
        scratch_shapes=[pltpu.VMEM((N_STREAM, GDN_DK, GDN_DK), F32)],
        compiler_params=_params(("parallel",)),
        name="gdn_scan_lat" if seq == DEC_SEQ else "gdn_scan_ctx",
    )(tinv, qk, r, qd, kd, gl, s0)


def _first_max(vals):
    sel = []
    taken = None
    for a, va in enumerate(vals):
        is_max = None
        for b, vb in enumerate(vals):
            if a == b:
                continue
            c = va >= vb
            is_max = c if is_max is None else (is_max & c)
        if taken is not None:
            is_max = is_max & jnp.logical_not(taken)
        sel.append(is_max)
        taken = is_max if taken is None else (taken | is_max)
    return sel


def _gate_rows(lt):
    gl = [lt[a:a + 1, :] for a in range(N_GROUPS)]
    gsel = _first_max(gl)
    gmax = functools.reduce(jnp.maximum, gl)
    gden = functools.reduce(lambda x, y: x + y, [jnp.exp(x - gmax) for x in gl])
    g_w = 1.0 / gden
    el = []
    for kx in range(EXP_PER_GROUP):
        acc = None
        for a in range(N_GROUPS):
            row = N_GROUPS + a * EXP_PER_GROUP + kx
            term = jnp.where(gsel[a], lt[row:row + 1, :], 0.0)
            acc = term if acc is None else acc + term
        el.append(acc)
    emax = functools.reduce(jnp.maximum, el)
    ee = [jnp.exp(x - emax) for x in el]
    eden = functools.reduce(lambda x, y: x + y, ee)
    ep = [x / eden for x in ee]
    top1 = _first_max(ep)
    ep2 = [jnp.where(top1[kx], -1.0, ep[kx]) for kx in range(EXP_PER_GROUP)]
    top2 = _first_max(ep2)
    chosen = [top1[kx] | top2[kx] for kx in range(EXP_PER_GROUP)]
    wsum = functools.reduce(lambda x, y: x + y,
                            [jnp.where(chosen[kx], ep[kx], 0.0) for kx in range(EXP_PER_GROUP)])
    within = [jnp.where(chosen[kx], ep[kx] / wsum, 0.0) for kx in range(EXP_PER_GROUP)]
    rows = []
    for a in range(N_GROUPS):
        for kx in range(EXP_PER_GROUP):
            rows.append(jnp.where(gsel[a], g_w * within[kx], 0.0))
    return rows


def _outproj_kernel(idx_ref, x_ref, ona_ref, og_ref, zg_ref, mod_ref, wo_ref, gng_ref, n2g_ref,
                    wr_ref, br_ref, x1_ref, h2_ref, gate_ref):
    del idx_ref
    m = mod_ref[...]
    og = og_ref[...]
    zg = zg_ref[...]
    parts = []
    for hd in range(GDN_HEADS):
        cols = slice(hd * GDN_DK, (hd + 1) * GDN_DK)
        oh = og[:, cols]
        oh = oh * lax.rsqrt(jnp.mean(oh * oh, axis=-1, keepdims=True) + EPS) * gng_ref[...]
        parts.append((oh * _silu(zg[:, cols])).astype(BF16))
    mix = _dot(ona_ref[...].astype(BF16), wo_ref[0:NA_W, :])
    for hd in range(GDN_HEADS):
        r0 = NA_W + hd * GDN_DK
        mix = mix + _dot(parts[hd], wo_ref[r0:r0 + GDN_DK, :])
    x1 = x_ref[...] + m[2:3, :] * mix
    x1_ref[...] = x1
    y = x1 * lax.rsqrt(jnp.mean(x1 * x1, axis=-1, keepdims=True) + EPS) * n2g_ref[...]
    h2 = y * (1.0 + m[4:5, :]) + m[3:4, :]
    h2_ref[...] = h2.astype(BF16)
    logits = _dot3(h2, wr_ref[...]) + br_ref[...]
    rows = _gate_rows(logits.T)
    rid = lax.broadcasted_iota(jnp.int32, (N_EXPERTS, TM), 0)
    gate_t = jnp.zeros((N_EXPERTS, TM), F32)
    for ex in range(N_EXPERTS):
        gate_t = jnp.where(rid == ex, rows[ex], gate_t)
    gate_t = jnp.concatenate([gate_t, jnp.zeros((LANE - N_EXPERTS, TM), F32)], axis=0)
    gate_ref[...] = gate_t.T


def _outproj(tile_mod, x, o_na, o_gdn, zg, mod, w_out, gng, n2g, w_r, b_r):
    tok = lambda w: pl.BlockSpec((TM, w), lambda t, idx: (t, 0))
    full = lambda a: pl.BlockSpec(a.shape, lambda t, idx: (0,) * a.ndim)
    return pl.pallas_call(
        _outproj_kernel,
        out_shape=[jax.ShapeDtypeStruct((N_TOK, D_MODEL), F32),
                   jax.ShapeDtypeStruct((N_TOK, D_MODEL), BF16),
                   jax.ShapeDtypeStruct((N_TOK, LANE), F32)],
        grid_spec=pltpu.PrefetchScalarGridSpec(
            num_scalar_prefetch=1,
            grid=(N_TOK // TM,),
            in_specs=[
                tok(D_MODEL), tok(NA_W), tok(GDN_W), tok(GDN_W),
                pl.BlockSpec((None, 6, D_MODEL), lambda t, idx: (idx[t], 0, 0)),
                full(w_out), full(gng), full(n2g), full(w_r), full(b_r),
            ],
            out_specs=[tok(D_MODEL), tok(D_MODEL), tok(LANE)],
        ),
        compiler_params=_params(("parallel",)),
        name="outproj",
    )(tile_mod, x, o_na, o_gdn, zg, mod, w_out, gng, n2g, w_r, b_r)


def _moe_kernel(idx_ref, x1_ref, h2_ref, gate_ref, mod_ref, w1_ref, w3_ref, w2_ref, o_ref, acc_ref):
    del idx_ref
    e = pl.program_id(1)

    @pl.when(e == 0)
    def _():
        acc_ref[...] = jnp.zeros(acc_ref.shape, F32)

    h = h2_ref[...]
    a = _dot(h, w1_ref[...].astype(BF16))
    b = _dot(h, w3_ref[...].astype(BF16))
    gate = gate_ref[...]
    lane = lax.broadcasted_iota(jnp.int32, gate.shape, 1)
    gcol = jnp.sum(jnp.where(lane == e, gate, 0.0), axis=-1, keepdims=True)
    s = (_silu(a) * b * gcol).astype(BF16)
    acc_ref[...] += _dot(s, w2_ref[...].astype(BF16))

    @pl.when(e == N_EXPERTS - 1)
    def _():
        o_ref[...] = x1_ref[...] + mod_ref[5:6, :] * acc_ref[...]


def _moe(layer, tile_mod, x1, h2, gate, mod, w1, w3, w2):
    tok = lambda w: pl.BlockSpec((TM_MOE, w), lambda t, e, idx: (t, 0))
    return pl.pallas_call(
        _moe_kernel,
        out_shape=jax.ShapeDtypeStruct((N_TOK, D_MODEL), F32),
        grid_spec=pltpu.PrefetchScalarGridSpec(
            num_scalar_prefetch=1,
            grid=(N_TOK // TM_MOE, N_EXPERTS),
            in_specs=[
                tok(D_MODEL), tok(D_MODEL), tok(LANE),
                pl.BlockSpec((None, 6, D_MODEL), lambda t, e, idx: (idx[t], 0, 0)),
                pl.BlockSpec((None, None, D_MODEL, EXPERT_FF), lambda t, e, idx: (layer, e, 0, 0)),
                pl.BlockSpec((None, None, D_MODEL, EXPERT_FF), lambda t, e, idx: (layer, e, 0, 0)),
                pl.BlockSpec((None, None, EXPERT_FF, D_MODEL), lambda t, e, idx: (layer, e, 0, 0)),
            ],
            out_specs=tok(D_MODEL),
            scratch_shapes=[pltpu.VMEM((TM_MOE, D_MODEL), F32)],
        ),
        compiler_params=_params(("parallel", "arbitrary")),
        name="moe",
    )(tile_mod, x1, h2, gate, mod, w1, w3, w2)


def _tile_mod_index(tile):
    t = np.arange(N_TOK // tile) * tile
    return jnp.asarray(np.where(t < N_CTX, 0, 1 + (t - N_CTX) // DEC_SEQ), jnp.int32)


def kernel(x_prompt, x_sample, c, cache_k, cache_v, state_ssm, c_ctx, ada_w, ada_b, norm1_g, norm2_g, w_in, w_out, na_qn_g, na_kn_g, na_rpb, gdn_conv_w, gdn_a_log, gdn_dt_bias, gdn_norm_g, moe_w_rg, moe_b_rg, moe_w_re, moe_b_re, moe_w1, moe_w3, moe_w2):
    x = jnp.concatenate([x_prompt.reshape(N_CTX, D_MODEL), x_sample.reshape(N_LAT, D_MODEL)], axis=0)
    cv = jnp.concatenate([c_ctx[None, :], c, jnp.zeros((N_MOD_PAD - N_MOD, D_MODEL), F32)], axis=0)
    mod_all = _modulation(cv, ada_w, ada_b).reshape(DEPTH, N_MOD_PAD, 6, D_MODEL)

    idx_tm = _tile_mod_index(TM)
    idx_moe = _tile_mod_index(TM_MOE)
    hh = np.arange(NA_W) // NA_HD
    ones_bd = jnp.asarray(hh[:, None] == hh[None, :], BF16)
    rope = _rope_tables()
    cache_k4 = cache_k.reshape(DEC_BATCH, DEPTH, PAST_LEN, NA_W)
    cache_v4 = cache_v.reshape(DEC_BATCH, DEPTH, PAST_LEN, NA_W)
    lane_pad = lambda a, at: jnp.zeros((1, LANE), F32).at[0, at:at + a.size].set(a.reshape(-1))
    zeros_state = jnp.zeros((BATCH, 2 * GDN_HEADS, GDN_DK, GDN_DK), F32)

    ks, vs, ss = [], [], []
    for l in range(DEPTH):
        mod = mod_all[l]
        w_main = w_in[l, :, :N_MAIN].astype(BF16)
        w_ba = jnp.zeros((D_MODEL, LANE), BF16).at[:, :N_IN - N_MAIN].set(w_in[l, :, N_MAIN:].astype(BF16))
        qg = jnp.tile(na_qn_g[l], NA_HEADS)[None, :]
        kg = jnp.tile(na_kn_g[l], NA_HEADS)[None, :]
        q, k, v, zqkv, zg, zba = _inproj(idx_tm, x, mod, norm1_g[l][None, :], w_main, w_ba, ones_bd, qg, kg)

        o_ctx = _ctx_attention(q, k, v)
        bias = _na_bias_table(na_rpb[l])
        o_lat = _na_attention(l, q, k, v, cache_k4, cache_v4, bias)
        o_na = jnp.concatenate([o_ctx, o_lat], axis=0)

        conv_w = jnp.zeros((8, 3 * GDN_W), F32).at[:CONV_K].set(gdn_conv_w[l])
        a_row = lane_pad(gdn_a_log[l], 2 * GDN_HEADS)
        dt_row = lane_pad(gdn_dt_bias[l], 2 * GDN_HEADS)
        a_c, *ops_c = _gdn_prep(False, zqkv, zba, conv_w, a_row, dt_row, None)
        a_l, *ops_l = _gdn_prep(True, zqkv, zba, conv_w, a_row, dt_row, rope)
        og_c, s_ctx = _gdn_scan(SEQ, BATCH, _tri_inverse(a_c), *ops_c, zeros_state)
        s0_lat = state_ssm[:, l].reshape(DEC_BATCH, N_STREAM, GDN_DK, GDN_DK)
        og_l, _ = _gdn_scan(DEC_SEQ, DEC_BATCH, _tri_inverse(a_l), *ops_l, s0_lat)
        o_gdn = jnp.concatenate([og_c, og_l], axis=0)

        w_r = jnp.zeros((D_MODEL, LANE), F32).at[:, :N_GROUPS].set(moe_w_rg[l])
        w_r = w_r.at[:, N_GROUPS:N_GROUPS + N_EXPERTS].set(moe_w_re[l])
        b_r = lane_pad(jnp.concatenate([moe_b_rg[l], moe_b_re[l]]), 0)
        x1, h2, gate = _outproj(idx_tm, x, o_na, o_gdn, zg, mod, w_out[l].astype(BF16),
                                gdn_norm_g[l][None, :], norm2_g[l][None, :], w_r, b_r)
        x = _moe(l, idx_moe, x1, h2, gate, mod, moe_w1, moe_w3, moe_w2)

        ks.append(k[:N_CTX].reshape(BATCH, SEQ, NA_HEADS, NA_HD))
        vs.append(v[:N_CTX].reshape(BATCH, SEQ, NA_HEADS, NA_HD))
        ss.append(s_ctx.reshape(BATCH, 2, GDN_HEADS, GDN_DK, GDN_DK))

    y_prompt = x[:N_CTX].reshape(BATCH, SEQ, D_MODEL)
    y_sample = x[N_CTX:].reshape(DEC_BATCH, DEC_SEQ, D_MODEL)
    return (y_prompt, y_sample, jnp.stack(ks, axis=1), jnp.stack(vs, axis=1), jnp.stack(ss, axis=1))
```

```python
import functools

import jax
import jax.numpy as jnp
import numpy as np
from jax import lax
from jax.experimental import pallas as pl
from jax.experimental.pallas import tpu as pltpu

F32 = jnp.float32
BF16 = jnp.bfloat16

D_MODEL = 1024
BATCH = 16
SEQ = 256
DEPTH = 2
DEC_BATCH = 8
DEC_SEQ = 1024
PAST_LEN = 512
GRID_W = 64
GRID_ROWS = DEC_SEQ // GRID_W
NA_HEADS = 8
NA_HD = 64
NA_W = NA_HEADS * NA_HD
NA_KH = 8
NA_KW = 16
GDN_HEADS = 4
GDN_DK = 128
GDN_W = GDN_HEADS * GDN_DK
CONV_K = 5
CHUNK = 64
ROPE_BASE = 10000.0
N_GROUPS = 4
EXP_PER_GROUP = 4
N_EXPERTS = 16
EXPERT_FF = 512
EPS = 1e-6
N_IN = 3 * NA_W + 4 * GDN_W + 4 * GDN_HEADS

N_CTX = BATCH * SEQ
N_LAT = DEC_BATCH * DEC_SEQ
N_TOK = N_CTX + N_LAT
N_MOD = 1 + DEC_BATCH
N_MOD_PAD = 16
LANE = 128
N_MAIN = 3 * NA_W + 4 * GDN_W
TM = 256
TM_MOE = 1024
VMEM_LIMIT = 56 * 1024 * 1024


def _dot(a, b):
    return jnp.dot(a, b, preferred_element_type=F32)


def _dot_nt(a, b):
    return lax.dot_general(a, b, (((1,), (1,)), ((), ())), preferred_element_type=F32)


def _dot_tn(a, b):
    return lax.dot_general(a, b, (((0,), (0,)), ((), ())), preferred_element_type=F32)


def _split2(x):
    hi = x.astype(BF16)
    lo = (x - hi.astype(F32)).astype(BF16)
    return hi, lo


def _split3(x):
    hi = x.astype(BF16)
    r = x - hi.astype(F32)
    mid = r.astype(BF16)
    lo = (r - mid.astype(F32)).astype(BF16)
    return hi, mid, lo


def _dot3(a, b):
    ah, al = _split2(a)
    bh, bl = _split2(b)
    return _dot(ah, bh) + (_dot(ah, bl) + _dot(al, bh))


def _dot_exact_lhs(mask_bf16, x):
    hi, mid, lo = _split3(x)
    return _dot(mask_bf16, hi) + (_dot(mask_bf16, mid) + _dot(mask_bf16, lo))


def _silu(x):
    return x * (1.0 / (1.0 + jnp.exp(-x)))


def _params(sem):
    return pltpu.CompilerParams(dimension_semantics=sem, vmem_limit_bytes=VMEM_LIMIT)


def _mod_kernel(cv_ref, w_ref, b_ref, o_ref):
    cv = cv_ref[...]
    s = _silu(cv)
    o_ref[...] = jnp.dot(s, w_ref[...], preferred_element_type=F32,
                         precision=lax.Precision.HIGHEST) + b_ref[...]


def _modulation(cv, ada_w, ada_b):
    nblk = 512
    return pl.pallas_call(
        _mod_kernel,
        out_shape=jax.ShapeDtypeStruct((DEPTH, N_MOD_PAD, 6 * D_MODEL), F32),
        grid=(DEPTH, 6 * D_MODEL // nblk),
        in_specs=[
            pl.BlockSpec((N_MOD_PAD, D_MODEL), lambda l, j: (0, 0)),
            pl.BlockSpec((None, D_MODEL, nblk), lambda l, j: (l, 0, j)),
            pl.BlockSpec((None, 1, nblk), lambda l, j: (l, 0, j)),
        ],
        out_specs=pl.BlockSpec((None, N_MOD_PAD, nblk), lambda l, j: (l, 0, j)),
        compiler_params=_params(("parallel", "parallel")),
        name="modulation",
    )(cv, ada_w, ada_b.reshape(DEPTH, 1, 6 * D_MODEL))


def _inproj_kernel(idx_ref, x_ref, mod_ref, n1g_ref, win_ref, wba_ref, ones_ref, qg_ref, kg_ref, *rest):
    q_ref, k_ref, v_ref, zqkv_ref, zg_ref, zba_ref, kcache_ref, vcache_ref = rest[-8:]
    del idx_ref
    x = x_ref[...]
    y = x * lax.rsqrt(jnp.mean(x * x, axis=-1, keepdims=True) + EPS) * n1g_ref[...]
    m = mod_ref[...]
    h = (y * (1.0 + m[1:2, :]) + m[0:1, :]).astype(BF16)
    z = _dot(h, win_ref[...])
    zba_ref[...] = _dot(h, wba_ref[...])
    ones = ones_ref[...]

    def head_rms(zz, g):
        hi, lo = _split2(zz * zz)
        ms = (_dot(hi, ones) + _dot(lo, ones)) * (1.0 / NA_HD)
        return zz * lax.rsqrt(ms + EPS) * g

    q_ref[...] = head_rms(z[:, 0:NA_W], qg_ref[...]) * (NA_HD ** -0.5)
    k = head_rms(z[:, NA_W:2 * NA_W], kg_ref[...])
    v = z[:, 2 * NA_W:3 * NA_W]
    k_ref[...] = k
    v_ref[...] = v
    zqkv_ref[...] = z[:, 3 * NA_W:3 * NA_W + 3 * GDN_W]
    zg_ref[...] = z[:, 3 * NA_W + 3 * GDN_W:N_MAIN]

    @pl.when(pl.program_id(0) < BATCH)
    def _():
        kcache_ref[...] = k
        vcache_ref[...] = v


def _inproj(layer, tile_mod, x, mod, n1g, w_main, w_ba, ones_bd, qg, kg, caches):
    assert TM == SEQ
    tok = lambda w: pl.BlockSpec((TM, w), lambda t, idx: (t, 0))
    full = lambda a: pl.BlockSpec(a.shape, lambda t, idx: (0,) * a.ndim)
    out_w = (NA_W, NA_W, NA_W, 3 * GDN_W, GDN_W, LANE)
    cache_blk = pl.BlockSpec((None, None, SEQ, NA_W), lambda t, idx: (jnp.minimum(t, BATCH - 1), layer, 0, 0))
    cache_shape = jax.ShapeDtypeStruct((BATCH, DEPTH, SEQ, NA_W), F32)
    ins = [tile_mod, x, mod, n1g, w_main, w_ba, ones_bd, qg, kg]
    specs = [tok(D_MODEL), pl.BlockSpec((None, 6, D_MODEL), lambda t, idx: (idx[t], 0, 0)),
             full(n1g), full(w_main), full(w_ba), full(ones_bd), full(qg), full(kg)]
    aliases = {}
    if caches is not None:
        aliases = {len(ins): len(out_w), len(ins) + 1: len(out_w) + 1}
        ins += list(caches)
        specs += [pl.BlockSpec(memory_space=pl.ANY)] * 2
    return pl.pallas_call(
        _inproj_kernel,
        out_shape=[jax.ShapeDtypeStruct((N_TOK, w), F32) for w in out_w] + [cache_shape, cache_shape],
        grid_spec=pltpu.PrefetchScalarGridSpec(
            num_scalar_prefetch=1,
            grid=(N_TOK // TM,),
            in_specs=specs,
            out_specs=[tok(w) for w in out_w] + [cache_blk, cache_blk],
        ),
        input_output_aliases=aliases,
        compiler_params=_params(("arbitrary",)),
        name="inproj",
    )(*ins)


def _pair_masks(shape):
    lane = lax.broadcasted_iota(jnp.int32, shape, 1)
    return lane < NA_HD


def _ctx_attn_kernel(q_ref, k_ref, v_ref, o_ref):
    for p in range(NA_HEADS // 2):
        cols = slice(p * LANE, (p + 1) * LANE)
        q = q_ref[:, cols]
        kb = k_ref[:, cols].astype(BF16)
        vb = v_ref[:, cols].astype(BF16)
        first = _pair_masks(q.shape)
        outs = []
        for hm in (first, jnp.logical_not(first)):
            s = _dot_nt(jnp.where(hm, q, 0.0).astype(BF16), kb)
            e = jnp.exp(s - jnp.max(s, axis=-1, keepdims=True))
            o = _dot(e.astype(BF16), vb)
            outs.append(o / jnp.sum(e, axis=-1, keepdims=True))
        o_ref[:, cols] = jnp.where(first, outs[0], outs[1])


def _ctx_attention(q, k, v):
    blk = pl.BlockSpec((SEQ, NA_W), lambda b: (b, 0))
    return pl.pallas_call(
        _ctx_attn_kernel,
        out_shape=jax.ShapeDtypeStruct((N_TOK, NA_W), F32),
        grid=(BATCH,),
        in_specs=[blk, blk, blk],
        out_specs=blk,
        compiler_params=_params(("parallel",)),
        name="ctx_attention",
    )(q, k, v)


def _na_kernel(q_ref, k_ref, v_ref, kc_ref, vc_ref, bias_ref, o_all_ref, o_ref):
    del o_all_ref
    r = pl.program_id(1)
    base = jnp.clip(r - NA_KH // 2, 0, GRID_ROWS - NA_KH)
    dr0 = base - r + (NA_KH - 1)
    row0 = pl.multiple_of(base * GRID_W, GRID_W)
    nwin = NA_KH * GRID_W
    first = _pair_masks((GRID_W, LANE))
    heads = range(NA_HEADS)
    col = lambda h: slice((h // 2) * LANE, (h // 2 + 1) * LANE)
    qh = [jnp.where(first if h % 2 == 0 else jnp.logical_not(first), q_ref[:, col(h)], 0.0).astype(BF16)
          for h in heads]
    pairs = range(NA_HEADS // 2)
    kw = [k_ref[pl.ds(row0, nwin), col(2 * p)].astype(BF16) for p in pairs]
    kc = [kc_ref[:, col(2 * p)].astype(BF16) for p in pairs]
    vw = [v_ref[pl.ds(row0, nwin), col(2 * p)].astype(BF16) for p in pairs]
    vc = [vc_ref[:, col(2 * p)].astype(BF16) for p in pairs]
    s_loc = [_dot_nt(qh[h], kw[h // 2]) + bias_ref[h, dr0] for h in heads]
    s_ctx = [_dot_nt(qh[h], kc[h // 2]) for h in heads]
    mx = [jnp.maximum(jnp.max(s_loc[h], axis=-1, keepdims=True), jnp.max(s_ctx[h], axis=-1, keepdims=True))
          for h in heads]
    e_loc = [jnp.exp(s_loc[h] - mx[h]) for h in heads]
    e_ctx = [jnp.exp(s_ctx[h] - mx[h]) for h in heads]
    den = [jnp.sum(e_loc[h], axis=-1, keepdims=True) + jnp.sum(e_ctx[h], axis=-1, keepdims=True) for h in heads]
    o = [_dot(e_loc[h].astype(BF16), vw[h // 2]) + _dot(e_ctx[h].astype(BF16), vc[h // 2]) for h in heads]
    for p in pairs:
        o_ref[:, col(2 * p)] = jnp.where(first, o[2 * p] / den[2 * p], o[2 * p + 1] / den[2 * p + 1])


def _na_attention(layer, q, k, v, cache_k, cache_v, bias, o_all):
    lat0 = N_CTX // DEC_SEQ
    qblk = pl.BlockSpec((GRID_W, NA_W), lambda b, r: (N_CTX // GRID_W + b * GRID_ROWS + r, 0))
    kvblk = pl.BlockSpec((DEC_SEQ, NA_W), lambda b, r: (lat0 + b, 0))
    cblk = pl.BlockSpec((None, None, PAST_LEN, NA_W), lambda b, r: (b, layer, 0, 0))
    return pl.pallas_call(
        _na_kernel,
        out_shape=jax.ShapeDtypeStruct((N_TOK, NA_W), F32),
        grid=(DEC_BATCH, GRID_ROWS),
        in_specs=[qblk, kvblk, kvblk, cblk, cblk,
                  pl.BlockSpec(bias.shape, lambda b, r: (0, 0, 0, 0)),
                  pl.BlockSpec(memory_space=pl.ANY)],
        out_specs=qblk,
        input_output_aliases={6: 0},
        compiler_params=_params(("parallel", "arbitrary")),
        name="na_attention",
    )(q, k, v, cache_k, cache_v, bias, o_all)


N_DC = 2 * NA_KW - 1


def _na_bias_kernel(r_ref, o_ref):
    shape = (GRID_W, NA_KH * GRID_W)
    qc = lax.broadcasted_iota(jnp.int32, shape, 0)
    kc = lax.broadcasted_iota(jnp.int32, shape, 1) & (GRID_W - 1)
    ws = jnp.clip(qc - NA_KW // 2, 0, GRID_W - NA_KW)
    valid = (kc >= ws) & (kc < ws + NA_KW)
    dc = jnp.where(valid, kc - qc + (NA_KW - 1), -1)
    acc = jnp.full(shape, -jnp.inf, F32)
    for d in range(N_DC):
        acc = jnp.where(dc == d, r_ref[d:d + 1, :], acc)
    o_ref[...] = acc


def _na_bias_table(rpb):
    win = jnp.stack([rpb[:, d0:d0 + NA_KH, :] for d0 in range(NA_KH)], axis=1)
    rexp = jnp.repeat(win.transpose(0, 1, 3, 2), GRID_W, axis=-1)
    rexp = jnp.pad(rexp, ((0, 0), (0, 0), (0, 32 - N_DC), (0, 0)))
    blk = lambda rows: pl.BlockSpec((None, None, rows, NA_KH * GRID_W), lambda h, d: (h, d, 0, 0))
    return pl.pallas_call(
        _na_bias_kernel,
        out_shape=jax.ShapeDtypeStruct((NA_HEADS, NA_KH, GRID_W, NA_KH * GRID_W), F32),
        grid=(NA_HEADS, NA_KH),
        in_specs=[blk(32)],
        out_specs=blk(GRID_W),
        compiler_params=_params(("parallel", "parallel")),
        name="na_bias",
    )(rexp)


def _rope_tables():
    t = np.arange(DEC_SEQ)
    half = GDN_DK // 2
    inv_freq = (np.float32(ROPE_BASE) ** (-np.arange(0, half, 2, dtype=np.float32) / np.float32(half)))
    ang_r = (t // GRID_W).astype(np.float32)[:, None] * inv_freq
    ang_c = (t % GRID_W).astype(np.float32)[:, None] * inv_freq
    cr, sr, cc, sc = np.cos(ang_r), np.sin(ang_r), np.cos(ang_c), np.sin(ang_c)
    z = np.zeros_like(sr)
    cos = np.concatenate([cr, cr, cc, cc], axis=1)
    s_up = np.concatenate([-sr, z, -sc, z], axis=1)
    s_dn = np.concatenate([z, sr, z, sc], axis=1)
    return (jnp.asarray(cos, F32), jnp.asarray(s_up, F32), jnp.asarray(s_dn, F32))


HALO = 8
PREP_CHUNKS = 4
N_STREAM = 2 * GDN_HEADS
LOG_COL = N_STREAM


def _gdn_prep_kernel(axial, seq, *refs):
    if axial:
        (zqkv_ref, zba_ref, cw_ref, alog_ref, dtb_ref, cos_ref, sup_ref, sdn_ref,
         a_ref, qk_ref, r_ref, qd_ref, kd_ref, gl_ref, xs_ref) = refs
    else:
        (zqkv_ref, zba_ref, cw_ref, alog_ref, dtb_ref,
         a_ref, qk_ref, r_ref, qd_ref, kd_ref, gl_ref, xs_ref) = refs
    width = 3 * GDN_W
    part = pl.program_id(1)

    @pl.when(part == 0)
    def _():
        xs_ref[0:HALO, :] = jnp.zeros((HALO, width), F32)
        xs_ref[HALO + seq:HALO + seq + HALO, :] = jnp.zeros((HALO, width), F32)
        xs_ref[HALO:HALO + seq, :] = zqkv_ref[...]

    ri = lax.broadcasted_iota(jnp.int32, (CHUNK, CHUNK), 0)
    ci = lax.broadcasted_iota(jnp.int32, (CHUNK, CHUNK), 1)
    incl = (ri >= ci, ri <= ci)
    strict = (ri > ci, ri < ci)
    lane = lax.broadcasted_iota(jnp.int32, (CHUNK, LANE), 1)

    def body(i, carry):
        r0 = pl.multiple_of((part * PREP_CHUNKS + i) * CHUNK, CHUNK)
        rows = pl.ds(r0, CHUNK)
        win = xs_ref[pl.ds(r0, CHUNK + 2 * HALO), :]
        acc = None
        for j in range(CONV_K):
            lo = HALO + j - CONV_K // 2
            tap = win[lo:lo + CHUNK, :] * cw_ref[j:j + 1, :]
            acc = tap if acc is None else acc + tap
        act = _silu(acc)

        zba = zba_ref[rows, :]
        beta_all = 1.0 / (1.0 + jnp.exp(-zba))
        xa = zba + dtb_ref[...]
        softplus = jnp.maximum(xa, 0.0) + jnp.log(1.0 + jnp.exp(-jnp.abs(xa)))
        bg = jnp.where(lane < LOG_COL, beta_all, -jnp.exp(alog_ref[...]) * softplus)

        heads = []
        for hd in range(GDN_HEADS):
            qk_pair = []
            for which in range(2):
                c0 = which * GDN_W + hd * GDN_DK
                xh = act[:, c0:c0 + GDN_DK]
                xh = xh * lax.rsqrt(jnp.sum(xh * xh, axis=-1, keepdims=True) + EPS)
                if axial:
                    xh = (xh * cos_ref[rows, :]
                          + pltpu.roll(xh, GDN_DK - GDN_DK // 4, 1) * sup_ref[rows, :]
                          + pltpu.roll(xh, GDN_DK // 4, 1) * sdn_ref[rows, :])
                qk_pair.append(xh)
            q = qk_pair[0] * (GDN_DK ** -0.5)
            k = qk_pair[1]
            v = act[:, 2 * GDN_W + hd * GDN_DK:2 * GDN_W + (hd + 1) * GDN_DK]
            kbf = k.astype(BF16)
            heads.append((q, k, v, _dot_nt(kbf, kbf), _dot_nt(q.astype(BF16), kbf)))

        gcums = [_dot_exact_lhs(incl[d].astype(BF16), bg) for d in range(2)]
        gcum_ts = [g.T for g in gcums]
        for d in range(2):
            gcum, gcum_t = gcums[d], gcum_ts[d]
            last = gcum[0:1, :] if d else gcum[CHUNK - 1:CHUNK, :]
            e_in = jnp.exp(gcum)
            e_out = jnp.exp(last - gcum)
            gl_ref[i, d:d + 1, :] = jnp.exp(last)
            for hd in range(GDN_HEADS):
                q, k, v, kk, qk = heads[hd]
                cb = d * GDN_HEADS + hd
                cg = LOG_COL + cb
                beta = bg[:, cb:cb + 1]
                eg = e_in[:, cg:cg + 1]
                el = e_out[:, cg:cg + 1]
                diff = gcum[:, cg:cg + 1] - gcum_t[cg:cg + 1, :]
                decay = jnp.where(incl[d], jnp.exp(jnp.where(incl[d], diff, 0.0)), 0.0)
                a_ref[d, i, hd] = jnp.where(strict[d], beta * kk * decay, 0.0)
                qk_ref[d, i, hd] = (qk * decay).astype(BF16)
                kb = k * beta
                r_ref[d, i, hd, :, 0:GDN_DK] = (v * beta).astype(BF16)
                r_ref[d, i, hd, :, GDN_DK:2 * GDN_DK] = (kb * eg).astype(BF16)
                qd_ref[d, i, hd] = (q * eg).astype(BF16)
                kd_ref[d, i, hd] = (k * el).astype(BF16)
        return carry

    lax.fori_loop(0, PREP_CHUNKS, body, 0)


def _gdn_prep(axial, zqkv, zba, conv_w, a_log, dt_bias, rope):
    seq = DEC_SEQ if axial else SEQ
    nseq = DEC_BATCH if axial else BATCH
    nparts = seq // (PREP_CHUNKS * CHUNK)
    nchunk = nseq * seq // CHUNK
    blk0 = N_CTX // DEC_SEQ if axial else 0
    tok = lambda w: pl.BlockSpec((seq, w), lambda b, p: (blk0 + b, 0))
    full = lambda a: pl.BlockSpec(a.shape, lambda b, p: (0,) * a.ndim)
    ins = [zqkv, zba, conv_w, a_log, dt_bias]
    specs = [tok(3 * GDN_W), tok(LANE), full(conv_w), full(a_log), full(dt_bias)]
    if axial:
        ins += list(rope)
        specs += [full(t) for t in rope]
    tile = lambda w: pl.BlockSpec((2, PREP_CHUNKS, GDN_HEADS, CHUNK, w), lambda b, p: (0, b * nparts + p, 0, 0, 0))
    shape = lambda w, dt: jax.ShapeDtypeStruct((2, nchunk, GDN_HEADS, CHUNK, w), dt)
    return pl.pallas_call(
        functools.partial(_gdn_prep_kernel, axial, seq),
        out_shape=[shape(CHUNK, F32), shape(CHUNK, BF16), shape(2 * GDN_DK, BF16),
                   shape(GDN_DK, BF16), shape(GDN_DK, BF16),
                   jax.ShapeDtypeStruct((nchunk, 2, LANE), F32)],
        grid=(nseq, nparts),
        in_specs=specs,
        out_specs=[tile(CHUNK), tile(CHUNK), tile(2 * GDN_DK), tile(GDN_DK), tile(GDN_DK),
                   pl.BlockSpec((PREP_CHUNKS, 2, LANE), lambda b, p: (b * nparts + p, 0, 0))],
        scratch_shapes=[pltpu.VMEM((seq + 2 * HALO, 3 * GDN_W), F32)],
        compiler_params=_params(("parallel", "arbitrary")),
        name="gdn_prep_lat" if axial else "gdn_prep_ctx",
    )(*ins)


TRI_BLK = 8
TRI_ELEMS = CHUNK * CHUNK


def _tri_inverse_kernel(groups_per_dir, a_ref, o_ref, at_ref, tt_ref):
    backward = pl.program_id(0) >= groups_per_dir
    for blk in range(TRI_ELEMS // LANE):
        cols = slice(blk * LANE, (blk + 1) * LANE)
        at_ref[cols, :] = a_ref[:, cols].T
    tt_ref[...] = jnp.zeros(tt_ref.shape, F32)
    cidx = lax.broadcasted_iota(jnp.int32, (CHUNK, LANE), 0)

    def phys(idx):
        return jnp.where(backward, CHUNK - 1 - idx, idx)

    for ib in range(CHUNK // TRI_BLK):
        def row_body(ii, carry, ib=ib):
            i = phys(ib * TRI_BLK + ii)
            acc = (cidx == i).astype(F32)
            for jl in range((ib + 1) * TRI_BLK):
                j = phys(jl)
                arow = at_ref[pl.ds(i * CHUNK + j, 1), :]
                acc = acc - arow * tt_ref[pl.ds(pl.multiple_of(j * CHUNK, CHUNK), CHUNK), :]
            tt_ref[pl.ds(pl.multiple_of(i * CHUNK, CHUNK), CHUNK), :] = acc
            return carry

        lax.fori_loop(0, TRI_BLK, row_body, 0)

    for blk in range(TRI_ELEMS // LANE):
        cols = slice(blk * LANE, (blk + 1) * LANE)
        o_ref[:, cols] = tt_ref[cols, :].T.astype(BF16)


def _tri_inverse(a):
    nprob = a.shape[1] * a.shape[2]
    groups_per_dir = nprob // LANE
    blk = pl.BlockSpec((LANE, TRI_ELEMS), lambda g: (g, 0))
    out = pl.pallas_call(
        functools.partial(_tri_inverse_kernel, groups_per_dir),
        out_shape=jax.ShapeDtypeStruct((2 * nprob, TRI_ELEMS), BF16),
        grid=(2 * groups_per_dir,),
        in_specs=[blk],
        out_specs=blk,
        scratch_shapes=[pltpu.VMEM((TRI_ELEMS, LANE), F32), pltpu.VMEM((TRI_ELEMS, LANE), F32)],
        compiler_params=_params(("parallel",)),
        name="tri_inverse",
    )(a.reshape(2 * nprob, TRI_ELEMS))
    return out.reshape(a.shape)


def _gdn_scan_kernel(nchunk, t_ref, qk_ref, r_ref, qd_ref, kd_ref, gl_ref, s0_ref, *rest):
    o_ref, sfin_ref, state_ref = rest[-3:]
    state_ref[...] = s0_ref[...]
    streams = [(d, hd) for d in range(2) for hd in range(GDN_HEADS)]

    def body(c, carry):
        cc = (c, nchunk - 1 - c)
        uws = [_dot(t_ref[d, cc[d], hd], r_ref[d, cc[d], hd]).astype(BF16) for d, hd in streams]
        mbs = [_dot_tn(kd_ref[d, cc[d], hd], uw) for (d, hd), uw in zip(streams, uws)]
        qos = [_dot(qk_ref[d, cc[d], hd], uw) for (d, hd), uw in zip(streams, uws)]
        sts = [state_ref[d * GDN_HEADS + hd] for d, hd in streams]
        sbs = [s.astype(BF16) for s in sts]
        for (d, hd), qo, sb in zip(streams, qos, sbs):
            rows = pl.ds(pl.multiple_of(cc[d] * CHUNK, CHUNK), CHUNK)
            qp = (qd_ref[d, cc[d], hd].astype(F32) - qo[:, GDN_DK:]).astype(BF16)
            o_ref[d, rows, hd * GDN_DK:(hd + 1) * GDN_DK] = _dot(qp, sb) + qo[:, :GDN_DK]
        for (d, hd), mb, s, sb in zip(streams, mbs, sts, sbs):
            sidx = d * GDN_HEADS + hd
            gl = gl_ref[cc[d]][d:d + 1, LOG_COL + sidx:LOG_COL + sidx + 1]
            state_ref[sidx] = s * gl - _dot(mb[:, GDN_DK:].astype(BF16), sb) + mb[:, :GDN_DK]
        return carry

    lax.fori_loop(0, nchunk, body, 0)
    sfin_ref[...] = state_ref[...]


def _gdn_scan(seq, nseq, tinv, qk, r, qd, kd, gl, s0, o_all=None):
    nc = seq // CHUNK
    blk0 = 0 if o_all is None else N_CTX // seq
    tile = lambda w: pl.BlockSpec((2, nc, GDN_HEADS, CHUNK, w), lambda b: (0, b, 0, 0, 0))
    sblk = pl.BlockSpec((None, N_STREAM, GDN_DK, GDN_DK), lambda b: (b, 0, 0, 0))
    oblk = pl.BlockSpec((2, seq, GDN_W), lambda b: (0, blk0 + b, 0))
    ins = [tinv, qk, r, qd, kd, gl, s0]
    specs = [tile(CHUNK), tile(CHUNK), tile(2 * GDN_DK), tile(GDN_DK), tile(GDN_DK),
             pl.BlockSpec((nc, 2, LANE), lambda b: (b, 0, 0)), sblk]
    aliases = {}
    if o_all is not None:
        ins.append(o_all)
        specs.append(pl.BlockSpec(memory_space=pl.ANY))
        aliases = {len(ins) - 1: 0}
    return pl.pallas_call(
        functools.partial(_gdn_scan_kernel, nc),
        out_shape=[jax.ShapeDtypeStruct((2, N_TOK, GDN_W), F32),
                   jax.ShapeDtypeStruct((nseq, N_STREAM, GDN_DK, GDN_DK), F32)],
        grid=(nseq,),
        in_specs=specs,
        out_specs=[oblk, sblk],
        scratch_shapes=[pltpu.VMEM((N_STREAM, GDN_DK, GDN_DK), F32)],
        input_output_aliases=aliases,
        compiler_params=_params(("parallel",)),
        name="gdn_scan_lat" if seq == DEC_SEQ else "gdn_scan_ctx",
    )(*ins)


def _first_max(vals):
    sel = []
    taken = None
    for a, va in enumerate(vals):
        is_max = None
        for b, vb in enumerate(vals):
            if a == b:
                continue
            c = va >= vb
            is_max = c if is_max is None else (is_max & c)
        if taken is not None:
            is_max = is_max & jnp.logical_not(taken)
        sel.append(is_max)
        taken = is_max if taken is None else (taken | is_max)
    return sel


def _gate_rows(lt):
    gl = [lt[a:a + 1, :] for a in range(N_GROUPS)]
    gsel = _first_max(gl)
    gmax = functools.reduce(jnp.maximum, gl)
    gden = functools.reduce(lambda x, y: x + y, [jnp.exp(x - gmax) for x in gl])
    g_w = 1.0 / gden
    el = []
    for kx in range(EXP_PER_GROUP):
        acc = None
        for a in range(N_GROUPS):
            row = N_GROUPS + a * EXP_PER_GROUP + kx
            term = jnp.where(gsel[a], lt[row:row + 1, :], 0.0)
            acc = term if acc is None else acc + term
        el.append(acc)
    emax = functools.reduce(jnp.maximum, el)
    ee = [jnp.exp(x - emax) for x in el]
    eden = functools.reduce(lambda x, y: x + y, ee)
    ep = [x / eden for x in ee]
    top1 = _first_max(ep)
    ep2 = [jnp.where(top1[kx], -1.0, ep[kx]) for kx in range(EXP_PER_GROUP)]
    top2 = _first_max(ep2)
    chosen = [top1[kx] | top2[kx] for kx in range(EXP_PER_GROUP)]
    wsum = functools.reduce(lambda x, y: x + y,
                            [jnp.where(chosen[kx], ep[kx], 0.0) for kx in range(EXP_PER_GROUP)])
    within = [jnp.where(chosen[kx], ep[kx] / wsum, 0.0) for kx in range(EXP_PER_GROUP)]
    rows = []
    for a in range(N_GROUPS):
        for kx in range(EXP_PER_GROUP):
            rows.append(jnp.where(gsel[a], g_w * within[kx], 0.0))
    return rows


def _outproj_kernel(idx_ref, x_ref, ona_ref, og_ref, zg_ref, mod_ref, wo_ref, gng_ref, n2g_ref,
                    wr_ref, br_ref, x1_ref, h2_ref, gate_ref):
    del idx_ref
    m = mod_ref[...]
    og = og_ref[0] + og_ref[1]
    zg = zg_ref[...]
    parts = []
    for hd in range(GDN_HEADS):
        cols = slice(hd * GDN_DK, (hd + 1) * GDN_DK)
        oh = og[:, cols]
        oh = oh * lax.rsqrt(jnp.mean(oh * oh, axis=-1, keepdims=True) + EPS) * gng_ref[...]
        parts.append((oh * _silu(zg[:, cols])).astype(BF16))
    mix = _dot(ona_ref[...].astype(BF16), wo_ref[0:NA_W, :])
    for hd in range(GDN_HEADS):
        r0 = NA_W + hd * GDN_DK
        mix = mix + _dot(parts[hd], wo_ref[r0:r0 + GDN_DK, :])
    x1 = x_ref[...] + m[2:3, :] * mix
    x1_ref[...] = x1
    y = x1 * lax.rsqrt(jnp.mean(x1 * x1, axis=-1, keepdims=True) + EPS) * n2g_ref[...]
    h2 = y * (1.0 + m[4:5, :]) + m[3:4, :]
    h2_ref[...] = h2.astype(BF16)
    logits = _dot3(h2, wr_ref[...]) + br_ref[...]
    rows = _gate_rows(logits.T)
    rid = lax.broadcasted_iota(jnp.int32, (N_EXPERTS, TM), 0)
    gate_t = jnp.zeros((N_EXPERTS, TM), F32)
    for ex in range(N_EXPERTS):
        gate_t = jnp.where(rid == ex, rows[ex], gate_t)
    gate_t = jnp.concatenate([gate_t, jnp.zeros((LANE - N_EXPERTS, TM), F32)], axis=0)
    gate_ref[...] = gate_t.T


def _outproj(tile_mod, x, o_na, o_gdn, zg, mod, w_out, gng, n2g, w_r, b_r):
    tok = lambda w: pl.BlockSpec((TM, w), lambda t, idx: (t, 0))
    full = lambda a: pl.BlockSpec(a.shape, lambda t, idx: (0,) * a.ndim)
    return pl.pallas_call(
        _outproj_kernel,
        out_shape=[jax.ShapeDtypeStruct((N_TOK, D_MODEL), F32),
                   jax.ShapeDtypeStruct((N_TOK, D_MODEL), BF16),
                   jax.ShapeDtypeStruct((N_TOK, LANE), F32)],
        grid_spec=pltpu.PrefetchScalarGridSpec(
            num_scalar_prefetch=1,
            grid=(N_TOK // TM,),
            in_specs=[
                tok(D_MODEL), tok(NA_W),
                pl.BlockSpec((2, TM, GDN_W), lambda t, idx: (0, t, 0)), tok(GDN_W),
                pl.BlockSpec((None, 6, D_MODEL), lambda t, idx: (idx[t], 0, 0)),
                full(w_out), full(gng), full(n2g), full(w_r), full(b_r),
            ],
            out_specs=[tok(D_MODEL), tok(D_MODEL), tok(LANE)],
        ),
        compiler_params=_params(("parallel",)),
        name="outproj",
    )(tile_mod, x, o_na, o_gdn, zg, mod, w_out, gng, n2g, w_r, b_r)


def _moe_kernel(idx_ref, x1_ref, h2_ref, gate_ref, mod_ref, w1_ref, w3_ref, w2_ref, o_ref, acc_ref):
    del idx_ref
    e = pl.program_id(1)

    @pl.when(e == 0)
    def _():
        acc_ref[...] = jnp.zeros(acc_ref.shape, F32)

    h = h2_ref[...]
    a = _dot(h, w1_ref[...].astype(BF16))
    b = _dot(h, w3_ref[...].astype(BF16))
    gate = gate_ref[...]
    lane = lax.broadcasted_iota(jnp.int32, gate.shape, 1)
    gcol = jnp.sum(jnp.where(lane == e, gate, 0.0), axis=-1, keepdims=True)
    s = (_silu(a) * b * gcol).astype(BF16)
    acc_ref[...] += _dot(s, w2_ref[...].astype(BF16))

    @pl.when(e == N_EXPERTS - 1)
    def _():
        o_ref[...] = x1_ref[...] + mod_ref[5:6, :] * acc_ref[...]


def _moe(layer, tile_mod, x1, h2, gate, mod, w1, w3, w2):
    tok = lambda w: pl.BlockSpec((TM_MOE, w), lambda t, e, idx: (t, 0))
    return pl.pallas_call(
        _moe_kernel,
        out_shape=jax.ShapeDtypeStruct((N_TOK, D_MODEL), F32),
        grid_spec=pltpu.PrefetchScalarGridSpec(
            num_scalar_prefetch=1,
            grid=(N_TOK // TM_MOE, N_EXPERTS),
            in_specs=[
                tok(D_MODEL), tok(D_MODEL), tok(LANE),
                pl.BlockSpec((None, 6, D_MODEL), lambda t, e, idx: (idx[t], 0, 0)),
                pl.BlockSpec((None, None, D_MODEL, EXPERT_FF), lambda t, e, idx: (layer, e, 0, 0)),
                pl.BlockSpec((None, None, D_MODEL, EXPERT_FF), lambda t, e, idx: (layer, e, 0, 0)),
                pl.BlockSpec((None, None, EXPERT_FF, D_MODEL), lambda t, e, idx: (layer, e, 0, 0)),
            ],
            out_specs=tok(D_MODEL),
            scratch_shapes=[pltpu.VMEM((TM_MOE, D_MODEL), F32)],
        ),
        compiler_params=_params(("parallel", "arbitrary")),
        name="moe",
    )(tile_mod, x1, h2, gate, mod, w1, w3, w2)


def _tile_mod_index(tile):
    t = np.arange(N_TOK // tile) * tile
    return jnp.asarray(np.where(t < N_CTX, 0, 1 + (t - N_CTX) // DEC_SEQ), jnp.int32)


def kernel(x_prompt, x_sample, c, cache_k, cache_v, state_ssm, c_ctx, ada_w, ada_b, norm1_g, norm2_g, w_in, w_out, na_qn_g, na_kn_g, na_rpb, gdn_conv_w, gdn_a_log, gdn_dt_bias, gdn_norm_g, moe_w_rg, moe_b_rg, moe_w_re, moe_b_re, moe_w1, moe_w3, moe_w2):
    x = jnp.concatenate([x_prompt.reshape(N_CTX, D_MODEL), x_sample.reshape(N_LAT, D_MODEL)], axis=0)
    cv = jnp.concatenate([c_ctx[None, :], c, jnp.zeros((N_MOD_PAD - N_MOD, D_MODEL), F32)], axis=0)
    mod_all = _modulation(cv, ada_w, ada_b).reshape(DEPTH, N_MOD_PAD, 6, D_MODEL)

    idx_tm = _tile_mod_index(TM)
    idx_moe = _tile_mod_index(TM_MOE)
    hh = np.arange(NA_W) // NA_HD
    ones_bd = jnp.asarray(hh[:, None] == hh[None, :], BF16)
    rope = _rope_tables()
    cache_k4 = cache_k.reshape(DEC_BATCH, DEPTH, PAST_LEN, NA_W)
    cache_v4 = cache_v.reshape(DEC_BATCH, DEPTH, PAST_LEN, NA_W)
    lane_pad = lambda a, at: jnp.zeros((1, LANE), F32).at[0, at:at + a.size].set(a.reshape(-1))
    zeros_state = jnp.zeros((BATCH, 2 * GDN_HEADS, GDN_DK, GDN_DK), F32)

    caches, ss = None, []
    for l in range(DEPTH):
        mod = mod_all[l]
        w_main = w_in[l, :, :N_MAIN].astype(BF16)
        w_ba = jnp.zeros((D_MODEL, LANE), BF16).at[:, :N_IN - N_MAIN].set(w_in[l, :, N_MAIN:].astype(BF16))
        qg = jnp.tile(na_qn_g[l], NA_HEADS)[None, :]
        kg = jnp.tile(na_kn_g[l], NA_HEADS)[None, :]
        q, k, v, zqkv, zg, zba, *caches = _inproj(l, idx_tm, x, mod, norm1_g[l][None, :], w_main, w_ba,
                                                  ones_bd, qg, kg, caches)

        bias = _na_bias_table(na_rpb[l])
        o_na = _na_attention(l, q, k, v, cache_k4, cache_v4, bias, _ctx_attention(q, k, v))

        conv_w = jnp.zeros((8, 3 * GDN_W), F32).at[:CONV_K].set(gdn_conv_w[l])
        a_row = lane_pad(gdn_a_log[l], 2 * GDN_HEADS)
        dt_row = lane_pad(gdn_dt_bias[l], 2 * GDN_HEADS)
        a_c, *ops_c = _gdn_prep(False, zqkv, zba, conv_w, a_row, dt_row, None)
        a_l, *ops_l = _gdn_prep(True, zqkv, zba, conv_w, a_row, dt_row, rope)
        og_c, s_ctx = _gdn_scan(SEQ, BATCH, _tri_inverse(a_c), *ops_c, zeros_state)
        s0_lat = state_ssm[:, l].reshape(DEC_BATCH, N_STREAM, GDN_DK, GDN_DK)
        o_gdn, _ = _gdn_scan(DEC_SEQ, DEC_BATCH, _tri_inverse(a_l), *ops_l, s0_lat, og_c)

        w_r = jnp.zeros((D_MODEL, LANE), F32).at[:, :N_GROUPS].set(moe_w_rg[l])
        w_r = w_r.at[:, N_GROUPS:N_GROUPS + N_EXPERTS].set(moe_w_re[l])
        b_r = lane_pad(jnp.concatenate([moe_b_rg[l], moe_b_re[l]]), 0)
        x1, h2, gate = _outproj(idx_tm, x, o_na, o_gdn, zg, mod, w_out[l].astype(BF16),
                                gdn_norm_g[l][None, :], norm2_g[l][None, :], w_r, b_r)
        x = _moe(l, idx_moe, x1, h2, gate, mod, moe_w1, moe_w3, moe_w2)

        ss.append(s_ctx.reshape(BATCH, 2, GDN_HEADS, GDN_DK, GDN_DK))

    y_prompt = x[:N_CTX].reshape(BATCH, SEQ, D_MODEL)
    y_sample = x[N_CTX:].reshape(DEC_BATCH, DEC_SEQ, D_MODEL)
    new_k, new_v = (a.reshape(BATCH, DEPTH, SEQ, NA_HEADS, NA_HD) for a in caches)
    return (y_prompt, y_sample, new_k, new_v, jnp.stack(ss, axis=1))
```

```python
import functools

import jax
import jax.numpy as jnp
import numpy as np
from jax import lax
from jax.experimental import pallas as pl
from jax.experimental.pallas import tpu as pltpu

F32 = jnp.float32
BF16 = jnp.bfloat16

D_MODEL = 1024
BATCH = 16
SEQ = 256
DEPTH = 2
DEC_BATCH = 8
DEC_SEQ = 1024
PAST_LEN = 512
GRID_W = 64
GRID_ROWS = DEC_SEQ // GRID_W
NA_HEADS = 8
NA_HD = 64
NA_W = NA_HEADS * NA_HD
NA_KH = 8
NA_KW = 16
GDN_HEADS = 4
GDN_DK = 128
GDN_W = GDN_HEADS * GDN_DK
CONV_K = 5
CHUNK = 64
ROPE_BASE = 10000.0
N_GROUPS = 4
EXP_PER_GROUP = 4
N_EXPERTS = 16
EXPERT_FF = 512
EPS = 1e-6
N_IN = 3 * NA_W + 4 * GDN_W + 4 * GDN_HEADS

N_CTX = BATCH * SEQ
N_LAT = DEC_BATCH * DEC_SEQ
N_TOK = N_CTX + N_LAT
N_MOD = 1 + DEC_BATCH
N_MOD_PAD = 16
LANE = 128
N_MAIN = 3 * NA_W + 4 * GDN_W
TM = 256
VMEM_LIMIT = 56 * 1024 * 1024


def _dot(a, b):
    return jnp.dot(a, b, preferred_element_type=F32)


def _dot_nt(a, b):
    return lax.dot_general(a, b, (((1,), (1,)), ((), ())), preferred_element_type=F32)


def _dot_tn(a, b):
    return lax.dot_general(a, b, (((0,), (0,)), ((), ())), preferred_element_type=F32)


def _split2(x):
    hi = x.astype(BF16)
    lo = (x - hi.astype(F32)).astype(BF16)
    return hi, lo


def _split3(x):
    hi = x.astype(BF16)
    r = x - hi.astype(F32)
    mid = r.astype(BF16)
    lo = (r - mid.astype(F32)).astype(BF16)
    return hi, mid, lo


def _dot3(a, b):
    ah, al = _split2(a)
    bh, bl = _split2(b)
    return _dot(ah, bh) + (_dot(ah, bl) + _dot(al, bh))


def _dot_exact_lhs(mask_bf16, x):
    hi, mid, lo = _split3(x)
    return _dot(mask_bf16, hi) + (_dot(mask_bf16, mid) + _dot(mask_bf16, lo))


def _silu(x):
    return x * (1.0 / (1.0 + jnp.exp(-x)))


def _params(sem):
    return pltpu.CompilerParams(dimension_semantics=sem, vmem_limit_bytes=VMEM_LIMIT)


def _mod_kernel(cv_ref, w_ref, b_ref, o_ref):
    cv = cv_ref[...]
    s = _silu(cv)
    o_ref[...] = jnp.dot(s, w_ref[...], preferred_element_type=F32,
                         precision=lax.Precision.HIGHEST) + b_ref[...]


def _modulation(cv, ada_w, ada_b):
    nblk = 512
    return pl.pallas_call(
        _mod_kernel,
        out_shape=jax.ShapeDtypeStruct((DEPTH, N_MOD_PAD, 6 * D_MODEL), F32),
        grid=(DEPTH, 6 * D_MODEL // nblk),
        in_specs=[
            pl.BlockSpec((N_MOD_PAD, D_MODEL), lambda l, j: (0, 0)),
            pl.BlockSpec((None, D_MODEL, nblk), lambda l, j: (l, 0, j)),
            pl.BlockSpec((None, 1, nblk), lambda l, j: (l, 0, j)),
        ],
        out_specs=pl.BlockSpec((None, N_MOD_PAD, nblk), lambda l, j: (l, 0, j)),
        compiler_params=_params(("parallel", "parallel")),
        name="modulation",
    )(cv, ada_w, ada_b.reshape(DEPTH, 1, 6 * D_MODEL))


def _inproj_kernel(idx_ref, x_ref, mod_ref, n1g_ref, win_ref, wba_ref, ones_ref, qg_ref, kg_ref, *rest):
    q_ref, k_ref, v_ref, zqkv_ref, zg_ref, zba_ref, kcache_ref, vcache_ref = rest[-8:]
    del idx_ref
    x = x_ref[...]
    y = x * lax.rsqrt(jnp.mean(x * x, axis=-1, keepdims=True) + EPS) * n1g_ref[...]
    m = mod_ref[...]
    h = (y * (1.0 + m[1:2, :]) + m[0:1, :]).astype(BF16)
    z = _dot(h, win_ref[...])
    zba_ref[...] = _dot(h, wba_ref[...])
    ones = ones_ref[...]

    def head_rms(zz, g):
        hi, lo = _split2(zz * zz)
        ms = (_dot(hi, ones) + _dot(lo, ones)) * (1.0 / NA_HD)
        return zz * lax.rsqrt(ms + EPS) * g

    q_ref[...] = head_rms(z[:, 0:NA_W], qg_ref[...]) * (NA_HD ** -0.5)
    k = head_rms(z[:, NA_W:2 * NA_W], kg_ref[...])
    v = z[:, 2 * NA_W:3 * NA_W]
    k_ref[...] = k
    v_ref[...] = v
    zqkv_ref[...] = z[:, 3 * NA_W:3 * NA_W + 3 * GDN_W]
    zg_ref[...] = z[:, 3 * NA_W + 3 * GDN_W:N_MAIN]

    @pl.when(pl.program_id(0) < BATCH)
    def _():
        kcache_ref[...] = k
        vcache_ref[...] = v


def _inproj(layer, tile_mod, x, mod, n1g, w_main, w_ba, ones_bd, qg, kg, caches):
    assert TM == SEQ
    tok = lambda w: pl.BlockSpec((TM, w), lambda t, idx: (t, 0))
    full = lambda a: pl.BlockSpec(a.shape, lambda t, idx: (0,) * a.ndim)
    out_w = (NA_W, NA_W, NA_W, 3 * GDN_W, GDN_W, LANE)
    cache_blk = pl.BlockSpec((None, None, SEQ, NA_W), lambda t, idx: (jnp.minimum(t, BATCH - 1), layer, 0, 0))
    cache_shape = jax.ShapeDtypeStruct((BATCH, DEPTH, SEQ, NA_W), F32)
    ins = [tile_mod, x, mod, n1g, w_main, w_ba, ones_bd, qg, kg]
    specs = [tok(D_MODEL), pl.BlockSpec((None, 6, D_MODEL), lambda t, idx: (idx[t], 0, 0)),
             full(n1g), full(w_main), full(w_ba), full(ones_bd), full(qg), full(kg)]
    aliases = {}
    if caches is not None:
        aliases = {len(ins): len(out_w), len(ins) + 1: len(out_w) + 1}
        ins += list(caches)
        specs += [pl.BlockSpec(memory_space=pl.ANY)] * 2
    return pl.pallas_call(
        _inproj_kernel,
        out_shape=[jax.ShapeDtypeStruct((N_TOK, w), F32) for w in out_w] + [cache_shape, cache_shape],
        grid_spec=pltpu.PrefetchScalarGridSpec(
            num_scalar_prefetch=1,
            grid=(N_TOK // TM,),
            in_specs=specs,
            out_specs=[tok(w) for w in out_w] + [cache_blk, cache_blk],
        ),
        input_output_aliases=aliases,
        compiler_params=_params(("arbitrary",)),
        name="inproj",
    )(*ins)


def _pair_masks(shape):
    lane = lax.broadcasted_iota(jnp.int32, shape, 1)
    return lane < NA_HD


def _ctx_attn_kernel(q_ref, k_ref, v_ref, o_ref):
    for p in range(NA_HEADS // 2):
        cols = slice(p * LANE, (p + 1) * LANE)
        q = q_ref[:, cols]
        kb = k_ref[:, cols].astype(BF16)
        vb = v_ref[:, cols].astype(BF16)
        first = _pair_masks(q.shape)
        outs = []
        for hm in (first, jnp.logical_not(first)):
            s = _dot_nt(jnp.where(hm, q, 0.0).astype(BF16), kb)
            e = jnp.exp(s - jnp.max(s, axis=-1, keepdims=True))
            o = _dot(e.astype(BF16), vb)
            outs.append(o / jnp.sum(e, axis=-1, keepdims=True))
        o_ref[:, cols] = jnp.where(first, outs[0], outs[1])


def _ctx_attention(q, k, v):
    blk = pl.BlockSpec((SEQ, NA_W), lambda b: (b, 0))
    return pl.pallas_call(
        _ctx_attn_kernel,
        out_shape=jax.ShapeDtypeStruct((N_TOK, NA_W), F32),
        grid=(BATCH,),
        in_specs=[blk, blk, blk],
        out_specs=blk,
        compiler_params=_params(("parallel",)),
        name="ctx_attention",
    )(q, k, v)


def _na_kernel(q_ref, k_ref, v_ref, kc_ref, vc_ref, bias_ref, o_all_ref, o_ref):
    del o_all_ref
    r = pl.program_id(1)
    base = jnp.clip(r - NA_KH // 2, 0, GRID_ROWS - NA_KH)
    dr0 = base - r + (NA_KH - 1)
    row0 = pl.multiple_of(base * GRID_W, GRID_W)
    nwin = NA_KH * GRID_W
    first = _pair_masks((GRID_W, LANE))
    heads = range(NA_HEADS)
    col = lambda h: slice((h // 2) * LANE, (h // 2 + 1) * LANE)
    qh = [jnp.where(first if h % 2 == 0 else jnp.logical_not(first), q_ref[:, col(h)], 0.0).astype(BF16)
          for h in heads]
    pairs = range(NA_HEADS // 2)
    kw = [k_ref[pl.ds(row0, nwin), col(2 * p)].astype(BF16) for p in pairs]
    kc = [kc_ref[:, col(2 * p)].astype(BF16) for p in pairs]
    vw = [v_ref[pl.ds(row0, nwin), col(2 * p)].astype(BF16) for p in pairs]
    vc = [vc_ref[:, col(2 * p)].astype(BF16) for p in pairs]
    s_loc = [_dot_nt(qh[h], kw[h // 2]) + bias_ref[h, dr0] for h in heads]
    s_ctx = [_dot_nt(qh[h], kc[h // 2]) for h in heads]
    mx = [jnp.maximum(jnp.max(s_loc[h], axis=-1, keepdims=True), jnp.max(s_ctx[h], axis=-1, keepdims=True))
          for h in heads]
    e_loc = [jnp.exp(s_loc[h] - mx[h]) for h in heads]
    e_ctx = [jnp.exp(s_ctx[h] - mx[h]) for h in heads]
    den = [jnp.sum(e_loc[h], axis=-1, keepdims=True) + jnp.sum(e_ctx[h], axis=-1, keepdims=True) for h in heads]
    o = [_dot(e_loc[h].astype(BF16), vw[h // 2]) + _dot(e_ctx[h].astype(BF16), vc[h // 2]) for h in heads]
    for p in pairs:
        o_ref[:, col(2 * p)] = jnp.where(first, o[2 * p] / den[2 * p], o[2 * p + 1] / den[2 * p + 1])


def _na_attention(layer, q, k, v, cache_k, cache_v, bias, o_all):
    lat0 = N_CTX // DEC_SEQ
    qblk = pl.BlockSpec((GRID_W, NA_W), lambda b, r: (N_CTX // GRID_W + b * GRID_ROWS + r, 0))
    kvblk = pl.BlockSpec((DEC_SEQ, NA_W), lambda b, r: (lat0 + b, 0))
    cblk = pl.BlockSpec((None, None, PAST_LEN, NA_W), lambda b, r: (b, layer, 0, 0))
    return pl.pallas_call(
        _na_kernel,
        out_shape=jax.ShapeDtypeStruct((N_TOK, NA_W), F32),
        grid=(DEC_BATCH, GRID_ROWS),
        in_specs=[qblk, kvblk, kvblk, cblk, cblk,
                  pl.BlockSpec(bias.shape, lambda b, r: (0, 0, 0, 0)),
                  pl.BlockSpec(memory_space=pl.ANY)],
        out_specs=qblk,
        input_output_aliases={6: 0},
        compiler_params=_params(("parallel", "arbitrary")),
        name="na_attention",
    )(q, k, v, cache_k, cache_v, bias, o_all)


N_DC = 2 * NA_KW - 1


def _na_bias_kernel(r_ref, o_ref):
    shape = (GRID_W, NA_KH * GRID_W)
    qc = lax.broadcasted_iota(jnp.int32, shape, 0)
    kc = lax.broadcasted_iota(jnp.int32, shape, 1) & (GRID_W - 1)
    ws = jnp.clip(qc - NA_KW // 2, 0, GRID_W - NA_KW)
    valid = (kc >= ws) & (kc < ws + NA_KW)
    dc = jnp.where(valid, kc - qc + (NA_KW - 1), -1)
    acc = jnp.full(shape, -jnp.inf, F32)
    for d in range(N_DC):
        acc = jnp.where(dc == d, r_ref[d:d + 1, :], acc)
    o_ref[...] = acc


def _na_bias_table(rpb):
    win = jnp.stack([rpb[:, d0:d0 + NA_KH, :] for d0 in range(NA_KH)], axis=1)
    rexp = jnp.repeat(win.transpose(0, 1, 3, 2), GRID_W, axis=-1)
    rexp = jnp.pad(rexp, ((0, 0), (0, 0), (0, 32 - N_DC), (0, 0)))
    blk = lambda rows: pl.BlockSpec((None, None, rows, NA_KH * GRID_W), lambda h, d: (h, d, 0, 0))
    return pl.pallas_call(
        _na_bias_kernel,
        out_shape=jax.ShapeDtypeStruct((NA_HEADS, NA_KH, GRID_W, NA_KH * GRID_W), F32),
        grid=(NA_HEADS, NA_KH),
        in_specs=[blk(32)],
        out_specs=blk(GRID_W),
        compiler_params=_params(("parallel", "parallel")),
        name="na_bias",
    )(rexp)


def _rope_tables():
    t = np.arange(DEC_SEQ)
    half = GDN_DK // 2
    inv_freq = (np.float32(ROPE_BASE) ** (-np.arange(0, half, 2, dtype=np.float32) / np.float32(half)))
    ang_r = (t // GRID_W).astype(np.float32)[:, None] * inv_freq
    ang_c = (t % GRID_W).astype(np.float32)[:, None] * inv_freq
    cr, sr, cc, sc = np.cos(ang_r), np.sin(ang_r), np.cos(ang_c), np.sin(ang_c)
    z = np.zeros_like(sr)
    cos = np.concatenate([cr, cr, cc, cc], axis=1)
    s_up = np.concatenate([-sr, z, -sc, z], axis=1)
    s_dn = np.concatenate([z, sr, z, sc], axis=1)
    return (jnp.asarray(cos, F32), jnp.asarray(s_up, F32), jnp.asarray(s_dn, F32))


HALO = 8
PREP_CHUNKS = 4
N_STREAM = 2 * GDN_HEADS
LOG_COL = N_STREAM


def _gdn_prep_kernel(axial, seq, *refs):
    if axial:
        (zqkv_ref, zba_ref, cw_ref, alog_ref, dtb_ref, cos_ref, sup_ref, sdn_ref,
         a_ref, qk_ref, r_ref, qd_ref, kd_ref, gl_ref, xs_ref) = refs
    else:
        (zqkv_ref, zba_ref, cw_ref, alog_ref, dtb_ref,
         a_ref, qk_ref, r_ref, qd_ref, kd_ref, gl_ref, xs_ref) = refs
    width = 3 * GDN_W
    part = pl.program_id(1)

    @pl.when(part == 0)
    def _():
        xs_ref[0:HALO, :] = jnp.zeros((HALO, width), F32)
        xs_ref[HALO + seq:HALO + seq + HALO, :] = jnp.zeros((HALO, width), F32)
        xs_ref[HALO:HALO + seq, :] = zqkv_ref[...]

    ri = lax.broadcasted_iota(jnp.int32, (CHUNK, CHUNK), 0)
    ci = lax.broadcasted_iota(jnp.int32, (CHUNK, CHUNK), 1)
    incl = (ri >= ci, ri <= ci)
    strict = (ri > ci, ri < ci)
    lane = lax.broadcasted_iota(jnp.int32, (CHUNK, LANE), 1)

    def body(i, carry):
        r0 = pl.multiple_of((part * PREP_CHUNKS + i) * CHUNK, CHUNK)
        rows = pl.ds(r0, CHUNK)
        win = xs_ref[pl.ds(r0, CHUNK + 2 * HALO), :]
        acc = None
        for j in range(CONV_K):
            lo = HALO + j - CONV_K // 2
            tap = win[lo:lo + CHUNK, :] * cw_ref[j:j + 1, :]
            acc = tap if acc is None else acc + tap
        act = _silu(acc)

        zba = zba_ref[rows, :]
        beta_all = 1.0 / (1.0 + jnp.exp(-zba))
        xa = zba + dtb_ref[...]
        softplus = jnp.maximum(xa, 0.0) + jnp.log(1.0 + jnp.exp(-jnp.abs(xa)))
        bg = jnp.where(lane < LOG_COL, beta_all, -jnp.exp(alog_ref[...]) * softplus)

        heads = []
        for hd in range(GDN_HEADS):
            qk_pair = []
            for which in range(2):
                c0 = which * GDN_W + hd * GDN_DK
                xh = act[:, c0:c0 + GDN_DK]
                xh = xh * lax.rsqrt(jnp.sum(xh * xh, axis=-1, keepdims=True) + EPS)
                if axial:
                    xh = (xh * cos_ref[rows, :]
                          + pltpu.roll(xh, GDN_DK - GDN_DK // 4, 1) * sup_ref[rows, :]
                          + pltpu.roll(xh, GDN_DK // 4, 1) * sdn_ref[rows, :])
                qk_pair.append(xh)
            q = qk_pair[0] * (GDN_DK ** -0.5)
            k = qk_pair[1]
            v = act[:, 2 * GDN_W + hd * GDN_DK:2 * GDN_W + (hd + 1) * GDN_DK]
            kbf = k.astype(BF16)
            heads.append((q, k, v, _dot_nt(kbf, kbf), _dot_nt(q.astype(BF16), kbf)))

        gcums = [_dot_exact_lhs(incl[d].astype(BF16), bg) for d in range(2)]
        gcum_ts = [g.T for g in gcums]
        for d in range(2):
            gcum, gcum_t = gcums[d], gcum_ts[d]
            last = gcum[0:1, :] if d else gcum[CHUNK - 1:CHUNK, :]
            e_in = jnp.exp(gcum)
            e_out = jnp.exp(last - gcum)
            gl_ref[i, d:d + 1, :] = jnp.exp(last)
            for hd in range(GDN_HEADS):
                q, k, v, kk, qk = heads[hd]
                cb = d * GDN_HEADS + hd
                cg = LOG_COL + cb
                beta = bg[:, cb:cb + 1]
                eg = e_in[:, cg:cg + 1]
                el = e_out[:, cg:cg + 1]
                diff = gcum[:, cg:cg + 1] - gcum_t[cg:cg + 1, :]
                decay = jnp.where(incl[d], jnp.exp(jnp.where(incl[d], diff, 0.0)), 0.0)
                a_ref[d, i, hd] = jnp.where(strict[d], beta * kk * decay, 0.0)
                qk_ref[d, i, hd] = (qk * decay).astype(BF16)
                kb = k * beta
                r_ref[d, i, hd, :, 0:GDN_DK] = (v * beta).astype(BF16)
                r_ref[d, i, hd, :, GDN_DK:2 * GDN_DK] = (kb * eg).astype(BF16)
                qd_ref[d, i, hd] = (q * eg).astype(BF16)
                kd_ref[d, i, hd] = (k * el).astype(BF16)
        return carry

    lax.fori_loop(0, PREP_CHUNKS, body, 0)


def _gdn_prep(axial, zqkv, zba, conv_w, a_log, dt_bias, rope):
    seq = DEC_SEQ if axial else SEQ
    nseq = DEC_BATCH if axial else BATCH
    nparts = seq // (PREP_CHUNKS * CHUNK)
    nchunk = nseq * seq // CHUNK
    blk0 = N_CTX // DEC_SEQ if axial else 0
    tok = lambda w: pl.BlockSpec((seq, w), lambda b, p: (blk0 + b, 0))
    full = lambda a: pl.BlockSpec(a.shape, lambda b, p: (0,) * a.ndim)
    ins = [zqkv, zba, conv_w, a_log, dt_bias]
    specs = [tok(3 * GDN_W), tok(LANE), full(conv_w), full(a_log), full(dt_bias)]
    if axial:
        ins += list(rope)
        specs += [full(t) for t in rope]
    tile = lambda w: pl.BlockSpec((2, PREP_CHUNKS, GDN_HEADS, CHUNK, w), lambda b, p: (0, b * nparts + p, 0, 0, 0))
    shape = lambda w, dt: jax.ShapeDtypeStruct((2, nchunk, GDN_HEADS, CHUNK, w), dt)
    return pl.pallas_call(
        functools.partial(_gdn_prep_kernel, axial, seq),
        out_shape=[shape(CHUNK, F32), shape(CHUNK, BF16), shape(2 * GDN_DK, BF16),
                   shape(GDN_DK, BF16), shape(GDN_DK, BF16),
                   jax.ShapeDtypeStruct((nchunk, 2, LANE), F32)],
        grid=(nseq, nparts),
        in_specs=specs,
        out_specs=[tile(CHUNK), tile(CHUNK), tile(2 * GDN_DK), tile(GDN_DK), tile(GDN_DK),
                   pl.BlockSpec((PREP_CHUNKS, 2, LANE), lambda b, p: (b * nparts + p, 0, 0))],
        scratch_shapes=[pltpu.VMEM((seq + 2 * HALO, 3 * GDN_W), F32)],
        compiler_params=_params(("parallel", "arbitrary")),
        name="gdn_prep_lat" if axial else "gdn_prep_ctx",
    )(*ins)


TRI_BLK = 8
TRI_ELEMS = CHUNK * CHUNK


def _tri_inverse_kernel(groups_per_dir, a_ref, o_ref, at_ref, tt_ref):
    backward = pl.program_id(0) >= groups_per_dir
    for blk in range(TRI_ELEMS // LANE):
        cols = slice(blk * LANE, (blk + 1) * LANE)
        at_ref[cols, :] = a_ref[:, cols].T
    tt_ref[...] = jnp.zeros(tt_ref.shape, F32)
    cidx = lax.broadcasted_iota(jnp.int32, (CHUNK, LANE), 0)

    def phys(idx):
        return jnp.where(backward, CHUNK - 1 - idx, idx)

    for ib in range(CHUNK // TRI_BLK):
        def row_body(ii, carry, ib=ib):
            i = phys(ib * TRI_BLK + ii)
            acc = (cidx == i).astype(F32)
            for jl in range((ib + 1) * TRI_BLK):
                j = phys(jl)
                arow = at_ref[pl.ds(i * CHUNK + j, 1), :]
                acc = acc - arow * tt_ref[pl.ds(pl.multiple_of(j * CHUNK, CHUNK), CHUNK), :]
            tt_ref[pl.ds(pl.multiple_of(i * CHUNK, CHUNK), CHUNK), :] = acc
            return carry

        lax.fori_loop(0, TRI_BLK, row_body, 0)

    for blk in range(TRI_ELEMS // LANE):
        cols = slice(blk * LANE, (blk + 1) * LANE)
        o_ref[:, cols] = tt_ref[cols, :].T.astype(BF16)


def _tri_inverse(a):
    nprob = a.shape[1] * a.shape[2]
    groups_per_dir = nprob // LANE
    blk = pl.BlockSpec((LANE, TRI_ELEMS), lambda g: (g, 0))
    out = pl.pallas_call(
        functools.partial(_tri_inverse_kernel, groups_per_dir),
        out_shape=jax.ShapeDtypeStruct((2 * nprob, TRI_ELEMS), BF16),
        grid=(2 * groups_per_dir,),
        in_specs=[blk],
        out_specs=blk,
        scratch_shapes=[pltpu.VMEM((TRI_ELEMS, LANE), F32), pltpu.VMEM((TRI_ELEMS, LANE), F32)],
        compiler_params=_params(("parallel",)),
        name="tri_inverse",
    )(a.reshape(2 * nprob, TRI_ELEMS))
    return out.reshape(a.shape)


def _gdn_scan_kernel(nchunk, t_ref, qk_ref, r_ref, qd_ref, kd_ref, gl_ref, s0_ref, *rest):
    o_ref, sfin_ref, state_ref = rest[-3:]
    state_ref[...] = s0_ref[...]
    streams = [(d, hd) for d in range(2) for hd in range(GDN_HEADS)]

    def body(c, carry):
        cc = (c, nchunk - 1 - c)
        uws = [_dot(t_ref[d, cc[d], hd], r_ref[d, cc[d], hd]).astype(BF16) for d, hd in streams]
        mbs = [_dot_tn(kd_ref[d, cc[d], hd], uw) for (d, hd), uw in zip(streams, uws)]
        qos = [_dot(qk_ref[d, cc[d], hd], uw) for (d, hd), uw in zip(streams, uws)]
        sts = [state_ref[d * GDN_HEADS + hd] for d, hd in streams]
        sbs = [s.astype(BF16) for s in sts]
        for (d, hd), qo, sb in zip(streams, qos, sbs):
            rows = pl.ds(pl.multiple_of(cc[d] * CHUNK, CHUNK), CHUNK)
            qp = (qd_ref[d, cc[d], hd].astype(F32) - qo[:, GDN_DK:]).astype(BF16)
            o_ref[d, rows, hd * GDN_DK:(hd + 1) * GDN_DK] = _dot(qp, sb) + qo[:, :GDN_DK]
        for (d, hd), mb, s, sb in zip(streams, mbs, sts, sbs):
            sidx = d * GDN_HEADS + hd
            gl = gl_ref[cc[d]][d:d + 1, LOG_COL + sidx:LOG_COL + sidx + 1]
            state_ref[sidx] = s * gl - _dot(mb[:, GDN_DK:].astype(BF16), sb) + mb[:, :GDN_DK]
        return carry

    lax.fori_loop(0, nchunk, body, 0)
    sfin_ref[...] = state_ref[...]


def _gdn_scan(seq, nseq, tinv, qk, r, qd, kd, gl, s0, o_all=None):
    nc = seq // CHUNK
    blk0 = 0 if o_all is None else N_CTX // seq
    tile = lambda w: pl.BlockSpec((2, nc, GDN_HEADS, CHUNK, w), lambda b: (0, b, 0, 0, 0))
    sblk = pl.BlockSpec((None, N_STREAM, GDN_DK, GDN_DK), lambda b: (b, 0, 0, 0))
    oblk = pl.BlockSpec((2, seq, GDN_W), lambda b: (0, blk0 + b, 0))
    ins = [tinv, qk, r, qd, kd, gl, s0]
    specs = [tile(CHUNK), tile(CHUNK), tile(2 * GDN_DK), tile(GDN_DK), tile(GDN_DK),
             pl.BlockSpec((nc, 2, LANE), lambda b: (b, 0, 0)), sblk]
    aliases = {}
    if o_all is not None:
        ins.append(o_all)
        specs.append(pl.BlockSpec(memory_space=pl.ANY))
        aliases = {len(ins) - 1: 0}
    return pl.pallas_call(
        functools.partial(_gdn_scan_kernel, nc),
        out_shape=[jax.ShapeDtypeStruct((2, N_TOK, GDN_W), F32),
                   jax.ShapeDtypeStruct((nseq, N_STREAM, GDN_DK, GDN_DK), F32)],
        grid=(nseq,),
        in_specs=specs,
        out_specs=[oblk, sblk],
        scratch_shapes=[pltpu.VMEM((N_STREAM, GDN_DK, GDN_DK), F32)],
        input_output_aliases=aliases,
        compiler_params=_params(("parallel",)),
        name="gdn_scan_lat" if seq == DEC_SEQ else "gdn_scan_ctx",
    )(*ins)


def _first_max(vals):
    sel = []
    taken = None
    for a, va in enumerate(vals):
        is_max = None
        for b, vb in enumerate(vals):
            if a == b:
                continue
            c = va >= vb
            is_max = c if is_max is None else (is_max & c)
        if taken is not None:
            is_max = is_max & jnp.logical_not(taken)
        sel.append(is_max)
        taken = is_max if taken is None else (taken | is_max)
    return sel


PAIR_SLOTS = ((0, 1), (0, 2), (0, 3), (1, 3), (1, 2), (3, 2))
N_CLASS = N_GROUPS * len(PAIR_SLOTS)
CLASS_EXPERTS = tuple((g * EXP_PER_GROUP + a, g * EXP_PER_GROUP + b)
                      for g in range(N_GROUPS) for a, b in PAIR_SLOTS)


def _route_rows(lt):
    gl = [lt[a:a + 1, :] for a in range(N_GROUPS)]
    gsel = _first_max(gl)
    gmax = functools.reduce(jnp.maximum, gl)
    gden = functools.reduce(lambda x, y: x + y, [jnp.exp(x - gmax) for x in gl])
    g_w = 1.0 / gden
    el = []
    for kx in range(EXP_PER_GROUP):
        acc = None
        for a in range(N_GROUPS):
            row = N_GROUPS + a * EXP_PER_GROUP + kx
            term = jnp.where(gsel[a], lt[row:row + 1, :], 0.0)
            acc = term if acc is None else acc + term
        el.append(acc)
    emax = functools.reduce(jnp.maximum, el)
    ee = [jnp.exp(x - emax) for x in el]
    eden = functools.reduce(lambda x, y: x + y, ee)
    ep = [x / eden for x in ee]
    top1 = _first_max(ep)
    ep2 = [jnp.where(top1[kx], -1.0, ep[kx]) for kx in range(EXP_PER_GROUP)]
    top2 = _first_max(ep2)
    chosen = [top1[kx] | top2[kx] for kx in range(EXP_PER_GROUP)]
    wsum = functools.reduce(lambda x, y: x + y,
                            [jnp.where(chosen[kx], ep[kx], 0.0) for kx in range(EXP_PER_GROUP)])
    within = [jnp.where(chosen[kx], ep[kx] / wsum, 0.0) for kx in range(EXP_PER_GROUP)]
    cls = jnp.zeros(g_w.shape, jnp.int32)
    w_a = jnp.zeros(g_w.shape, F32)
    w_b = jnp.zeros(g_w.shape, F32)
    for g in range(N_GROUPS):
        for kx, (a, b) in enumerate(PAIR_SLOTS):
            hit = gsel[g] & chosen[a] & chosen[b]
            cls = jnp.where(hit, g * len(PAIR_SLOTS) + kx, cls)
            w_a = jnp.where(hit, g_w * within[a], w_a)
            w_b = jnp.where(hit, g_w * within[b], w_b)
    return cls, w_a, w_b


H2W = D_MODEL + LANE
COL_WA = D_MODEL
COL_WB = D_MODEL + 1


def _outproj_kernel(idx_ref, x_ref, ona_ref, og_ref, zg_ref, mod_ref, wo_ref, gng_ref, n2g_ref,
                    wr_ref, br_ref, x1_ref, h2x_ref, cls_ref):
    del idx_ref
    m = mod_ref[...]
    og = og_ref[0] + og_ref[1]
    zg = zg_ref[...]
    parts = []
    for hd in range(GDN_HEADS):
        cols = slice(hd * GDN_DK, (hd + 1) * GDN_DK)
        oh = og[:, cols]
        oh = oh * lax.rsqrt(jnp.mean(oh * oh, axis=-1, keepdims=True) + EPS) * gng_ref[...]
        parts.append((oh * _silu(zg[:, cols])).astype(BF16))
    mix = _dot(ona_ref[...].astype(BF16), wo_ref[0:NA_W, :])
    for hd in range(GDN_HEADS):
        r0 = NA_W + hd * GDN_DK
        mix = mix + _dot(parts[hd], wo_ref[r0:r0 + GDN_DK, :])
    x1 = x_ref[...] + m[2:3, :] * mix
    x1_ref[...] = x1
    y = x1 * lax.rsqrt(jnp.mean(x1 * x1, axis=-1, keepdims=True) + EPS) * n2g_ref[...]
    h2 = y * (1.0 + m[4:5, :]) + m[3:4, :]
    h2x_ref[:, 0:D_MODEL] = h2.astype(BF16).astype(F32)
    logits = _dot3(h2, wr_ref[...]) + br_ref[...]
    cls, w_a, w_b = _route_rows(logits.T)
    rid = lax.broadcasted_iota(jnp.int32, (8, TM), 0)
    wt = jnp.where(rid == COL_WA - D_MODEL, w_a, jnp.where(rid == COL_WB - D_MODEL, w_b, 0.0))
    wt = jnp.concatenate([wt, jnp.zeros((LANE - 8, TM), F32)], axis=0)
    h2x_ref[:, D_MODEL:H2W] = wt.T
    cls_ref[...] = jnp.broadcast_to(cls, (8, TM))


def _outproj(tile_mod, x, o_na, o_gdn, zg, mod, w_out, gng, n2g, w_r, b_r):
    tok = lambda w: pl.BlockSpec((TM, w), lambda t, idx: (t, 0))
    full = lambda a: pl.BlockSpec(a.shape, lambda t, idx: (0,) * a.ndim)
    return pl.pallas_call(
        _outproj_kernel,
        out_shape=[jax.ShapeDtypeStruct((N_TOK, D_MODEL), F32),
                   jax.ShapeDtypeStruct((N_TOK, H2W), F32),
                   jax.ShapeDtypeStruct((N_TOK // TM, 8, TM), jnp.int32)],
        grid_spec=pltpu.PrefetchScalarGridSpec(
            num_scalar_prefetch=1,
            grid=(N_TOK // TM,),
            in_specs=[
                tok(D_MODEL), tok(NA_W),
                pl.BlockSpec((2, TM, GDN_W), lambda t, idx: (0, t, 0)), tok(GDN_W),
                pl.BlockSpec((None, 6, D_MODEL), lambda t, idx: (idx[t], 0, 0)),
                full(w_out), full(gng), full(n2g), full(w_r), full(b_r),
            ],
            out_specs=[tok(D_MODEL), tok(H2W), pl.BlockSpec((None, 8, TM), lambda t, idx: (t, 0, 0))],
        ),
        compiler_params=_params(("parallel",)),
        name="outproj",
    )(tile_mod, x, o_na, o_gdn, zg, mod, w_out, gng, n2g, w_r, b_r)


TM_X = 256
N_XTILE = N_TOK // TM_X + N_CLASS
N_SLOT = N_XTILE * TM_X
CLS_ROWS = N_TOK // LANE
TAB_EA, TAB_EB, TAB_USED = 0, 1, 2


def _route_pos_kernel(cls_ref, slot_ref, tab_ref):
    cls = cls_ref[...]
    li = lax.broadcasted_iota(jnp.int32, (LANE, LANE), 0)
    lj = lax.broadcasted_iota(jnp.int32, (LANE, LANE), 1)
    before_lane = (li < lj).astype(BF16)
    ri = lax.broadcasted_iota(jnp.int32, (CLS_ROWS, CLS_ROWS), 0)
    rj = lax.broadcasted_iota(jnp.int32, (CLS_ROWS, CLS_ROWS), 1)
    before_row = (rj < ri).astype(BF16)
    tile_start = lax.broadcasted_iota(jnp.int32, (1, LANE), 1) * TM_X
    off = jnp.zeros((1, 1), jnp.int32)
    slot = jnp.zeros(cls.shape, jnp.int32)
    tab_a = jnp.zeros((1, LANE), jnp.int32)
    tab_b = jnp.zeros((1, LANE), jnp.int32)
    for c in range(N_CLASS):
        hit = cls == c
        one = hit.astype(F32)
        in_row = _dot(one.astype(BF16), before_lane)
        row_tot = jnp.broadcast_to(jnp.sum(one, axis=1, keepdims=True), one.shape)
        rank = in_row + _dot(before_row, row_tot.astype(BF16))
        count = jnp.sum(row_tot[:, 0:1], axis=0, keepdims=True).astype(jnp.int32)
        slot = jnp.where(hit, off + rank.astype(jnp.int32), slot)
        nxt = off + (((count + (TM_X - 1)) >> 8) << 8)
        mine = (tile_start >= off) & (tile_start < nxt)
        tab_a = jnp.where(mine, CLASS_EXPERTS[c][0], tab_a)
        tab_b = jnp.where(mine, CLASS_EXPERTS[c][1], tab_b)
        off = nxt
    slot_ref[...] = slot
    row = lax.broadcasted_iota(jnp.int32, (8, LANE), 0)
    used = jnp.broadcast_to(off >> 8, (8, LANE))
    tab_ref[...] = jnp.where(row == TAB_EA, tab_a, jnp.where(row == TAB_EB, tab_b, used))


def _route_positions(cls):
    assert TM_X == 256
    return pl.pallas_call(
        _route_pos_kernel,
        out_shape=[jax.ShapeDtypeStruct((CLS_ROWS, LANE), jnp.int32),
                   jax.ShapeDtypeStruct((8, LANE), jnp.int32)],
        name="route_positions",
    )(cls)


def _row_copy(src, src_row, dst, dst_row, sem):
    return pltpu.make_async_copy(src.at[pl.ds(src_row, 1), :], dst.at[pl.ds(dst_row, 1), :], sem)


ROW_UNROLL = 8


def _permute_kernel(slot_ref, h_ref, xs_in_ref, xs_ref, buf_ref, sem_ref):
    del xs_in_ref
    t = pl.program_id(0)
    cur = t % 2
    buf_ref[cur] = h_ref[...]

    def issue(i, carry):
        _row_copy(buf_ref.at[cur], i, xs_ref, slot_ref[t * TM + i], sem_ref.at[cur]).start()
        return carry

    lax.fori_loop(0, TM, issue, 0, unroll=ROW_UNROLL)

    def drain(which):
        def wait(i, carry):
            _row_copy(buf_ref.at[which], 0, xs_ref, 0, sem_ref.at[which]).wait()
            return carry
        lax.fori_loop(0, TM, wait, 0, unroll=ROW_UNROLL)

    @pl.when(t > 0)
    def _():
        drain(1 - cur)

    @pl.when(t == pl.num_programs(0) - 1)
    def _():
        drain(cur)


def _permute(slot, h2x):
    xs0 = jnp.zeros((N_SLOT, H2W), F32)
    return pl.pallas_call(
        _permute_kernel,
        out_shape=jax.ShapeDtypeStruct((N_SLOT, H2W), F32),
        grid_spec=pltpu.PrefetchScalarGridSpec(
            num_scalar_prefetch=1,
            grid=(N_TOK // TM,),
            in_specs=[pl.BlockSpec((TM, H2W), lambda t, s: (t, 0)), pl.BlockSpec(memory_space=pl.ANY)],
            out_specs=pl.BlockSpec(memory_space=pl.ANY),
            scratch_shapes=[pltpu.VMEM((2, TM, H2W), F32), pltpu.SemaphoreType.DMA((2,))],
        ),
        input_output_aliases={2: 0},
        compiler_params=_params(("arbitrary",)),
        name="moe_permute",
    )(slot, h2x, xs0)


def _experts_kernel(tab_ref, xs_ref, w1a_ref, w3a_ref, w2a_ref, w1b_ref, w3b_ref, w2b_ref, ys_ref,
                    c1a_ref, c3a_ref, c2a_ref, c1b_ref, c3b_ref, c2b_ref):
    t = pl.program_id(0)
    prev = jnp.maximum(t - 1, 0)

    @pl.when(t < tab_ref[TAB_USED, 0])
    def _():
        @pl.when((t == 0) | (tab_ref[TAB_EA, t] != tab_ref[TAB_EA, prev]))
        def _():
            c1a_ref[...] = w1a_ref[...].astype(BF16)
            c3a_ref[...] = w3a_ref[...].astype(BF16)
            c2a_ref[...] = w2a_ref[...].astype(BF16)

        @pl.when((t == 0) | (tab_ref[TAB_EB, t] != tab_ref[TAB_EB, prev]))
        def _():
            c1b_ref[...] = w1b_ref[...].astype(BF16)
            c3b_ref[...] = w3b_ref[...].astype(BF16)
            c2b_ref[...] = w2b_ref[...].astype(BF16)

        x = xs_ref[:, 0:D_MODEL].astype(BF16)
        gates = xs_ref[:, D_MODEL:H2W]
        w_a = gates[:, COL_WA - D_MODEL:COL_WA - D_MODEL + 1]
        w_b = gates[:, COL_WB - D_MODEL:COL_WB - D_MODEL + 1]
        a1 = _dot(x, c1a_ref[...])
        a3 = _dot(x, c3a_ref[...])
        b1 = _dot(x, c1b_ref[...])
        b3 = _dot(x, c3b_ref[...])
        sa = (_silu(a1) * a3 * w_a).astype(BF16)
        sb = (_silu(b1) * b3 * w_b).astype(BF16)
        ys_ref[...] = _dot(sa, c2a_ref[...]) + _dot(sb, c2b_ref[...])


def _experts(layer, tab, xs, w1, w3, w2):
    last = lambda t, tab: jnp.minimum(t, tab[TAB_USED, 0] - 1)
    tile = lambda w: pl.BlockSpec((TM_X, w), lambda t, tab: (last(t, tab), 0))
    wspec = lambda shape, row: pl.BlockSpec((None, None) + shape,
                                            lambda t, tab: (layer, tab[row, last(t, tab)], 0, 0))
    up, down = (D_MODEL, EXPERT_FF), (EXPERT_FF, D_MODEL)
    return pl.pallas_call(
        _experts_kernel,
        out_shape=jax.ShapeDtypeStruct((N_SLOT, D_MODEL), F32),
        grid_spec=pltpu.PrefetchScalarGridSpec(
            num_scalar_prefetch=1,
            grid=(N_XTILE,),
            in_specs=[tile(H2W),
                      wspec(up, TAB_EA), wspec(up, TAB_EA), wspec(down, TAB_EA),
                      wspec(up, TAB_EB), wspec(up, TAB_EB), wspec(down, TAB_EB)],
            out_specs=tile(D_MODEL),
            scratch_shapes=[pltpu.VMEM(up, BF16), pltpu.VMEM(up, BF16), pltpu.VMEM(down, BF16),
                            pltpu.VMEM(up, BF16), pltpu.VMEM(up, BF16), pltpu.VMEM(down, BF16)],
        ),
        compiler_params=_params(("arbitrary",)),
        name="moe_experts",
    )(tab, xs, w1, w3, w2, w1, w3, w2)


def _unpermute_kernel(idx_ref, slot_ref, x1_ref, mod_ref, ys_ref, o_ref, buf_ref, sem_ref):
    del idx_ref
    t = pl.program_id(0)
    cur = t % 2

    def fetch(tile, which):
        def issue(i, carry):
            _row_copy(ys_ref, slot_ref[tile * TM + i], buf_ref.at[which], i, sem_ref.at[which]).start()
            return carry
        lax.fori_loop(0, TM, issue, 0, unroll=ROW_UNROLL)

    @pl.when(t == 0)
    def _():
        fetch(0, 0)

    @pl.when(t + 1 < pl.num_programs(0))
    def _():
        fetch(t + 1, 1 - cur)

    def wait(i, carry):
        _row_copy(ys_ref, 0, buf_ref.at[cur], 0, sem_ref.at[cur]).wait()
        return carry

    lax.fori_loop(0, TM, wait, 0, unroll=ROW_UNROLL)
    o_ref[...] = x1_ref[...] + mod_ref[5:6, :] * buf_ref[cur]


def _unpermute(tile_mod, slot, x1, mod, ys):
    tok = pl.BlockSpec((TM, D_MODEL), lambda t, idx, s: (t, 0))
    return pl.pallas_call(
        _unpermute_kernel,
        out_shape=jax.ShapeDtypeStruct((N_TOK, D_MODEL), F32),
        grid_spec=pltpu.PrefetchScalarGridSpec(
            num_scalar_prefetch=2,
            grid=(N_TOK // TM,),
            in_specs=[tok, pl.BlockSpec((None, 6, D_MODEL), lambda t, idx, s: (idx[t], 0, 0)),
                      pl.BlockSpec(memory_space=pl.ANY)],
            out_specs=tok,
            scratch_shapes=[pltpu.VMEM((2, TM, D_MODEL), F32), pltpu.SemaphoreType.DMA((2,))],
        ),
        compiler_params=_params(("arbitrary",)),
        name="moe_unpermute",
    )(tile_mod, slot, x1, mod, ys)


def _tile_mod_index(tile):
    t = np.arange(N_TOK // tile) * tile
    return jnp.asarray(np.where(t < N_CTX, 0, 1 + (t - N_CTX) // DEC_SEQ), jnp.int32)


def kernel(x_prompt, x_sample, c, cache_k, cache_v, state_ssm, c_ctx, ada_w, ada_b, norm1_g, norm2_g, w_in, w_out, na_qn_g, na_kn_g, na_rpb, gdn_conv_w, gdn_a_log, gdn_dt_bias, gdn_norm_g, moe_w_rg, moe_b_rg, moe_w_re, moe_b_re, moe_w1, moe_w3, moe_w2):
    x = jnp.concatenate([x_prompt.reshape(N_CTX, D_MODEL), x_sample.reshape(N_LAT, D_MODEL)], axis=0)
    cv = jnp.concatenate([c_ctx[None, :], c, jnp.zeros((N_MOD_PAD - N_MOD, D_MODEL), F32)], axis=0)
    mod_all = _modulation(cv, ada_w, ada_b).reshape(DEPTH, N_MOD_PAD, 6, D_MODEL)

    idx_tm = _tile_mod_index(TM)
    hh = np.arange(NA_W) // NA_HD
    ones_bd = jnp.asarray(hh[:, None] == hh[None, :], BF16)
    rope = _rope_tables()
    cache_k4 = cache_k.reshape(DEC_BATCH, DEPTH, PAST_LEN, NA_W)
    cache_v4 = cache_v.reshape(DEC_BATCH, DEPTH, PAST_LEN, NA_W)
    lane_pad = lambda a, at: jnp.zeros((1, LANE), F32).at[0, at:at + a.size].set(a.reshape(-1))
    zeros_state = jnp.zeros((BATCH, 2 * GDN_HEADS, GDN_DK, GDN_DK), F32)

    caches, ss = None, []
    for l in range(DEPTH):
        mod = mod_all[l]
        w_main = w_in[l, :, :N_MAIN].astype(BF16)
        w_ba = jnp.zeros((D_MODEL, LANE), BF16).at[:, :N_IN - N_MAIN].set(w_in[l, :, N_MAIN:].astype(BF16))
        qg = jnp.tile(na_qn_g[l], NA_HEADS)[None, :]
        kg = jnp.tile(na_kn_g[l], NA_HEADS)[None, :]
        q, k, v, zqkv, zg, zba, *caches = _inproj(l, idx_tm, x, mod, norm1_g[l][None, :], w_main, w_ba,
                                                  ones_bd, qg, kg, caches)

        bias = _na_bias_table(na_rpb[l])
        o_na = _na_attention(l, q, k, v, cache_k4, cache_v4, bias, _ctx_attention(q, k, v))

        conv_w = jnp.zeros((8, 3 * GDN_W), F32).at[:CONV_K].set(gdn_conv_w[l])
        a_row = lane_pad(gdn_a_log[l], 2 * GDN_HEADS)
        dt_row = lane_pad(gdn_dt_bias[l], 2 * GDN_HEADS)
        a_c, *ops_c = _gdn_prep(False, zqkv, zba, conv_w, a_row, dt_row, None)
        a_l, *ops_l = _gdn_prep(True, zqkv, zba, conv_w, a_row, dt_row, rope)
        og_c, s_ctx = _gdn_scan(SEQ, BATCH, _tri_inverse(a_c), *ops_c, zeros_state)
        s0_lat = state_ssm[:, l].reshape(DEC_BATCH, N_STREAM, GDN_DK, GDN_DK)
        o_gdn, _ = _gdn_scan(DEC_SEQ, DEC_BATCH, _tri_inverse(a_l), *ops_l, s0_lat, og_c)

        w_r = jnp.zeros((D_MODEL, LANE), F32).at[:, :N_GROUPS].set(moe_w_rg[l])
        w_r = w_r.at[:, N_GROUPS:N_GROUPS + N_EXPERTS].set(moe_w_re[l])
        b_r = lane_pad(jnp.concatenate([moe_b_rg[l], moe_b_re[l]]), 0)
        x1, h2x, cls = _outproj(idx_tm, x, o_na, o_gdn, zg, mod, w_out[l].astype(BF16),
                                gdn_norm_g[l][None, :], norm2_g[l][None, :], w_r, b_r)
        slot, tab = _route_positions(cls[:, 0, :].reshape(CLS_ROWS, LANE))
        slot = slot.reshape(N_TOK)
        ys = _experts(l, tab, _permute(slot, h2x), moe_w1, moe_w3, moe_w2)
        x = _unpermute(idx_tm, slot, x1, mod, ys)

        ss.append(s_ctx.reshape(BATCH, 2, GDN_HEADS, GDN_DK, GDN_DK))

    y_prompt = x[:N_CTX].reshape(BATCH, SEQ, D_MODEL)
    y_sample = x[N_CTX:].reshape(DEC_BATCH, DEC_SEQ, D_MODEL)
    new_k, new_v = (a.reshape(BATCH, DEPTH, SEQ, NA_HEADS, NA_HD) for a in caches)
    return (y_prompt, y_sample, new_k, new_v, jnp.stack(ss, axis=1))
```

```python
import functools

import jax
import jax.numpy as jnp
import numpy as np
from jax import lax
from jax.experimental import pallas as pl
from jax.experimental.pallas import tpu as pltpu

F32 = jnp.float32
BF16 = jnp.bfloat16

D_MODEL = 1024
BATCH = 16
SEQ = 256
DEPTH = 2
DEC_BATCH = 8
DEC_SEQ = 1024
PAST_LEN = 512
GRID_W = 64
GRID_ROWS = DEC_SEQ // GRID_W
NA_HEADS = 8
NA_HD = 64
NA_W = NA_HEADS * NA_HD
NA_KH = 8
NA_KW = 16
GDN_HEADS = 4
GDN_DK = 128
GDN_W = GDN_HEADS * GDN_DK
CONV_K = 5
CHUNK = 64
ROPE_BASE = 10000.0
N_GROUPS = 4
EXP_PER_GROUP = 4
N_EXPERTS = 16
EXPERT_FF = 512
EPS = 1e-6
N_IN = 3 * NA_W + 4 * GDN_W + 4 * GDN_HEADS

N_CTX = BATCH * SEQ
N_LAT = DEC_BATCH * DEC_SEQ
N_TOK = N_CTX + N_LAT
N_MOD = 1 + DEC_BATCH
N_MOD_PAD = 16
LANE = 128
N_MAIN = 3 * NA_W + 4 * GDN_W
TM = 256
VMEM_LIMIT = 56 * 1024 * 1024


def _dot(a, b):
    return jnp.dot(a, b, preferred_element_type=F32)


def _dot_nt(a, b):
    return lax.dot_general(a, b, (((1,), (1,)), ((), ())), preferred_element_type=F32)


def _dot_tn(a, b):
    return lax.dot_general(a, b, (((0,), (0,)), ((), ())), preferred_element_type=F32)


def _split2(x):
    hi = x.astype(BF16)
    lo = (x - hi.astype(F32)).astype(BF16)
    return hi, lo


def _split3(x):
    hi = x.astype(BF16)
    r = x - hi.astype(F32)
    mid = r.astype(BF16)
    lo = (r - mid.astype(F32)).astype(BF16)
    return hi, mid, lo


def _dot3(a, b):
    ah, al = _split2(a)
    bh, bl = _split2(b)
    return _dot(ah, bh) + (_dot(ah, bl) + _dot(al, bh))


def _dot_exact_lhs(mask_bf16, x):
    hi, mid, lo = _split3(x)
    return _dot(mask_bf16, hi) + (_dot(mask_bf16, mid) + _dot(mask_bf16, lo))


def _silu(x):
    return x * (1.0 / (1.0 + jnp.exp(-x)))


def _params(sem):
    return pltpu.CompilerParams(dimension_semantics=sem, vmem_limit_bytes=VMEM_LIMIT)


def _mod_kernel(cv_ref, w_ref, b_ref, o_ref):
    cv = cv_ref[...]
    s = _silu(cv)
    o_ref[...] = jnp.dot(s, w_ref[...], preferred_element_type=F32,
                         precision=lax.Precision.HIGHEST) + b_ref[...]


def _modulation(cv, ada_w, ada_b):
    nblk = 512
    return pl.pallas_call(
        _mod_kernel,
        out_shape=jax.ShapeDtypeStruct((DEPTH, N_MOD_PAD, 6 * D_MODEL), F32),
        grid=(DEPTH, 6 * D_MODEL // nblk),
        in_specs=[
            pl.BlockSpec((N_MOD_PAD, D_MODEL), lambda l, j: (0, 0)),
            pl.BlockSpec((None, D_MODEL, nblk), lambda l, j: (l, 0, j)),
            pl.BlockSpec((None, 1, nblk), lambda l, j: (l, 0, j)),
        ],
        out_specs=pl.BlockSpec((None, N_MOD_PAD, nblk), lambda l, j: (l, 0, j)),
        compiler_params=_params(("parallel", "parallel")),
        name="modulation",
    )(cv, ada_w, ada_b.reshape(DEPTH, 1, 6 * D_MODEL))


def _token_specs(x):
    if not isinstance(x, tuple):
        return [pl.BlockSpec((TM, D_MODEL), lambda t, *_: (t, 0))]
    return [pl.BlockSpec((TM, D_MODEL), lambda t, *_: (jnp.minimum(t, N_CTX // TM - 1), 0)),
            pl.BlockSpec((TM, D_MODEL), lambda t, *_: (jnp.maximum(t - N_CTX // TM, 0), 0))]


def _token_tile(nx, refs):
    if nx == 1:
        return refs[0][...]
    return jnp.where(pl.program_id(0) < N_CTX // TM, refs[0][...], refs[1][...])


def _inproj_kernel(nx, idx_ref, *refs):
    x = _token_tile(nx, refs)
    mod_ref, n1g_ref, win_ref, wba_ref, ones_ref, qg_ref, kg_ref = refs[nx:nx + 7]
    q_ref, k_ref, v_ref, zqkv_ref, zg_ref, zba_ref, kcache_ref, vcache_ref = refs[-8:]
    del idx_ref
    y = x * lax.rsqrt(jnp.mean(x * x, axis=-1, keepdims=True) + EPS) * n1g_ref[...]
    m = mod_ref[...]
    h = (y * (1.0 + m[1:2, :]) + m[0:1, :]).astype(BF16)
    z = _dot(h, win_ref[...])
    zba_ref[...] = _dot(h, wba_ref[...])
    ones = ones_ref[...]

    def head_rms(zz, g):
        hi, lo = _split2(zz * zz)
        ms = (_dot(hi, ones) + _dot(lo, ones)) * (1.0 / NA_HD)
        return zz * lax.rsqrt(ms + EPS) * g

    q_ref[...] = head_rms(z[:, 0:NA_W], qg_ref[...]) * (NA_HD ** -0.5)
    k = head_rms(z[:, NA_W:2 * NA_W], kg_ref[...])
    v = z[:, 2 * NA_W:3 * NA_W]
    k_ref[...] = k
    v_ref[...] = v
    zqkv_ref[...] = z[:, 3 * NA_W:3 * NA_W + 3 * GDN_W]
    zg_ref[...] = z[:, 3 * NA_W + 3 * GDN_W:N_MAIN]

    @pl.when(pl.program_id(0) < BATCH)
    def _():
        kcache_ref[...] = k
        vcache_ref[...] = v


def _inproj(layer, tile_mod, x, mod, n1g, w_main, w_ba, ones_bd, qg, kg, caches):
    assert TM == SEQ
    tok = lambda w: pl.BlockSpec((TM, w), lambda t, idx: (t, 0))
    full = lambda a: pl.BlockSpec(a.shape, lambda t, idx: (0,) * a.ndim)
    out_w = (NA_W, NA_W, NA_W, 3 * GDN_W, GDN_W, LANE)
    cache_blk = pl.BlockSpec((None, None, SEQ, NA_W), lambda t, idx: (jnp.minimum(t, BATCH - 1), layer, 0, 0))
    cache_shape = jax.ShapeDtypeStruct((BATCH, DEPTH, SEQ, NA_W), F32)
    xs = list(x) if isinstance(x, tuple) else [x]
    ins = [tile_mod, *xs, mod, n1g, w_main, w_ba, ones_bd, qg, kg]
    specs = _token_specs(x) + [pl.BlockSpec((None, 6, D_MODEL), lambda t, idx: (idx[t], 0, 0)),
                               full(n1g), full(w_main), full(w_ba), full(ones_bd), full(qg), full(kg)]
    aliases = {}
    if caches is not None:
        aliases = {len(ins): len(out_w), len(ins) + 1: len(out_w) + 1}
        ins += list(caches)
        specs += [pl.BlockSpec(memory_space=pl.ANY)] * 2
    return pl.pallas_call(
        functools.partial(_inproj_kernel, len(xs)),
        out_shape=[jax.ShapeDtypeStruct((N_TOK, w), F32) for w in out_w] + [cache_shape, cache_shape],
        grid_spec=pltpu.PrefetchScalarGridSpec(
            num_scalar_prefetch=1,
            grid=(N_TOK // TM,),
            in_specs=specs,
            out_specs=[tok(w) for w in out_w] + [cache_blk, cache_blk],
        ),
        input_output_aliases=aliases,
        compiler_params=_params(("arbitrary",)),
        name="inproj",
    )(*ins)


def _pair_masks(shape):
    lane = lax.broadcasted_iota(jnp.int32, shape, 1)
    return lane < NA_HD


def _ctx_attn_kernel(q_ref, k_ref, v_ref, o_ref):
    for p in range(NA_HEADS // 2):
        cols = slice(p * LANE, (p + 1) * LANE)
        q = q_ref[:, cols]
        kb = k_ref[:, cols].astype(BF16)
        vb = v_ref[:, cols].astype(BF16)
        first = _pair_masks(q.shape)
        outs = []
        for hm in (first, jnp.logical_not(first)):
            s = _dot_nt(jnp.where(hm, q, 0.0).astype(BF16), kb)
            e = jnp.exp(s - jnp.max(s, axis=-1, keepdims=True))
            o = _dot(e.astype(BF16), vb)
            outs.append(o / jnp.sum(e, axis=-1, keepdims=True))
        o_ref[:, cols] = jnp.where(first, outs[0], outs[1])


def _ctx_attention(q, k, v):
    blk = pl.BlockSpec((SEQ, NA_W), lambda b: (b, 0))
    return pl.pallas_call(
        _ctx_attn_kernel,
        out_shape=jax.ShapeDtypeStruct((N_TOK, NA_W), F32),
        grid=(BATCH,),
        in_specs=[blk, blk, blk],
        out_specs=blk,
        compiler_params=_params(("parallel",)),
        name="ctx_attention",
    )(q, k, v)


NA_PAIR_GROUP = 2


def _na_kernel(q_ref, k_ref, v_ref, kc_ref, vc_ref, bias_ref, o_all_ref, o_ref):
    del o_all_ref
    r = pl.program_id(1)
    base = jnp.clip(r - NA_KH // 2, 0, GRID_ROWS - NA_KH)
    dr0 = base - r + (NA_KH - 1)
    row0 = pl.multiple_of(base * GRID_W, GRID_W)
    nwin = NA_KH * GRID_W
    first = _pair_masks((GRID_W, LANE))
    col = lambda p: slice(p * LANE, (p + 1) * LANE)
    for g0 in range(0, NA_HEADS // 2, NA_PAIR_GROUP):
        pairs = range(g0, g0 + NA_PAIR_GROUP)
        heads = range(2 * g0, 2 * (g0 + NA_PAIR_GROUP))
        qh = {h: jnp.where(first if h % 2 == 0 else jnp.logical_not(first), q_ref[:, col(h // 2)], 0.0).astype(BF16)
              for h in heads}
        kw = {p: k_ref[pl.ds(row0, nwin), col(p)].astype(BF16) for p in pairs}
        kc = {p: kc_ref[:, col(p)].astype(BF16) for p in pairs}
        vw = {p: v_ref[pl.ds(row0, nwin), col(p)].astype(BF16) for p in pairs}
        vc = {p: vc_ref[:, col(p)].astype(BF16) for p in pairs}
        s_loc = {h: _dot_nt(qh[h], kw[h // 2]) + bias_ref[h, dr0] for h in heads}
        s_ctx = {h: _dot_nt(qh[h], kc[h // 2]) for h in heads}
        mx = {h: jnp.maximum(jnp.max(s_loc[h], axis=-1, keepdims=True), jnp.max(s_ctx[h], axis=-1, keepdims=True))
              for h in heads}
        e_loc = {h: jnp.exp(s_loc[h] - mx[h]) for h in heads}
        e_ctx = {h: jnp.exp(s_ctx[h] - mx[h]) for h in heads}
        den = {h: jnp.sum(e_loc[h], axis=-1, keepdims=True) + jnp.sum(e_ctx[h], axis=-1, keepdims=True)
               for h in heads}
        o = {h: _dot(e_loc[h].astype(BF16), vw[h // 2]) + _dot(e_ctx[h].astype(BF16), vc[h // 2]) for h in heads}
        for p in pairs:
            o_ref[:, col(p)] = jnp.where(first, o[2 * p] / den[2 * p], o[2 * p + 1] / den[2 * p + 1])


def _na_attention(layer, q, k, v, cache_k, cache_v, bias, o_all):
    lat0 = N_CTX // DEC_SEQ
    qblk = pl.BlockSpec((GRID_W, NA_W), lambda b, r: (N_CTX // GRID_W + b * GRID_ROWS + r, 0))
    kvblk = pl.BlockSpec((DEC_SEQ, NA_W), lambda b, r: (lat0 + b, 0))
    cblk = pl.BlockSpec((None, None, PAST_LEN, NA_W), lambda b, r: (b, layer, 0, 0))
    return pl.pallas_call(
        _na_kernel,
        out_shape=jax.ShapeDtypeStruct((N_TOK, NA_W), F32),
        grid=(DEC_BATCH, GRID_ROWS),
        in_specs=[qblk, kvblk, kvblk, cblk, cblk,
                  pl.BlockSpec(bias.shape, lambda b, r: (0, 0, 0, 0)),
                  pl.BlockSpec(memory_space=pl.ANY)],
        out_specs=qblk,
        input_output_aliases={6: 0},
        compiler_params=_params(("parallel", "arbitrary")),
        name="na_attention",
    )(q, k, v, cache_k, cache_v, bias, o_all)


N_DC = 2 * NA_KW - 1


def _na_bias_kernel(r_ref, o_ref):
    shape = (GRID_W, NA_KH * GRID_W)
    qc = lax.broadcasted_iota(jnp.int32, shape, 0)
    kc = lax.broadcasted_iota(jnp.int32, shape, 1) & (GRID_W - 1)
    ws = jnp.clip(qc - NA_KW // 2, 0, GRID_W - NA_KW)
    valid = (kc >= ws) & (kc < ws + NA_KW)
    dc = jnp.where(valid, kc - qc + (NA_KW - 1), -1)
    for d0 in range(NA_KH):
        acc = jnp.full(shape, -jnp.inf, F32)
        for d in range(N_DC):
            acc = jnp.where(dc == d, r_ref[d0, d:d + 1, :], acc)
        o_ref[d0] = acc


def _na_bias_table(rpb):
    win = jnp.stack([rpb[:, d0:d0 + NA_KH, :] for d0 in range(NA_KH)], axis=1)
    rexp = jnp.repeat(win.transpose(0, 1, 3, 2), GRID_W, axis=-1)
    rexp = jnp.pad(rexp, ((0, 0), (0, 0), (0, 32 - N_DC), (0, 0)))
    blk = lambda rows: pl.BlockSpec((None, NA_KH, rows, NA_KH * GRID_W), lambda h: (h, 0, 0, 0))
    return pl.pallas_call(
        _na_bias_kernel,
        out_shape=jax.ShapeDtypeStruct((NA_HEADS, NA_KH, GRID_W, NA_KH * GRID_W), F32),
        grid=(NA_HEADS,),
        in_specs=[blk(32)],
        out_specs=blk(GRID_W),
        compiler_params=_params(("parallel",)),
        name="na_bias",
    )(rexp)


def _rope_tables():
    t = np.arange(DEC_SEQ)
    half = GDN_DK // 2
    inv_freq = (np.float32(ROPE_BASE) ** (-np.arange(0, half, 2, dtype=np.float32) / np.float32(half)))
    ang_r = (t // GRID_W).astype(np.float32)[:, None] * inv_freq
    ang_c = (t % GRID_W).astype(np.float32)[:, None] * inv_freq
    cr, sr, cc, sc = np.cos(ang_r), np.sin(ang_r), np.cos(ang_c), np.sin(ang_c)
    z = np.zeros_like(sr)
    cos = np.concatenate([cr, cr, cc, cc], axis=1)
    s_up = np.concatenate([-sr, z, -sc, z], axis=1)
    s_dn = np.concatenate([z, sr, z, sc], axis=1)
    return (jnp.asarray(cos, F32), jnp.asarray(s_up, F32), jnp.asarray(s_dn, F32))


HALO = 8
PREP_CHUNKS = 4
N_STREAM = 2 * GDN_HEADS
LOG_COL = N_STREAM


def _gdn_prep_kernel(axial, seq, *refs):
    if axial:
        (zqkv_ref, zba_ref, cw_ref, alog_ref, dtb_ref, cos_ref, sup_ref, sdn_ref,
         a_ref, qk_ref, r_ref, qd_ref, kd_ref, gl_ref, xs_ref) = refs
    else:
        (zqkv_ref, zba_ref, cw_ref, alog_ref, dtb_ref,
         a_ref, qk_ref, r_ref, qd_ref, kd_ref, gl_ref, xs_ref) = refs
    width = 3 * GDN_W
    part = pl.program_id(1)

    @pl.when(part == 0)
    def _():
        xs_ref[0:HALO, :] = jnp.zeros((HALO, width), F32)
        xs_ref[HALO + seq:HALO + seq + HALO, :] = jnp.zeros((HALO, width), F32)
        xs_ref[HALO:HALO + seq, :] = zqkv_ref[...]

    ri = lax.broadcasted_iota(jnp.int32, (CHUNK, CHUNK), 0)
    ci = lax.broadcasted_iota(jnp.int32, (CHUNK, CHUNK), 1)
    incl = (ri >= ci, ri <= ci)
    strict = (ri > ci, ri < ci)
    lane = lax.broadcasted_iota(jnp.int32, (CHUNK, LANE), 1)

    def body(i, carry):
        r0 = pl.multiple_of((part * PREP_CHUNKS + i) * CHUNK, CHUNK)
        rows = pl.ds(r0, CHUNK)
        win = xs_ref[pl.ds(r0, CHUNK + 2 * HALO), :]
        acc = win[HALO:HALO + CHUNK, :] * cw_ref[CONV_K // 2:CONV_K // 2 + 1, :]
        for j in range(CONV_K):
            if j != CONV_K // 2:
                lo = HALO + j - CONV_K // 2
                acc = acc + win[lo:lo + CHUNK, :] * cw_ref[j:j + 1, :]
        act = _silu(acc)

        zba = zba_ref[rows, :]
        beta_all = 1.0 / (1.0 + jnp.exp(-zba))
        xa = zba + dtb_ref[...]
        softplus = jnp.maximum(xa, 0.0) + jnp.log(1.0 + jnp.exp(-jnp.abs(xa)))
        bg = jnp.where(lane < LOG_COL, beta_all, -jnp.exp(alog_ref[...]) * softplus)

        slabs = [act[:, s * GDN_DK:(s + 1) * GDN_DK] for s in range(2 * GDN_HEADS)]
        ssq = [jnp.sum(x * x, axis=-1, keepdims=True) for x in slabs]
        slabs = [x * lax.rsqrt(s + EPS) for x, s in zip(slabs, ssq)]
        if axial:
            cos, s_up, s_dn = cos_ref[rows, :], sup_ref[rows, :], sdn_ref[rows, :]
            up = [pltpu.roll(x, GDN_DK - GDN_DK // 4, 1) for x in slabs]
            dn = [pltpu.roll(x, GDN_DK // 4, 1) for x in slabs]
            slabs = [x * cos + u * s_up + w * s_dn for x, u, w in zip(slabs, up, dn)]
        qs = [x * (GDN_DK ** -0.5) for x in slabs[:GDN_HEADS]]
        ks = slabs[GDN_HEADS:]
        kbfs = [k.astype(BF16) for k in ks]
        kks = [_dot_nt(kb_, kb_) for kb_ in kbfs]
        qks = [_dot_nt(q.astype(BF16), kb_) for q, kb_ in zip(qs, kbfs)]
        heads = [(qs[hd], ks[hd], act[:, 2 * GDN_W + hd * GDN_DK:2 * GDN_W + (hd + 1) * GDN_DK], kks[hd], qks[hd])
                 for hd in range(GDN_HEADS)]

        gcums = [_dot_exact_lhs(incl[d].astype(BF16), bg) for d in range(2)]
        gcum_ts = [g.T for g in gcums]
        for d in range(2):
            gcum, gcum_t = gcums[d], gcum_ts[d]
            last = gcum[0:1, :] if d else gcum[CHUNK - 1:CHUNK, :]
            e_in = jnp.exp(gcum)
            e_out = jnp.exp(last - gcum)
            gl_ref[i, d:d + 1, :] = jnp.exp(last)
            for hd in range(GDN_HEADS):
                q, k, v, kk, qk = heads[hd]
                cb = d * GDN_HEADS + hd
                cg = LOG_COL + cb
                beta = bg[:, cb:cb + 1]
                eg = e_in[:, cg:cg + 1]
                el = e_out[:, cg:cg + 1]
                diff = gcum[:, cg:cg + 1] - gcum_t[cg:cg + 1, :]
                decay = jnp.where(incl[d], jnp.exp(jnp.where(incl[d], diff, 0.0)), 0.0)
                a_ref[d, i, hd] = jnp.where(strict[d], beta * kk * decay, 0.0)
                qk_ref[d, i, hd] = (qk * decay).astype(BF16)
                kb = k * beta
                r_ref[d, i, hd, :, 0:GDN_DK] = (v * beta).astype(BF16)
                r_ref[d, i, hd, :, GDN_DK:2 * GDN_DK] = (kb * eg).astype(BF16)
                qd_ref[d, i, hd] = (q * eg).astype(BF16)
                kd_ref[d, i, hd] = (k * el).astype(BF16)
        return carry

    lax.fori_loop(0, PREP_CHUNKS, body, 0)


def _gdn_prep(axial, zqkv, zba, conv_w, a_log, dt_bias, rope):
    seq = DEC_SEQ if axial else SEQ
    nseq = DEC_BATCH if axial else BATCH
    nparts = seq // (PREP_CHUNKS * CHUNK)
    nchunk = nseq * seq // CHUNK
    blk0 = N_CTX // DEC_SEQ if axial else 0
    tok = lambda w: pl.BlockSpec((seq, w), lambda b, p: (blk0 + b, 0))
    full = lambda a: pl.BlockSpec(a.shape, lambda b, p: (0,) * a.ndim)
    ins = [zqkv, zba, conv_w, a_log, dt_bias]
    specs = [tok(3 * GDN_W), tok(LANE), full(conv_w), full(a_log), full(dt_bias)]
    if axial:
        ins += list(rope)
        specs += [full(t) for t in rope]
    tile = lambda w: pl.BlockSpec((2, PREP_CHUNKS, GDN_HEADS, CHUNK, w), lambda b, p: (0, b * nparts + p, 0, 0, 0))
    shape = lambda w, dt: jax.ShapeDtypeStruct((2, nchunk, GDN_HEADS, CHUNK, w), dt)
    return pl.pallas_call(
        functools.partial(_gdn_prep_kernel, axial, seq),
        out_shape=[shape(CHUNK, F32), shape(CHUNK, BF16), shape(2 * GDN_DK, BF16),
                   shape(GDN_DK, BF16), shape(GDN_DK, BF16),
                   jax.ShapeDtypeStruct((nchunk, 2, LANE), F32)],
        grid=(nseq, nparts),
        in_specs=specs,
        out_specs=[tile(CHUNK), tile(CHUNK), tile(2 * GDN_DK), tile(GDN_DK), tile(GDN_DK),
                   pl.BlockSpec((PREP_CHUNKS, 2, LANE), lambda b, p: (b * nparts + p, 0, 0))],
        scratch_shapes=[pltpu.VMEM((seq + 2 * HALO, 3 * GDN_W), F32)],
        compiler_params=_params(("parallel", "arbitrary")),
        name="gdn_prep_lat" if axial else "gdn_prep_ctx",
    )(*ins)


TRI_BLK = 8
TRI_ELEMS = CHUNK * CHUNK


def _tri_inverse_kernel(groups_per_dir, a_ref, o_ref, at_ref, tt_ref):
    backward = pl.program_id(0) >= groups_per_dir
    for blk in range(TRI_ELEMS // LANE):
        cols = slice(blk * LANE, (blk + 1) * LANE)
        at_ref[cols, :] = a_ref[:, cols].T
    tt_ref[...] = jnp.zeros(tt_ref.shape, F32)
    cidx = lax.broadcasted_iota(jnp.int32, (CHUNK, LANE), 0)

    def phys(idx):
        return jnp.where(backward, CHUNK - 1 - idx, idx)

    for ib in range(CHUNK // TRI_BLK):
        def row_body(ii, carry, ib=ib):
            i = phys(ib * TRI_BLK + ii)
            acc = (cidx == i).astype(F32)
            for jl in range((ib + 1) * TRI_BLK):
                j = phys(jl)
                arow = at_ref[pl.ds(i * CHUNK + j, 1), :]
                acc = acc - arow * tt_ref[pl.ds(pl.multiple_of(j * CHUNK, CHUNK), CHUNK), :]
            tt_ref[pl.ds(pl.multiple_of(i * CHUNK, CHUNK), CHUNK), :] = acc
            return carry

        lax.fori_loop(0, TRI_BLK, row_body, 0)

    for blk in range(TRI_ELEMS // LANE):
        cols = slice(blk * LANE, (blk + 1) * LANE)
        o_ref[:, cols] = tt_ref[cols, :].T.astype(BF16)


def _tri_inverse(a):
    nprob = a.shape[1] * a.shape[2]
    groups_per_dir = nprob // LANE
    blk = pl.BlockSpec((LANE, TRI_ELEMS), lambda g: (g, 0))
    out = pl.pallas_call(
        functools.partial(_tri_inverse_kernel, groups_per_dir),
        out_shape=jax.ShapeDtypeStruct((2 * nprob, TRI_ELEMS), BF16),
        grid=(2 * groups_per_dir,),
        in_specs=[blk],
        out_specs=blk,
        scratch_shapes=[pltpu.VMEM((TRI_ELEMS, LANE), F32), pltpu.VMEM((TRI_ELEMS, LANE), F32)],
        compiler_params=_params(("parallel",)),
        name="tri_inverse",
    )(a.reshape(2 * nprob, TRI_ELEMS))
    return out.reshape(a.shape)


def _gdn_scan_kernel(nchunk, t_ref, qk_ref, r_ref, qd_ref, kd_ref, gl_ref, s0_ref, *rest):
    o_ref, sfin_ref, state_ref = rest[-3:]
    state_ref[...] = s0_ref[...]
    streams = [(d, hd) for d in range(2) for hd in range(GDN_HEADS)]

    def body(c, carry):
        cc = (c, nchunk - 1 - c)
        uws = [_dot(t_ref[d, cc[d], hd], r_ref[d, cc[d], hd]).astype(BF16) for d, hd in streams]
        mbs = [_dot_tn(kd_ref[d, cc[d], hd], uw) for (d, hd), uw in zip(streams, uws)]
        qos = [_dot(qk_ref[d, cc[d], hd], uw) for (d, hd), uw in zip(streams, uws)]
        sts = [state_ref[d * GDN_HEADS + hd] for d, hd in streams]
        sbs = [s.astype(BF16) for s in sts]
        for (d, hd), qo, sb in zip(streams, qos, sbs):
            rows = pl.ds(pl.multiple_of(cc[d] * CHUNK, CHUNK), CHUNK)
            qp = (qd_ref[d, cc[d], hd].astype(F32) - qo[:, GDN_DK:]).astype(BF16)
            o_ref[d, rows, hd * GDN_DK:(hd + 1) * GDN_DK] = _dot(qp, sb) + qo[:, :GDN_DK]
        for (d, hd), mb, s, sb in zip(streams, mbs, sts, sbs):
            sidx = d * GDN_HEADS + hd
            gl = gl_ref[cc[d]][d:d + 1, LOG_COL + sidx:LOG_COL + sidx + 1]
            state_ref[sidx] = s * gl - _dot(mb[:, GDN_DK:].astype(BF16), sb) + mb[:, :GDN_DK]
        return carry

    lax.fori_loop(0, nchunk, body, 0)
    sfin_ref[...] = state_ref[...]


def _gdn_scan(seq, nseq, tinv, qk, r, qd, kd, gl, s0, o_all=None):
    nc = seq // CHUNK
    blk0 = 0 if o_all is None else N_CTX // seq
    tile = lambda w: pl.BlockSpec((2, nc, GDN_HEADS, CHUNK, w), lambda b: (0, b, 0, 0, 0))
    sblk = pl.BlockSpec((None, N_STREAM, GDN_DK, GDN_DK), lambda b: (b, 0, 0, 0))
    oblk = pl.BlockSpec((2, seq, GDN_W), lambda b: (0, blk0 + b, 0))
    ins = [tinv, qk, r, qd, kd, gl, s0]
    specs = [tile(CHUNK), tile(CHUNK), tile(2 * GDN_DK), tile(GDN_DK), tile(GDN_DK),
             pl.BlockSpec((nc, 2, LANE), lambda b: (b, 0, 0)), sblk]
    aliases = {}
    if o_all is not None:
        ins.append(o_all)
        specs.append(pl.BlockSpec(memory_space=pl.ANY))
        aliases = {len(ins) - 1: 0}
    return pl.pallas_call(
        functools.partial(_gdn_scan_kernel, nc),
        out_shape=[jax.ShapeDtypeStruct((2, N_TOK, GDN_W), F32),
                   jax.ShapeDtypeStruct((nseq, N_STREAM, GDN_DK, GDN_DK), F32)],
        grid=(nseq,),
        in_specs=specs,
        out_specs=[oblk, sblk],
        scratch_shapes=[pltpu.VMEM((N_STREAM, GDN_DK, GDN_DK), F32)],
        input_output_aliases=aliases,
        compiler_params=_params(("parallel",)),
        name="gdn_scan_lat" if seq == DEC_SEQ else "gdn_scan_ctx",
    )(*ins)


def _first_max(vals):
    sel = []
    taken = None
    for a, va in enumerate(vals):
        is_max = None
        for b, vb in enumerate(vals):
            if a == b:
                continue
            c = va >= vb
            is_max = c if is_max is None else (is_max & c)
        if taken is not None:
            is_max = is_max & jnp.logical_not(taken)
        sel.append(is_max)
        taken = is_max if taken is None else (taken | is_max)
    return sel


PAIR_SLOTS = ((0, 1), (0, 2), (0, 3), (1, 3), (1, 2), (3, 2))
N_CLASS = N_GROUPS * len(PAIR_SLOTS)
CLASS_EXPERTS = tuple((g * EXP_PER_GROUP + a, g * EXP_PER_GROUP + b)
                      for g in range(N_GROUPS) for a, b in PAIR_SLOTS)


def _route_rows(lt):
    gl = [lt[a:a + 1, :] for a in range(N_GROUPS)]
    gsel = _first_max(gl)
    gmax = functools.reduce(jnp.maximum, gl)
    gden = functools.reduce(lambda x, y: x + y, [jnp.exp(x - gmax) for x in gl])
    g_w = 1.0 / gden
    el = []
    for kx in range(EXP_PER_GROUP):
        acc = None
        for a in range(N_GROUPS):
            row = N_GROUPS + a * EXP_PER_GROUP + kx
            term = jnp.where(gsel[a], lt[row:row + 1, :], 0.0)
            acc = term if acc is None else acc + term
        el.append(acc)
    emax = functools.reduce(jnp.maximum, el)
    ee = [jnp.exp(x - emax) for x in el]
    eden = functools.reduce(lambda x, y: x + y, ee)
    ep = [x / eden for x in ee]
    top1 = _first_max(ep)
    ep2 = [jnp.where(top1[kx], -1.0, ep[kx]) for kx in range(EXP_PER_GROUP)]
    top2 = _first_max(ep2)
    chosen = [top1[kx] | top2[kx] for kx in range(EXP_PER_GROUP)]
    wsum = functools.reduce(lambda x, y: x + y,
                            [jnp.where(chosen[kx], ep[kx], 0.0) for kx in range(EXP_PER_GROUP)])
    within = [jnp.where(chosen[kx], ep[kx] / wsum, 0.0) for kx in range(EXP_PER_GROUP)]
    cls = jnp.zeros(g_w.shape, jnp.int32)
    w_a = jnp.zeros(g_w.shape, F32)
    w_b = jnp.zeros(g_w.shape, F32)
    for g in range(N_GROUPS):
        for kx, (a, b) in enumerate(PAIR_SLOTS):
            hit = gsel[g] & chosen[a] & chosen[b]
            cls = jnp.where(hit, g * len(PAIR_SLOTS) + kx, cls)
            w_a = jnp.where(hit, g_w * within[a], w_a)
            w_b = jnp.where(hit, g_w * within[b], w_b)
    return cls, w_a, w_b


H2W = D_MODEL + LANE
COL_WA = D_MODEL
COL_WB = D_MODEL + 1


def _outproj_kernel(nx, idx_ref, *refs):
    x = _token_tile(nx, refs)
    (ona_ref, og_ref, zg_ref, mod_ref, wo_ref, gng_ref, n2g_ref, wr_ref, br_ref,
     x1_ref, h2x_ref, cls_ref) = refs[nx:]
    del idx_ref
    m = mod_ref[...]
    og = og_ref[0] + og_ref[1]
    zg = zg_ref[...]
    parts = []
    for hd in range(GDN_HEADS):
        cols = slice(hd * GDN_DK, (hd + 1) * GDN_DK)
        oh = og[:, cols]
        oh = oh * lax.rsqrt(jnp.mean(oh * oh, axis=-1, keepdims=True) + EPS) * gng_ref[...]
        parts.append((oh * _silu(zg[:, cols])).astype(BF16))
    mix = _dot(ona_ref[...].astype(BF16), wo_ref[0:NA_W, :])
    for hd in range(GDN_HEADS):
        r0 = NA_W + hd * GDN_DK
        mix = mix + _dot(parts[hd], wo_ref[r0:r0 + GDN_DK, :])
    x1 = x + m[2:3, :] * mix
    x1_ref[...] = x1
    y = x1 * lax.rsqrt(jnp.mean(x1 * x1, axis=-1, keepdims=True) + EPS) * n2g_ref[...]
    h2 = y * (1.0 + m[4:5, :]) + m[3:4, :]
    h2x_ref[:, 0:D_MODEL] = h2.astype(BF16).astype(F32)
    logits = _dot3(h2, wr_ref[...]) + br_ref[...]
    cls, w_a, w_b = _route_rows(logits.T)
    rid = lax.broadcasted_iota(jnp.int32, (8, TM), 0)
    wt = jnp.where(rid == COL_WA - D_MODEL, w_a, jnp.where(rid == COL_WB - D_MODEL, w_b, 0.0))
    wt = jnp.concatenate([wt, jnp.zeros((LANE - 8, TM), F32)], axis=0)
    h2x_ref[:, D_MODEL:H2W] = wt.T
    cls_ref[...] = jnp.broadcast_to(cls, (8, TM))


def _outproj(tile_mod, x, o_na, o_gdn, zg, mod, w_out, gng, n2g, w_r, b_r):
    tok = lambda w: pl.BlockSpec((TM, w), lambda t, idx: (t, 0))
    full = lambda a: pl.BlockSpec(a.shape, lambda t, idx: (0,) * a.ndim)
    xs = list(x) if isinstance(x, tuple) else [x]
    return pl.pallas_call(
        functools.partial(_outproj_kernel, len(xs)),
        out_shape=[jax.ShapeDtypeStruct((N_TOK, D_MODEL), F32),
                   jax.ShapeDtypeStruct((N_TOK, H2W), F32),
                   jax.ShapeDtypeStruct((N_TOK // TM, 8, TM), jnp.int32)],
        grid_spec=pltpu.PrefetchScalarGridSpec(
            num_scalar_prefetch=1,
            grid=(N_TOK // TM,),
            in_specs=_token_specs(x) + [
                tok(NA_W),
                pl.BlockSpec((2, TM, GDN_W), lambda t, idx: (0, t, 0)), tok(GDN_W),
                pl.BlockSpec((None, 6, D_MODEL), lambda t, idx: (idx[t], 0, 0)),
                full(w_out), full(gng), full(n2g), full(w_r), full(b_r),
            ],
            out_specs=[tok(D_MODEL), tok(H2W), pl.BlockSpec((None, 8, TM), lambda t, idx: (t, 0, 0))],
        ),
        compiler_params=_params(("parallel",)),
        name="outproj",
    )(tile_mod, *xs, o_na, o_gdn, zg, mod, w_out, gng, n2g, w_r, b_r)


TM_X = 256
N_XTILE = N_TOK // TM_X + N_CLASS
N_SLOT = N_XTILE * TM_X
CLS_ROWS = N_TOK // LANE
TAB_EA, TAB_EB, TAB_USED = 0, 1, 2


def _route_pos_kernel(cls_ref, slot_ref, tab_ref):
    cls = cls_ref[...]
    li = lax.broadcasted_iota(jnp.int32, (LANE, LANE), 0)
    lj = lax.broadcasted_iota(jnp.int32, (LANE, LANE), 1)
    before_lane = (li < lj).astype(BF16)
    ri = lax.broadcasted_iota(jnp.int32, (CLS_ROWS, CLS_ROWS), 0)
    rj = lax.broadcasted_iota(jnp.int32, (CLS_ROWS, CLS_ROWS), 1)
    before_row = (rj < ri).astype(BF16)
    tile_start = lax.broadcasted_iota(jnp.int32, (1, LANE), 1) * TM_X
    off = jnp.zeros((1, 1), jnp.int32)
    slot = jnp.zeros(cls.shape, jnp.int32)
    tab_a = jnp.zeros((1, LANE), jnp.int32)
    tab_b = jnp.zeros((1, LANE), jnp.int32)
    for c in range(N_CLASS):
        hit = cls == c
        one = hit.astype(F32)
        in_row = _dot(one.astype(BF16), before_lane)
        row_tot = jnp.broadcast_to(jnp.sum(one, axis=1, keepdims=True), one.shape)
        rank = in_row + _dot(before_row, row_tot.astype(BF16))
        count = jnp.sum(row_tot[:, 0:1], axis=0, keepdims=True).astype(jnp.int32)
        slot = jnp.where(hit, off + rank.astype(jnp.int32), slot)
        nxt = off + (((count + (TM_X - 1)) >> 8) << 8)
        mine = (tile_start >= off) & (tile_start < nxt)
        tab_a = jnp.where(mine, CLASS_EXPERTS[c][0], tab_a)
        tab_b = jnp.where(mine, CLASS_EXPERTS[c][1], tab_b)
        off = nxt
    slot_ref[...] = slot
    row = lax.broadcasted_iota(jnp.int32, (8, LANE), 0)
    used = jnp.broadcast_to(off >> 8, (8, LANE))
    tab_ref[...] = jnp.where(row == TAB_EA, tab_a, jnp.where(row == TAB_EB, tab_b, used))


def _route_positions(cls):
    assert TM_X == 256
    return pl.pallas_call(
        _route_pos_kernel,
        out_shape=[jax.ShapeDtypeStruct((CLS_ROWS, LANE), jnp.int32),
                   jax.ShapeDtypeStruct((8, LANE), jnp.int32)],
        name="route_positions",
    )(cls)


def _row_copy(src, src_row, dst, dst_row, sem):
    return pltpu.make_async_copy(src.at[pl.ds(src_row, 1), :], dst.at[pl.ds(dst_row, 1), :], sem)


ROW_UNROLL = 8


def _permute_kernel(slot_ref, h_ref, xs_in_ref, xs_ref, buf_ref, sem_ref):
    del xs_in_ref
    t = pl.program_id(0)
    cur = t % 2
    buf_ref[cur] = h_ref[...]

    def issue(i, carry):
        _row_copy(buf_ref.at[cur], i, xs_ref, slot_ref[t * TM + i], sem_ref.at[cur]).start()
        return carry

    lax.fori_loop(0, TM, issue, 0, unroll=ROW_UNROLL)

    def drain(which):
        def wait(i, carry):
            _row_copy(buf_ref.at[which], 0, xs_ref, 0, sem_ref.at[which]).wait()
            return carry
        lax.fori_loop(0, TM, wait, 0, unroll=ROW_UNROLL)

    @pl.when(t > 0)
    def _():
        drain(1 - cur)

    @pl.when(t == pl.num_programs(0) - 1)
    def _():
        drain(cur)


def _permute(slot, h2x):
    xs0 = jnp.zeros((N_SLOT, H2W), F32)
    return pl.pallas_call(
        _permute_kernel,
        out_shape=jax.ShapeDtypeStruct((N_SLOT, H2W), F32),
        grid_spec=pltpu.PrefetchScalarGridSpec(
            num_scalar_prefetch=1,
            grid=(N_TOK // TM,),
            in_specs=[pl.BlockSpec((TM, H2W), lambda t, s: (t, 0)), pl.BlockSpec(memory_space=pl.ANY)],
            out_specs=pl.BlockSpec(memory_space=pl.ANY),
            scratch_shapes=[pltpu.VMEM((2, TM, H2W), F32), pltpu.SemaphoreType.DMA((2,))],
        ),
        input_output_aliases={2: 0},
        compiler_params=_params(("arbitrary",)),
        name="moe_permute",
    )(slot, h2x, xs0)


def _experts_kernel(tab_ref, xs_ref, w1a_ref, w3a_ref, w2a_ref, w1b_ref, w3b_ref, w2b_ref, ys_ref,
                    c1a_ref, c3a_ref, c2a_ref, c1b_ref, c3b_ref, c2b_ref):
    t = pl.program_id(0)
    prev = jnp.maximum(t - 1, 0)

    @pl.when(t < tab_ref[TAB_USED, 0])
    def _():
        @pl.when((t == 0) | (tab_ref[TAB_EA, t] != tab_ref[TAB_EA, prev]))
        def _():
            c1a_ref[...] = w1a_ref[...].astype(BF16)
            c3a_ref[...] = w3a_ref[...].astype(BF16)
            c2a_ref[...] = w2a_ref[...].astype(BF16)

        @pl.when((t == 0) | (tab_ref[TAB_EB, t] != tab_ref[TAB_EB, prev]))
        def _():
            c1b_ref[...] = w1b_ref[...].astype(BF16)
            c3b_ref[...] = w3b_ref[...].astype(BF16)
            c2b_ref[...] = w2b_ref[...].astype(BF16)

        x = xs_ref[:, 0:D_MODEL].astype(BF16)
        gates = xs_ref[:, D_MODEL:H2W]
        w_a = gates[:, COL_WA - D_MODEL:COL_WA - D_MODEL + 1]
        w_b = gates[:, COL_WB - D_MODEL:COL_WB - D_MODEL + 1]
        a1 = _dot(x, c1a_ref[...])
        a3 = _dot(x, c3a_ref[...])
        b1 = _dot(x, c1b_ref[...])
        b3 = _dot(x, c3b_ref[...])
        sa = (_silu(a1) * a3 * w_a).astype(BF16)
        sb = (_silu(b1) * b3 * w_b).astype(BF16)
        ys_ref[...] = _dot(sa, c2a_ref[...]) + _dot(sb, c2b_ref[...])


def _experts(layer, tab, xs, w1, w3, w2):
    last = lambda t, tab: jnp.minimum(t, tab[TAB_USED, 0] - 1)
    tile = lambda w: pl.BlockSpec((TM_X, w), lambda t, tab: (last(t, tab), 0))
    wspec = lambda shape, row: pl.BlockSpec((None, None) + shape,
                                            lambda t, tab: (layer, tab[row, last(t, tab)], 0, 0))
    up, down = (D_MODEL, EXPERT_FF), (EXPERT_FF, D_MODEL)
    return pl.pallas_call(
        _experts_kernel,
        out_shape=jax.ShapeDtypeStruct((N_SLOT, D_MODEL), F32),
        grid_spec=pltpu.PrefetchScalarGridSpec(
            num_scalar_prefetch=1,
            grid=(N_XTILE,),
            in_specs=[tile(H2W),
                      wspec(up, TAB_EA), wspec(up, TAB_EA), wspec(down, TAB_EA),
                      wspec(up, TAB_EB), wspec(up, TAB_EB), wspec(down, TAB_EB)],
            out_specs=tile(D_MODEL),
            scratch_shapes=[pltpu.VMEM(up, BF16), pltpu.VMEM(up, BF16), pltpu.VMEM(down, BF16),
                            pltpu.VMEM(up, BF16), pltpu.VMEM(up, BF16), pltpu.VMEM(down, BF16)],
        ),
        compiler_params=_params(("arbitrary",)),
        name="moe_experts",
    )(tab, xs, w1, w3, w2, w1, w3, w2)


def _unpermute_kernel(idx_ref, slot_ref, x1_ref, mod_ref, ys_ref, *rest):
    outs, (buf_ref, sem_ref) = rest[:-2], rest[-2:]
    del idx_ref
    t = pl.program_id(0)
    cur = t % 2

    def fetch(tile, which):
        def issue(i, carry):
            _row_copy(ys_ref, slot_ref[tile * TM + i], buf_ref.at[which], i, sem_ref.at[which]).start()
            return carry
        lax.fori_loop(0, TM, issue, 0, unroll=ROW_UNROLL)

    @pl.when(t == 0)
    def _():
        fetch(0, 0)

    @pl.when(t + 1 < pl.num_programs(0))
    def _():
        fetch(t + 1, 1 - cur)

    def wait(i, carry):
        _row_copy(ys_ref, 0, buf_ref.at[cur], 0, sem_ref.at[cur]).wait()
        return carry

    lax.fori_loop(0, TM, wait, 0, unroll=ROW_UNROLL)
    y = x1_ref[...] + mod_ref[5:6, :] * buf_ref[cur]
    if len(outs) == 1:
        outs[0][...] = y
    else:
        @pl.when(t < N_CTX // TM)
        def _():
            outs[0][...] = y

        @pl.when(t >= N_CTX // TM)
        def _():
            outs[1][...] = y


def _unpermute(tile_mod, slot, x1, mod, ys, split):
    tok = pl.BlockSpec((TM, D_MODEL), lambda t, idx, s: (t, 0))
    if split:
        out_shape = [jax.ShapeDtypeStruct((N_CTX, D_MODEL), F32), jax.ShapeDtypeStruct((N_LAT, D_MODEL), F32)]
        out_specs = _token_specs((None, None))
    else:
        out_shape, out_specs = jax.ShapeDtypeStruct((N_TOK, D_MODEL), F32), tok
    return pl.pallas_call(
        _unpermute_kernel,
        out_shape=out_shape,
        grid_spec=pltpu.PrefetchScalarGridSpec(
            num_scalar_prefetch=2,
            grid=(N_TOK // TM,),
            in_specs=[tok, pl.BlockSpec((None, 6, D_MODEL), lambda t, idx, s: (idx[t], 0, 0)),
                      pl.BlockSpec(memory_space=pl.ANY)],
            out_specs=out_specs,
            scratch_shapes=[pltpu.VMEM((2, TM, D_MODEL), F32), pltpu.SemaphoreType.DMA((2,))],
        ),
        compiler_params=_params(("arbitrary",)),
        name="moe_unpermute",
    )(tile_mod, slot, x1, mod, ys)


def _tile_mod_index(tile):
    t = np.arange(N_TOK // tile) * tile
    return jnp.asarray(np.where(t < N_CTX, 0, 1 + (t - N_CTX) // DEC_SEQ), jnp.int32)


def kernel(x_prompt, x_sample, c, cache_k, cache_v, state_ssm, c_ctx, ada_w, ada_b, norm1_g, norm2_g, w_in, w_out, na_qn_g, na_kn_g, na_rpb, gdn_conv_w, gdn_a_log, gdn_dt_bias, gdn_norm_g, moe_w_rg, moe_b_rg, moe_w_re, moe_b_re, moe_w1, moe_w3, moe_w2):
    x = (x_prompt.reshape(N_CTX, D_MODEL), x_sample.reshape(N_LAT, D_MODEL))
    cv = jnp.concatenate([c_ctx[None, :], c, jnp.zeros((N_MOD_PAD - N_MOD, D_MODEL), F32)], axis=0)
    mod_all = _modulation(cv, ada_w, ada_b).reshape(DEPTH, N_MOD_PAD, 6, D_MODEL)

    idx_tm = _tile_mod_index(TM)
    hh = np.arange(NA_W) // NA_HD
    ones_bd = jnp.asarray(hh[:, None] == hh[None, :], BF16)
    rope = _rope_tables()
    cache_k4 = cache_k.reshape(DEC_BATCH, DEPTH, PAST_LEN, NA_W)
    cache_v4 = cache_v.reshape(DEC_BATCH, DEPTH, PAST_LEN, NA_W)
    lane_pad = lambda a, at: jnp.zeros((1, LANE), F32).at[0, at:at + a.size].set(a.reshape(-1))
    zeros_state = jnp.zeros((BATCH, 2 * GDN_HEADS, GDN_DK, GDN_DK), F32)

    caches, ss = None, []
    for l in range(DEPTH):
        mod = mod_all[l]
        w_main = w_in[l, :, :N_MAIN].astype(BF16)
        w_ba = jnp.zeros((D_MODEL, LANE), BF16).at[:, :N_IN - N_MAIN].set(w_in[l, :, N_MAIN:].astype(BF16))
        qg = jnp.tile(na_qn_g[l], NA_HEADS)[None, :]
        kg = jnp.tile(na_kn_g[l], NA_HEADS)[None, :]
        q, k, v, zqkv, zg, zba, *caches = _inproj(l, idx_tm, x, mod, norm1_g[l][None, :], w_main, w_ba,
                                                  ones_bd, qg, kg, caches)

        bias = _na_bias_table(na_rpb[l])
        o_na = _na_attention(l, q, k, v, cache_k4, cache_v4, bias, _ctx_attention(q, k, v))

        conv_w = jnp.zeros((8, 3 * GDN_W), F32).at[:CONV_K].set(gdn_conv_w[l])
        a_row = lane_pad(gdn_a_log[l], 2 * GDN_HEADS)
        dt_row = lane_pad(gdn_dt_bias[l], 2 * GDN_HEADS)
        a_c, *ops_c = _gdn_prep(False, zqkv, zba, conv_w, a_row, dt_row, None)
        a_l, *ops_l = _gdn_prep(True, zqkv, zba, conv_w, a_row, dt_row, rope)
        og_c, s_ctx = _gdn_scan(SEQ, BATCH, _tri_inverse(a_c), *ops_c, zeros_state)
        s0_lat = state_ssm[:, l].reshape(DEC_BATCH, N_STREAM, GDN_DK, GDN_DK)
        o_gdn, _ = _gdn_scan(DEC_SEQ, DEC_BATCH, _tri_inverse(a_l), *ops_l, s0_lat, og_c)

        w_r = jnp.zeros((D_MODEL, LANE), F32).at[:, :N_GROUPS].set(moe_w_rg[l])
        w_r = w_r.at[:, N_GROUPS:N_GROUPS + N_EXPERTS].set(moe_w_re[l])
        b_r = lane_pad(jnp.concatenate([moe_b_rg[l], moe_b_re[l]]), 0)
        x1, h2x, cls = _outproj(idx_tm, x, o_na, o_gdn, zg, mod, w_out[l].astype(BF16),
                                gdn_norm_g[l][None, :], norm2_g[l][None, :], w_r, b_r)
        slot, tab = _route_positions(cls[:, 0, :].reshape(CLS_ROWS, LANE))
        slot = slot.reshape(N_TOK)
        ys = _experts(l, tab, _permute(slot, h2x), moe_w1, moe_w3, moe_w2)
        x = _unpermute(idx_tm, slot, x1, mod, ys, split=l == DEPTH - 1)

        ss.append(s_ctx.reshape(BATCH, 2, GDN_HEADS, GDN_DK, GDN_DK))

    y_prompt = x[0].reshape(BATCH, SEQ, D_MODEL)
    y_sample = x[1].reshape(DEC_BATCH, DEC_SEQ, D_MODEL)
    new_k, new_v = (a.reshape(BATCH, DEPTH, SEQ, NA_HEADS, NA_HD) for a in caches)
    return (y_prompt, y_sample, new_k, new_v, jnp.stack(ss, axis=1))
```

```python
import functools

import jax
import jax.numpy as jnp
import numpy as np
from jax import lax
from jax.experimental import pallas as pl
from jax.experimental.pallas import tpu as pltpu

F32 = jnp.float32
BF16 = jnp.bfloat16

D_MODEL = 1024
BATCH = 16
SEQ = 256
DEPTH = 2
DEC_BATCH = 8
DEC_SEQ = 1024
PAST_LEN = 512
GRID_W = 64
GRID_ROWS = DEC_SEQ // GRID_W
NA_HEADS = 8
NA_HD = 64
NA_W = NA_HEADS * NA_HD
NA_KH = 8
NA_KW = 16
GDN_HEADS = 4
GDN_DK = 128
GDN_W = GDN_HEADS * GDN_DK
CONV_K = 5
CHUNK = 64
ROPE_BASE = 10000.0
N_GROUPS = 4
EXP_PER_GROUP = 4
N_EXPERTS = 16
EXPERT_FF = 512
EPS = 1e-6
N_IN = 3 * NA_W + 4 * GDN_W + 4 * GDN_HEADS

N_CTX = BATCH * SEQ
N_LAT = DEC_BATCH * DEC_SEQ
N_TOK = N_CTX + N_LAT
N_MOD = 1 + DEC_BATCH
N_MOD_PAD = 16
LANE = 128
N_MAIN = 3 * NA_W + 4 * GDN_W
TM = 256
VMEM_LIMIT = 56 * 1024 * 1024


def _dot(a, b):
    return jnp.dot(a, b, preferred_element_type=F32)


def _dot_nt(a, b):
    return lax.dot_general(a, b, (((1,), (1,)), ((), ())), preferred_element_type=F32)


def _dot_tn(a, b):
    return lax.dot_general(a, b, (((0,), (0,)), ((), ())), preferred_element_type=F32)


def _split2(x):
    hi = x.astype(BF16)
    lo = (x - hi.astype(F32)).astype(BF16)
    return hi, lo


def _split3(x):
    hi = x.astype(BF16)
    r = x - hi.astype(F32)
    mid = r.astype(BF16)
    lo = (r - mid.astype(F32)).astype(BF16)
    return hi, mid, lo


def _dot3(a, b):
    ah, al = _split2(a)
    bh, bl = _split2(b)
    return _dot(ah, bh) + (_dot(ah, bl) + _dot(al, bh))


def _dot_exact_lhs(mask_bf16, x):
    hi, mid, lo = _split3(x)
    return _dot(mask_bf16, hi) + (_dot(mask_bf16, mid) + _dot(mask_bf16, lo))


def _silu(x):
    return x * (1.0 / (1.0 + jnp.exp(-x)))


def _params(sem):
    return pltpu.CompilerParams(dimension_semantics=sem, vmem_limit_bytes=VMEM_LIMIT)


def _mod_kernel(cv_ref, w_ref, b_ref, o_ref):
    cv = cv_ref[...]
    s = _silu(cv)
    o_ref[...] = jnp.dot(s, w_ref[...], preferred_element_type=F32,
                         precision=lax.Precision.HIGHEST) + b_ref[...]


def _modulation(cv, ada_w, ada_b):
    nblk = 512
    return pl.pallas_call(
        _mod_kernel,
        out_shape=jax.ShapeDtypeStruct((DEPTH, N_MOD_PAD, 6 * D_MODEL), F32),
        grid=(DEPTH, 6 * D_MODEL // nblk),
        in_specs=[
            pl.BlockSpec((N_MOD_PAD, D_MODEL), lambda l, j: (0, 0)),
            pl.BlockSpec((None, D_MODEL, nblk), lambda l, j: (l, 0, j)),
            pl.BlockSpec((None, 1, nblk), lambda l, j: (l, 0, j)),
        ],
        out_specs=pl.BlockSpec((None, N_MOD_PAD, nblk), lambda l, j: (l, 0, j)),
        compiler_params=_params(("parallel", "parallel")),
        name="modulation",
    )(cv, ada_w, ada_b.reshape(DEPTH, 1, 6 * D_MODEL))


def _token_specs(x):
    if not isinstance(x, tuple):
        return [pl.BlockSpec((TM, D_MODEL), lambda t, *_: (t, 0))]
    return [pl.BlockSpec((TM, D_MODEL), lambda t, *_: (jnp.minimum(t, N_CTX // TM - 1), 0)),
            pl.BlockSpec((TM, D_MODEL), lambda t, *_: (jnp.maximum(t - N_CTX // TM, 0), 0))]


def _token_tile(nx, refs):
    if nx == 1:
        return refs[0][...]
    return jnp.where(pl.program_id(0) < N_CTX // TM, refs[0][...], refs[1][...])


def _inproj_kernel(nx, idx_ref, *refs):
    x = _token_tile(nx, refs)
    mod_ref, n1g_ref, win_ref, wba_ref, ones_ref, qg_ref, kg_ref = refs[nx:nx + 7]
    q_ref, k_ref, v_ref, zqkv_ref, zg_ref, zba_ref, kcache_ref, vcache_ref = refs[-8:]
    del idx_ref
    y = x * lax.rsqrt(jnp.mean(x * x, axis=-1, keepdims=True) + EPS) * n1g_ref[...]
    m = mod_ref[...]
    h = (y * (1.0 + m[1:2, :]) + m[0:1, :]).astype(BF16)
    z = _dot(h, win_ref[...])
    zba_ref[...] = _dot(h, wba_ref[...])
    ones = ones_ref[...]

    def head_rms(zz, g):
        hi, lo = _split2(zz * zz)
        ms = (_dot(hi, ones) + _dot(lo, ones)) * (1.0 / NA_HD)
        return zz * lax.rsqrt(ms + EPS) * g

    q_ref[...] = head_rms(z[:, 0:NA_W], qg_ref[...]) * (NA_HD ** -0.5)
    k = head_rms(z[:, NA_W:2 * NA_W], kg_ref[...])
    v = z[:, 2 * NA_W:3 * NA_W]
    k_ref[...] = k
    v_ref[...] = v
    zqkv_ref[...] = z[:, 3 * NA_W:3 * NA_W + 3 * GDN_W]
    zg_ref[...] = z[:, 3 * NA_W + 3 * GDN_W:N_MAIN]

    @pl.when(pl.program_id(0) < BATCH)
    def _():
        kcache_ref[...] = k
        vcache_ref[...] = v


def _inproj(layer, tile_mod, x, mod, n1g, w_main, w_ba, ones_bd, qg, kg, caches):
    assert TM == SEQ
    tok = lambda w: pl.BlockSpec((TM, w), lambda t, idx: (t, 0))
    full = lambda a: pl.BlockSpec(a.shape, lambda t, idx: (0,) * a.ndim)
    out_w = (NA_W, NA_W, NA_W, 3 * GDN_W, GDN_W, LANE)
    cache_blk = pl.BlockSpec((None, None, SEQ, NA_W), lambda t, idx: (jnp.minimum(t, BATCH - 1), layer, 0, 0))
    cache_shape = jax.ShapeDtypeStruct((BATCH, DEPTH, SEQ, NA_W), F32)
    xs = list(x) if isinstance(x, tuple) else [x]
    ins = [tile_mod, *xs, mod, n1g, w_main, w_ba, ones_bd, qg, kg]
    specs = _token_specs(x) + [pl.BlockSpec((None, 6, D_MODEL), lambda t, idx: (idx[t], 0, 0)),
                               full(n1g), full(w_main), full(w_ba), full(ones_bd), full(qg), full(kg)]
    aliases = {}
    if caches is not None:
        aliases = {len(ins): len(out_w), len(ins) + 1: len(out_w) + 1}
        ins += list(caches)
        specs += [pl.BlockSpec(memory_space=pl.ANY)] * 2
    return pl.pallas_call(
        functools.partial(_inproj_kernel, len(xs)),
        out_shape=[jax.ShapeDtypeStruct((N_TOK, w), F32) for w in out_w] + [cache_shape, cache_shape],
        grid_spec=pltpu.PrefetchScalarGridSpec(
            num_scalar_prefetch=1,
            grid=(N_TOK // TM,),
            in_specs=specs,
            out_specs=[tok(w) for w in out_w] + [cache_blk, cache_blk],
        ),
        input_output_aliases=aliases,
        compiler_params=_params(("arbitrary",)),
        name="inproj",
    )(*ins)


def _pair_masks(shape):
    lane = lax.broadcasted_iota(jnp.int32, shape, 1)
    return lane < NA_HD


def _ctx_attn_kernel(q_ref, k_ref, v_ref, o_ref):
    for p in range(NA_HEADS // 2):
        cols = slice(p * LANE, (p + 1) * LANE)
        q = q_ref[:, cols]
        kb = k_ref[:, cols].astype(BF16)
        vb = v_ref[:, cols].astype(BF16)
        first = _pair_masks(q.shape)
        outs = []
        for hm in (first, jnp.logical_not(first)):
            s = _dot_nt(jnp.where(hm, q, 0.0).astype(BF16), kb)
            e = jnp.exp(s - jnp.max(s, axis=-1, keepdims=True))
            o = _dot(e.astype(BF16), vb)
            outs.append(o / jnp.sum(e, axis=-1, keepdims=True))
        o_ref[:, cols] = jnp.where(first, outs[0], outs[1])


def _ctx_attention(q, k, v):
    blk = pl.BlockSpec((SEQ, NA_W), lambda b: (b, 0))
    return pl.pallas_call(
        _ctx_attn_kernel,
        out_shape=jax.ShapeDtypeStruct((N_TOK, NA_W), F32),
        grid=(BATCH,),
        in_specs=[blk, blk, blk],
        out_specs=blk,
        compiler_params=_params(("parallel",)),
        name="ctx_attention",
    )(q, k, v)


NA_PAIR_GROUP = 4


def _na_kernel(q_ref, k_ref, v_ref, kc_ref, vc_ref, bias_ref, o_all_ref, o_ref):
    del o_all_ref
    r = pl.program_id(1)
    base = jnp.clip(r - NA_KH // 2, 0, GRID_ROWS - NA_KH)
    dr0 = base - r + (NA_KH - 1)
    row0 = pl.multiple_of(base * GRID_W, GRID_W)
    nwin = NA_KH * GRID_W
    first = _pair_masks((GRID_W, LANE))
    col = lambda p: slice(p * LANE, (p + 1) * LANE)
    for g0 in range(0, NA_HEADS // 2, NA_PAIR_GROUP):
        pairs = range(g0, g0 + NA_PAIR_GROUP)
        heads = range(2 * g0, 2 * (g0 + NA_PAIR_GROUP))
        qh = {h: jnp.where(first if h % 2 == 0 else jnp.logical_not(first), q_ref[:, col(h // 2)], 0.0).astype(BF16)
              for h in heads}
        kw = {p: k_ref[pl.ds(row0, nwin), col(p)].astype(BF16) for p in pairs}
        kc = {p: kc_ref[:, col(p)].astype(BF16) for p in pairs}
        vw = {p: v_ref[pl.ds(row0, nwin), col(p)].astype(BF16) for p in pairs}
        vc = {p: vc_ref[:, col(p)].astype(BF16) for p in pairs}
        s_loc = {h: _dot_nt(qh[h], kw[h // 2]) + bias_ref[h, dr0] for h in heads}
        s_ctx = {h: _dot_nt(qh[h], kc[h // 2]) for h in heads}
        mx = {h: jnp.maximum(jnp.max(s_loc[h], axis=-1, keepdims=True), jnp.max(s_ctx[h], axis=-1, keepdims=True))
              for h in heads}
        e_loc = {h: jnp.exp(s_loc[h] - mx[h]) for h in heads}
        e_ctx = {h: jnp.exp(s_ctx[h] - mx[h]) for h in heads}
        den = {h: jnp.sum(e_loc[h], axis=-1, keepdims=True) + jnp.sum(e_ctx[h], axis=-1, keepdims=True)
               for h in heads}
        o = {h: _dot(e_loc[h].astype(BF16), vw[h // 2]) + _dot(e_ctx[h].astype(BF16), vc[h // 2]) for h in heads}
        for p in pairs:
            o_ref[:, col(p)] = jnp.where(first, o[2 * p] / den[2 * p], o[2 * p + 1] / den[2 * p + 1])


def _na_attention(layer, q, k, v, cache_k, cache_v, bias, o_all):
    lat0 = N_CTX // DEC_SEQ
    qblk = pl.BlockSpec((GRID_W, NA_W), lambda b, r: (N_CTX // GRID_W + b * GRID_ROWS + r, 0))
    kvblk = pl.BlockSpec((DEC_SEQ, NA_W), lambda b, r: (lat0 + b, 0))
    cblk = pl.BlockSpec((None, None, PAST_LEN, NA_W), lambda b, r: (b, layer, 0, 0))
    return pl.pallas_call(
        _na_kernel,
        out_shape=jax.ShapeDtypeStruct((N_TOK, NA_W), F32),
        grid=(DEC_BATCH, GRID_ROWS),
        in_specs=[qblk, kvblk, kvblk, cblk, cblk,
                  pl.BlockSpec(bias.shape, lambda b, r: (0, 0, 0, 0)),
                  pl.BlockSpec(memory_space=pl.ANY)],
        out_specs=qblk,
        input_output_aliases={6: 0},
        compiler_params=_params(("parallel", "arbitrary")),
        name="na_attention",
    )(q, k, v, cache_k, cache_v, bias, o_all)


N_DC = 2 * NA_KW - 1


def _na_bias_kernel(r_ref, o_ref):
    shape = (GRID_W, NA_KH * GRID_W)
    qc = lax.broadcasted_iota(jnp.int32, shape, 0)
    kc = lax.broadcasted_iota(jnp.int32, shape, 1) & (GRID_W - 1)
    ws = jnp.clip(qc - NA_KW // 2, 0, GRID_W - NA_KW)
    valid = (kc >= ws) & (kc < ws + NA_KW)
    dc = jnp.where(valid, kc - qc + (NA_KW - 1), -1)
    for d0 in range(NA_KH):
        acc = jnp.full(shape, -jnp.inf, F32)
        for d in range(N_DC):
            acc = jnp.where(dc == d, r_ref[d0, d:d + 1, :], acc)
        o_ref[d0] = acc


def _na_bias_table(rpb):
    win = jnp.stack([rpb[:, d0:d0 + NA_KH, :] for d0 in range(NA_KH)], axis=1)
    rexp = jnp.repeat(win.transpose(0, 1, 3, 2), GRID_W, axis=-1)
    rexp = jnp.pad(rexp, ((0, 0), (0, 0), (0, 32 - N_DC), (0, 0)))
    blk = lambda rows: pl.BlockSpec((None, NA_KH, rows, NA_KH * GRID_W), lambda h: (h, 0, 0, 0))
    return pl.pallas_call(
        _na_bias_kernel,
        out_shape=jax.ShapeDtypeStruct((NA_HEADS, NA_KH, GRID_W, NA_KH * GRID_W), F32),
        grid=(NA_HEADS,),
        in_specs=[blk(32)],
        out_specs=blk(GRID_W),
        compiler_params=_params(("parallel",)),
        name="na_bias",
    )(rexp)


def _rope_tables():
    t = np.arange(DEC_SEQ)
    half = GDN_DK // 2
    inv_freq = (np.float32(ROPE_BASE) ** (-np.arange(0, half, 2, dtype=np.float32) / np.float32(half)))
    ang_r = (t // GRID_W).astype(np.float32)[:, None] * inv_freq
    ang_c = (t % GRID_W).astype(np.float32)[:, None] * inv_freq
    cr, sr, cc, sc = np.cos(ang_r), np.sin(ang_r), np.cos(ang_c), np.sin(ang_c)
    z = np.zeros_like(sr)
    cos = np.concatenate([cr, cr, cc, cc], axis=1)
    s_up = np.concatenate([-sr, z, -sc, z], axis=1)
    s_dn = np.concatenate([z, sr, z, sc], axis=1)
    return (jnp.asarray(cos, F32), jnp.asarray(s_up, F32), jnp.asarray(s_dn, F32))


HALO = 8
PREP_CHUNKS = 4
N_STREAM = 2 * GDN_HEADS
LOG_COL = N_STREAM


def _gdn_prep_kernel(axial, seq, *refs):
    if axial:
        (zqkv_ref, zba_ref, cw_ref, alog_ref, dtb_ref, cos_ref, sup_ref, sdn_ref,
         a_ref, qk_ref, r_ref, qd_ref, kd_ref, gl_ref, xs_ref) = refs
    else:
        (zqkv_ref, zba_ref, cw_ref, alog_ref, dtb_ref,
         a_ref, qk_ref, r_ref, qd_ref, kd_ref, gl_ref, xs_ref) = refs
    width = 3 * GDN_W
    part = pl.program_id(1)

    @pl.when(part == 0)
    def _():
        xs_ref[0:HALO, :] = jnp.zeros((HALO, width), F32)
        xs_ref[HALO + seq:HALO + seq + HALO, :] = jnp.zeros((HALO, width), F32)
        xs_ref[HALO:HALO + seq, :] = zqkv_ref[...]

    ri = lax.broadcasted_iota(jnp.int32, (CHUNK, CHUNK), 0)
    ci = lax.broadcasted_iota(jnp.int32, (CHUNK, CHUNK), 1)
    incl = (ri >= ci, ri <= ci)
    strict = (ri > ci, ri < ci)
    lane = lax.broadcasted_iota(jnp.int32, (CHUNK, LANE), 1)

    def body(i, carry):
        r0 = pl.multiple_of((part * PREP_CHUNKS + i) * CHUNK, CHUNK)
        rows = pl.ds(r0, CHUNK)
        win = xs_ref[pl.ds(r0, CHUNK + 2 * HALO), :]
        acc = win[HALO:HALO + CHUNK, :] * cw_ref[CONV_K // 2:CONV_K // 2 + 1, :]
        for j in range(CONV_K):
            if j != CONV_K // 2:
                lo = HALO + j - CONV_K // 2
                acc = acc + win[lo:lo + CHUNK, :] * cw_ref[j:j + 1, :]
        act = _silu(acc)

        zba = zba_ref[rows, :]
        beta_all = 1.0 / (1.0 + jnp.exp(-zba))
        xa = zba + dtb_ref[...]
        softplus = jnp.maximum(xa, 0.0) + jnp.log(1.0 + jnp.exp(-jnp.abs(xa)))
        bg = jnp.where(lane < LOG_COL, beta_all, -jnp.exp(alog_ref[...]) * softplus)

        slabs = [act[:, s * GDN_DK:(s + 1) * GDN_DK] for s in range(2 * GDN_HEADS)]
        ssq = [jnp.sum(x * x, axis=-1, keepdims=True) for x in slabs]
        slabs = [x * lax.rsqrt(s + EPS) for x, s in zip(slabs, ssq)]
        if axial:
            cos, s_up, s_dn = cos_ref[rows, :], sup_ref[rows, :], sdn_ref[rows, :]
            up = [pltpu.roll(x, GDN_DK - GDN_DK // 4, 1) for x in slabs]
            dn = [pltpu.roll(x, GDN_DK // 4, 1) for x in slabs]
            slabs = [x * cos + u * s_up + w * s_dn for x, u, w in zip(slabs, up, dn)]
        qs = [x * (GDN_DK ** -0.5) for x in slabs[:GDN_HEADS]]
        ks = slabs[GDN_HEADS:]
        kbfs = [k.astype(BF16) for k in ks]
        kks = [_dot_nt(kb_, kb_) for kb_ in kbfs]
        qks = [_dot_nt(q.astype(BF16), kb_) for q, kb_ in zip(qs, kbfs)]
        heads = [(qs[hd], ks[hd], act[:, 2 * GDN_W + hd * GDN_DK:2 * GDN_W + (hd + 1) * GDN_DK], kks[hd], qks[hd])
                 for hd in range(GDN_HEADS)]

        gcums = [_dot_exact_lhs(incl[d].astype(BF16), bg) for d in range(2)]
        gcum_ts = [g.T for g in gcums]
        for d in range(2):
            gcum, gcum_t = gcums[d], gcum_ts[d]
            last = gcum[0:1, :] if d else gcum[CHUNK - 1:CHUNK, :]
            e_in = jnp.exp(gcum)
            e_out = jnp.exp(last - gcum)
            gl_ref[i, d:d + 1, :] = jnp.exp(last)
            for hd in range(GDN_HEADS):
                q, k, v, kk, qk = heads[hd]
                cb = d * GDN_HEADS + hd
                cg = LOG_COL + cb
                beta = bg[:, cb:cb + 1]
                eg = e_in[:, cg:cg + 1]
                el = e_out[:, cg:cg + 1]
                diff = gcum[:, cg:cg + 1] - gcum_t[cg:cg + 1, :]
                decay = jnp.where(incl[d], jnp.exp(jnp.where(incl[d], diff, 0.0)), 0.0)
                a_ref[d, i, hd] = jnp.where(strict[d], beta * kk * decay, 0.0)
                qk_ref[d, i, hd] = (qk * decay).astype(BF16)
                kb = k * beta
                r_ref[d, i, hd, :, 0:GDN_DK] = (v * beta).astype(BF16)
                r_ref[d, i, hd, :, GDN_DK:2 * GDN_DK] = (kb * eg).astype(BF16)
                qd_ref[d, i, hd] = (q * eg).astype(BF16)
                kd_ref[d, i, hd] = (k * el).astype(BF16)
        return carry

    lax.fori_loop(0, PREP_CHUNKS, body, 0)


def _gdn_prep(axial, zqkv, zba, conv_w, a_log, dt_bias, rope):
    seq = DEC_SEQ if axial else SEQ
    nseq = DEC_BATCH if axial else BATCH
    nparts = seq // (PREP_CHUNKS * CHUNK)
    nchunk = nseq * seq // CHUNK
    blk0 = N_CTX // DEC_SEQ if axial else 0
    tok = lambda w: pl.BlockSpec((seq, w), lambda b, p: (blk0 + b, 0))
    full = lambda a: pl.BlockSpec(a.shape, lambda b, p: (0,) * a.ndim)
    ins = [zqkv, zba, conv_w, a_log, dt_bias]
    specs = [tok(3 * GDN_W), tok(LANE), full(conv_w), full(a_log), full(dt_bias)]
    if axial:
        ins += list(rope)
        specs += [full(t) for t in rope]
    tile = lambda w: pl.BlockSpec((2, PREP_CHUNKS, GDN_HEADS, CHUNK, w), lambda b, p: (0, b * nparts + p, 0, 0, 0))
    shape = lambda w, dt: jax.ShapeDtypeStruct((2, nchunk, GDN_HEADS, CHUNK, w), dt)
    return pl.pallas_call(
        functools.partial(_gdn_prep_kernel, axial, seq),
        out_shape=[shape(CHUNK, F32), shape(CHUNK, BF16), shape(2 * GDN_DK, BF16),
                   shape(GDN_DK, BF16), shape(GDN_DK, BF16),
                   jax.ShapeDtypeStruct((nchunk, 2, LANE), F32)],
        grid=(nseq, nparts),
        in_specs=specs,
        out_specs=[tile(CHUNK), tile(CHUNK), tile(2 * GDN_DK), tile(GDN_DK), tile(GDN_DK),
                   pl.BlockSpec((PREP_CHUNKS, 2, LANE), lambda b, p: (b * nparts + p, 0, 0))],
        scratch_shapes=[pltpu.VMEM((seq + 2 * HALO, 3 * GDN_W), F32)],
        compiler_params=_params(("parallel", "arbitrary")),
        name="gdn_prep_lat" if axial else "gdn_prep_ctx",
    )(*ins)


TRI_BLK = 8
TRI_ELEMS = CHUNK * CHUNK


def _tri_inverse_kernel(groups_per_dir, a_ref, o_ref, at_ref, tt_ref):
    backward = pl.program_id(0) >= groups_per_dir
    for blk in range(TRI_ELEMS // LANE):
        cols = slice(blk * LANE, (blk + 1) * LANE)
        at_ref[cols, :] = a_ref[:, cols].T
    tt_ref[...] = jnp.zeros(tt_ref.shape, F32)
    nblk = CHUNK // TRI_BLK

    def substitute(mirror):
        phys = (lambda idx: CHUNK - 1 - idx) if mirror else (lambda idx: idx)
        for ib in range(nblk):
            c_lo, width = ((nblk - 1 - ib) * TRI_BLK if mirror else 0), (ib + 1) * TRI_BLK

            def row_body(ii, carry, ib=ib, c_lo=c_lo, width=width):
                i = phys(ib * TRI_BLK + ii)
                cidx = lax.broadcasted_iota(jnp.int32, (width, LANE), 0) + c_lo
                acc = (cidx == i).astype(F32)
                for jb in range(ib + 1):
                    w_j = (jb + 1) * TRI_BLK
                    cj_lo = (nblk - 1 - jb) * TRI_BLK if mirror else 0
                    sub = slice(cj_lo - c_lo, cj_lo - c_lo + w_j)
                    part = acc[sub]
                    for jl in range(jb * TRI_BLK, (jb + 1) * TRI_BLK):
                        j = phys(jl)
                        arow = at_ref[pl.ds(i * CHUNK + j, 1), :]
                        part = part - arow * tt_ref[j * CHUNK + cj_lo:j * CHUNK + cj_lo + w_j, :]
                    acc = part if w_j == width else (
                        jnp.concatenate([part, acc[w_j:]], axis=0) if not mirror
                        else jnp.concatenate([acc[:width - w_j], part], axis=0))
                tt_ref[pl.ds(pl.multiple_of(i * CHUNK + c_lo, TRI_BLK), width), :] = acc
                return carry

            lax.fori_loop(0, TRI_BLK, row_body, 0)

    @pl.when(jnp.logical_not(backward))
    def _():
        substitute(False)

    @pl.when(backward)
    def _():
        substitute(True)

    for blk in range(TRI_ELEMS // LANE):
        cols = slice(blk * LANE, (blk + 1) * LANE)
        o_ref[:, cols] = tt_ref[cols, :].T.astype(BF16)


def _tri_inverse(a):
    nprob = a.shape[1] * a.shape[2]
    groups_per_dir = nprob // LANE
    blk = pl.BlockSpec((LANE, TRI_ELEMS), lambda g: (g, 0))
    out = pl.pallas_call(
        functools.partial(_tri_inverse_kernel, groups_per_dir),
        out_shape=jax.ShapeDtypeStruct((2 * nprob, TRI_ELEMS), BF16),
        grid=(2 * groups_per_dir,),
        in_specs=[blk],
        out_specs=blk,
        scratch_shapes=[pltpu.VMEM((TRI_ELEMS, LANE), F32), pltpu.VMEM((TRI_ELEMS, LANE), F32)],
        compiler_params=_params(("parallel",)),
        name="tri_inverse",
    )(a.reshape(2 * nprob, TRI_ELEMS))
    return out.reshape(a.shape)


def _gdn_scan_kernel(nchunk, t_ref, qk_ref, r_ref, qd_ref, kd_ref, gl_ref, s0_ref, *rest):
    o_ref, sfin_ref, state_ref = rest[-3:]
    state_ref[...] = s0_ref[...]
    streams = [(d, hd) for d in range(2) for hd in range(GDN_HEADS)]

    def body(c, carry):
        cc = (c, nchunk - 1 - c)
        uws = [_dot(t_ref[d, cc[d], hd], r_ref[d, cc[d], hd]).astype(BF16) for d, hd in streams]
        mbs = [_dot_tn(kd_ref[d, cc[d], hd], uw) for (d, hd), uw in zip(streams, uws)]
        qos = [_dot(qk_ref[d, cc[d], hd], uw) for (d, hd), uw in zip(streams, uws)]
        sts = [state_ref[d * GDN_HEADS + hd] for d, hd in streams]
        sbs = [s.astype(BF16) for s in sts]
        for (d, hd), qo, sb in zip(streams, qos, sbs):
            rows = pl.ds(pl.multiple_of(cc[d] * CHUNK, CHUNK), CHUNK)
            qp = (qd_ref[d, cc[d], hd].astype(F32) - qo[:, GDN_DK:]).astype(BF16)
            o_ref[d, rows, hd * GDN_DK:(hd + 1) * GDN_DK] = _dot(qp, sb) + qo[:, :GDN_DK]
        for (d, hd), mb, s, sb in zip(streams, mbs, sts, sbs):
            sidx = d * GDN_HEADS + hd
            gl = gl_ref[cc[d]][d:d + 1, LOG_COL + sidx:LOG_COL + sidx + 1]
            state_ref[sidx] = s * gl - _dot(mb[:, GDN_DK:].astype(BF16), sb) + mb[:, :GDN_DK]
        return carry

    lax.fori_loop(0, nchunk, body, 0)
    sfin_ref[...] = state_ref[...]


def _gdn_scan(seq, nseq, tinv, qk, r, qd, kd, gl, s0, o_all=None):
    nc = seq // CHUNK
    blk0 = 0 if o_all is None else N_CTX // seq
    tile = lambda w: pl.BlockSpec((2, nc, GDN_HEADS, CHUNK, w), lambda b: (0, b, 0, 0, 0))
    sblk = pl.BlockSpec((None, N_STREAM, GDN_DK, GDN_DK), lambda b: (b, 0, 0, 0))
    oblk = pl.BlockSpec((2, seq, GDN_W), lambda b: (0, blk0 + b, 0))
    ins = [tinv, qk, r, qd, kd, gl, s0]
    specs = [tile(CHUNK), tile(CHUNK), tile(2 * GDN_DK), tile(GDN_DK), tile(GDN_DK),
             pl.BlockSpec((nc, 2, LANE), lambda b: (b, 0, 0)), sblk]
    aliases = {}
    if o_all is not None:
        ins.append(o_all)
        specs.append(pl.BlockSpec(memory_space=pl.ANY))
        aliases = {len(ins) - 1: 0}
    return pl.pallas_call(
        functools.partial(_gdn_scan_kernel, nc),
        out_shape=[jax.ShapeDtypeStruct((2, N_TOK, GDN_W), F32),
                   jax.ShapeDtypeStruct((nseq, N_STREAM, GDN_DK, GDN_DK), F32)],
        grid=(nseq,),
        in_specs=specs,
        out_specs=[oblk, sblk],
        scratch_shapes=[pltpu.VMEM((N_STREAM, GDN_DK, GDN_DK), F32)],
        input_output_aliases=aliases,
        compiler_params=_params(("parallel",)),
        name="gdn_scan_lat" if seq == DEC_SEQ else "gdn_scan_ctx",
    )(*ins)


def _first_max(vals):
    sel = []
    taken = None
    for a, va in enumerate(vals):
        is_max = None
        for b, vb in enumerate(vals):
            if a == b:
                continue
            c = va >= vb
            is_max = c if is_max is None else (is_max & c)
        if taken is not None:
            is_max = is_max & jnp.logical_not(taken)
        sel.append(is_max)
        taken = is_max if taken is None else (taken | is_max)
    return sel


PAIR_SLOTS = ((0, 1), (0, 2), (0, 3), (1, 3), (1, 2), (3, 2))
N_CLASS = N_GROUPS * len(PAIR_SLOTS)
CLASS_EXPERTS = tuple((g * EXP_PER_GROUP + a, g * EXP_PER_GROUP + b)
                      for g in range(N_GROUPS) for a, b in PAIR_SLOTS)


def _route_rows(lt):
    gl = [lt[a:a + 1, :] for a in range(N_GROUPS)]
    gsel = _first_max(gl)
    gmax = functools.reduce(jnp.maximum, gl)
    gden = functools.reduce(lambda x, y: x + y, [jnp.exp(x - gmax) for x in gl])
    g_w = 1.0 / gden
    el = []
    for kx in range(EXP_PER_GROUP):
        acc = None
        for a in range(N_GROUPS):
            row = N_GROUPS + a * EXP_PER_GROUP + kx
            term = jnp.where(gsel[a], lt[row:row + 1, :], 0.0)
            acc = term if acc is None else acc + term
        el.append(acc)
    emax = functools.reduce(jnp.maximum, el)
    ee = [jnp.exp(x - emax) for x in el]
    eden = functools.reduce(lambda x, y: x + y, ee)
    ep = [x / eden for x in ee]
    top1 = _first_max(ep)
    ep2 = [jnp.where(top1[kx], -1.0, ep[kx]) for kx in range(EXP_PER_GROUP)]
    top2 = _first_max(ep2)
    chosen = [top1[kx] | top2[kx] for kx in range(EXP_PER_GROUP)]
    wsum = functools.reduce(lambda x, y: x + y,
                            [jnp.where(chosen[kx], ep[kx], 0.0) for kx in range(EXP_PER_GROUP)])
    within = [jnp.where(chosen[kx], ep[kx] / wsum, 0.0) for kx in range(EXP_PER_GROUP)]
    cls = jnp.zeros(g_w.shape, jnp.int32)
    w_a = jnp.zeros(g_w.shape, F32)
    w_b = jnp.zeros(g_w.shape, F32)
    for g in range(N_GROUPS):
        for kx, (a, b) in enumerate(PAIR_SLOTS):
            hit = gsel[g] & chosen[a] & chosen[b]
            cls = jnp.where(hit, g * len(PAIR_SLOTS) + kx, cls)
            w_a = jnp.where(hit, g_w * within[a], w_a)
            w_b = jnp.where(hit, g_w * within[b], w_b)
    return cls, w_a, w_b


H2W = D_MODEL + LANE
COL_WA = D_MODEL
COL_WB = D_MODEL + 1


def _outproj_kernel(nx, idx_ref, *refs):
    x = _token_tile(nx, refs)
    (ona_ref, og_ref, zg_ref, mod_ref, wo_ref, gng_ref, n2g_ref, wr_ref, br_ref,
     x1_ref, h2x_ref, cls_ref) = refs[nx:]
    del idx_ref
    m = mod_ref[...]
    og = og_ref[0] + og_ref[1]
    zg = zg_ref[...]
    parts = []
    for hd in range(GDN_HEADS):
        cols = slice(hd * GDN_DK, (hd + 1) * GDN_DK)
        oh = og[:, cols]
        oh = oh * lax.rsqrt(jnp.mean(oh * oh, axis=-1, keepdims=True) + EPS) * gng_ref[...]
        parts.append((oh * _silu(zg[:, cols])).astype(BF16))
    mix = _dot(ona_ref[...].astype(BF16), wo_ref[0:NA_W, :])
    for hd in range(GDN_HEADS):
        r0 = NA_W + hd * GDN_DK
        mix = mix + _dot(parts[hd], wo_ref[r0:r0 + GDN_DK, :])
    x1 = x + m[2:3, :] * mix
    x1_ref[...] = x1
    y = x1 * lax.rsqrt(jnp.mean(x1 * x1, axis=-1, keepdims=True) + EPS) * n2g_ref[...]
    h2 = y * (1.0 + m[4:5, :]) + m[3:4, :]
    h2x_ref[:, 0:D_MODEL] = h2.astype(BF16).astype(F32)
    logits = _dot3(h2, wr_ref[...]) + br_ref[...]
    cls, w_a, w_b = _route_rows(logits.T)
    rid = lax.broadcasted_iota(jnp.int32, (8, TM), 0)
    wt = jnp.where(rid == COL_WA - D_MODEL, w_a, jnp.where(rid == COL_WB - D_MODEL, w_b, 0.0))
    wt = jnp.concatenate([wt, jnp.zeros((LANE - 8, TM), F32)], axis=0)
    h2x_ref[:, D_MODEL:H2W] = wt.T
    cls_ref[...] = jnp.broadcast_to(cls, (8, TM))


def _outproj(tile_mod, x, o_na, o_gdn, zg, mod, w_out, gng, n2g, w_r, b_r):
    tok = lambda w: pl.BlockSpec((TM, w), lambda t, idx: (t, 0))
    full = lambda a: pl.BlockSpec(a.shape, lambda t, idx: (0,) * a.ndim)
    xs = list(x) if isinstance(x, tuple) else [x]
    return pl.pallas_call(
        functools.partial(_outproj_kernel, len(xs)),
        out_shape=[jax.ShapeDtypeStruct((N_TOK, D_MODEL), F32),
                   jax.ShapeDtypeStruct((N_TOK, H2W), F32),
                   jax.ShapeDtypeStruct((N_TOK // TM, 8, TM), jnp.int32)],
        grid_spec=pltpu.PrefetchScalarGridSpec(
            num_scalar_prefetch=1,
            grid=(N_TOK // TM,),
            in_specs=_token_specs(x) + [
                tok(NA_W),
                pl.BlockSpec((2, TM, GDN_W), lambda t, idx: (0, t, 0)), tok(GDN_W),
                pl.BlockSpec((None, 6, D_MODEL), lambda t, idx: (idx[t], 0, 0)),
                full(w_out), full(gng), full(n2g), full(w_r), full(b_r),
            ],
            out_specs=[tok(D_MODEL), tok(H2W), pl.BlockSpec((None, 8, TM), lambda t, idx: (t, 0, 0))],
        ),
        compiler_params=_params(("parallel",)),
        name="outproj",
    )(tile_mod, *xs, o_na, o_gdn, zg, mod, w_out, gng, n2g, w_r, b_r)


TM_X = 256
N_XTILE = N_TOK // TM_X + N_CLASS
N_SLOT = N_XTILE * TM_X
CLS_ROWS = N_TOK // LANE
TAB_EA, TAB_EB, TAB_USED, TAB_LAST = 0, 1, 2, 3


def _route_pos_kernel(cls_ref, slot_ref, tab_ref):
    cls = cls_ref[...]
    li = lax.broadcasted_iota(jnp.int32, (LANE, LANE), 0)
    lj = lax.broadcasted_iota(jnp.int32, (LANE, LANE), 1)
    before_lane = (li < lj).astype(BF16)
    ri = lax.broadcasted_iota(jnp.int32, (CLS_ROWS, CLS_ROWS), 0)
    rj = lax.broadcasted_iota(jnp.int32, (CLS_ROWS, CLS_ROWS), 1)
    before_row = (rj < ri).astype(BF16)
    lane = lax.broadcasted_iota(jnp.int32, (1, LANE), 1)
    tile_start = lane * TM_X
    off = jnp.zeros((1, 1), jnp.int32)
    slot = jnp.zeros(cls.shape, jnp.int32)
    tab_a = jnp.zeros((1, LANE), jnp.int32)
    tab_b = jnp.zeros((1, LANE), jnp.int32)
    tab_last = jnp.full((1, LANE), -1, jnp.int32)
    for c in range(N_CLASS):
        hit = cls == c
        one = hit.astype(F32)
        in_row = _dot(one.astype(BF16), before_lane)
        row_tot = jnp.broadcast_to(jnp.sum(one, axis=1, keepdims=True), one.shape)
        rank = in_row + _dot(before_row, row_tot.astype(BF16))
        count = jnp.sum(row_tot[:, 0:1], axis=0, keepdims=True).astype(jnp.int32)
        slot = jnp.where(hit, off + rank.astype(jnp.int32), slot)
        nxt = off + (((count + (TM_X - 1)) >> 8) << 8)
        mine = (tile_start >= off) & (tile_start < nxt)
        tab_a = jnp.where(mine, CLASS_EXPERTS[c][0], tab_a)
        tab_b = jnp.where(mine, CLASS_EXPERTS[c][1], tab_b)
        tab_last = jnp.where((lane == c) & (nxt > off), (nxt >> 8) - 1, tab_last)
        off = nxt
    slot_ref[...] = slot
    row = lax.broadcasted_iota(jnp.int32, (8, LANE), 0)
    used = jnp.broadcast_to(off >> 8, (8, LANE))
    tab_ref[...] = jnp.where(row == TAB_EA, tab_a, jnp.where(row == TAB_EB, tab_b,
                                                               jnp.where(row == TAB_LAST, tab_last, used)))


def _route_positions(cls):
    assert TM_X == 256
    return pl.pallas_call(
        _route_pos_kernel,
        out_shape=[jax.ShapeDtypeStruct((CLS_ROWS, LANE), jnp.int32),
                   jax.ShapeDtypeStruct((8, LANE), jnp.int32)],
        name="route_positions",
    )(cls)


def _row_copy(src, src_row, dst, dst_row, sem):
    return pltpu.make_async_copy(src.at[pl.ds(src_row, 1), :], dst.at[pl.ds(dst_row, 1), :], sem)


ROW_UNROLL = 8


def _permute_kernel(slot_ref, tab_ref, h_ref, xs_ref, buf_ref, zero_ref, sem_ref, zsem_ref):
    t = pl.program_id(0)
    cur = t % 2

    @pl.when(t == 0)
    def _():
        zero_ref[...] = jnp.zeros(zero_ref.shape, F32)

        def tile_copy(c):
            return pltpu.make_async_copy(
                zero_ref, xs_ref.at[pl.ds(pl.multiple_of(tab_ref[TAB_LAST, c] * TM_X, TM_X), TM_X), :], zsem_ref.at[0])

        for c in range(N_CLASS):
            @pl.when(tab_ref[TAB_LAST, c] >= 0)
            def _():
                tile_copy(c).start()
        for c in range(N_CLASS):
            @pl.when(tab_ref[TAB_LAST, c] >= 0)
            def _():
                tile_copy(c).wait()

    buf_ref[cur] = h_ref[...]

    def issue(i, carry):
        _row_copy(buf_ref.at[cur], i, xs_ref, slot_ref[t * TM + i], sem_ref.at[cur]).start()
        return carry

    lax.fori_loop(0, TM, issue, 0, unroll=ROW_UNROLL)

    def drain(which):
        def wait(i, carry):
            _row_copy(buf_ref.at[which], 0, xs_ref, 0, sem_ref.at[which]).wait()
            return carry
        lax.fori_loop(0, TM, wait, 0, unroll=ROW_UNROLL)

    @pl.when(t > 0)
    def _():
        drain(1 - cur)

    @pl.when(t == pl.num_programs(0) - 1)
    def _():
        drain(cur)


def _permute(slot, tab, h2x):
    assert TM_X == TM
    return pl.pallas_call(
        _permute_kernel,
        out_shape=jax.ShapeDtypeStruct((N_SLOT, H2W), F32),
        grid_spec=pltpu.PrefetchScalarGridSpec(
            num_scalar_prefetch=2,
            grid=(N_TOK // TM,),
            in_specs=[pl.BlockSpec((TM, H2W), lambda t, s, tab: (t, 0))],
            out_specs=pl.BlockSpec(memory_space=pl.ANY),
            scratch_shapes=[pltpu.VMEM((2, TM, H2W), F32), pltpu.VMEM((TM_X, H2W), F32),
                            pltpu.SemaphoreType.DMA((2,)), pltpu.SemaphoreType.DMA((1,))],
        ),
        compiler_params=_params(("arbitrary",)),
        name="moe_permute",
    )(slot, tab, h2x)


def _experts_kernel(tab_ref, xs_ref, w1a_ref, w3a_ref, w2a_ref, w1b_ref, w3b_ref, w2b_ref, ys_ref,
                    c1a_ref, c3a_ref, c2a_ref, c1b_ref, c3b_ref, c2b_ref):
    t = pl.program_id(0)
    prev = jnp.maximum(t - 1, 0)

    @pl.when(t < tab_ref[TAB_USED, 0])
    def _():
        @pl.when((t == 0) | (tab_ref[TAB_EA, t] != tab_ref[TAB_EA, prev]))
        def _():
            c1a_ref[...] = w1a_ref[...].astype(BF16)
            c3a_ref[...] = w3a_ref[...].astype(BF16)
            c2a_ref[...] = w2a_ref[...].astype(BF16)

        @pl.when((t == 0) | (tab_ref[TAB_EB, t] != tab_ref[TAB_EB, prev]))
        def _():
            c1b_ref[...] = w1b_ref[...].astype(BF16)
            c3b_ref[...] = w3b_ref[...].astype(BF16)
            c2b_ref[...] = w2b_ref[...].astype(BF16)

        x = xs_ref[:, 0:D_MODEL].astype(BF16)
        gates = xs_ref[:, D_MODEL:H2W]
        w_a = gates[:, COL_WA - D_MODEL:COL_WA - D_MODEL + 1]
        w_b = gates[:, COL_WB - D_MODEL:COL_WB - D_MODEL + 1]
        a1 = _dot(x, c1a_ref[...])
        a3 = _dot(x, c3a_ref[...])
        b1 = _dot(x, c1b_ref[...])
        b3 = _dot(x, c3b_ref[...])
        sa = (_silu(a1) * a3 * w_a).astype(BF16)
        sb = (_silu(b1) * b3 * w_b).astype(BF16)
        ys_ref[...] = _dot(sa, c2a_ref[...]) + _dot(sb, c2b_ref[...])


def _experts(layer, tab, xs, w1, w3, w2):
    last = lambda t, tab: jnp.minimum(t, tab[TAB_USED, 0] - 1)
    tile = lambda w: pl.BlockSpec((TM_X, w), lambda t, tab: (last(t, tab), 0))
    wspec = lambda shape, row: pl.BlockSpec((None, None) + shape,
                                            lambda t, tab: (layer, tab[row, last(t, tab)], 0, 0))
    up, down = (D_MODEL, EXPERT_FF), (EXPERT_FF, D_MODEL)
    return pl.pallas_call(
        _experts_kernel,
        out_shape=jax.ShapeDtypeStruct((N_SLOT, D_MODEL), F32),
        grid_spec=pltpu.PrefetchScalarGridSpec(
            num_scalar_prefetch=1,
            grid=(N_XTILE,),
            in_specs=[tile(H2W),
                      wspec(up, TAB_EA), wspec(up, TAB_EA), wspec(down, TAB_EA),
                      wspec(up, TAB_EB), wspec(up, TAB_EB), wspec(down, TAB_EB)],
            out_specs=tile(D_MODEL),
            scratch_shapes=[pltpu.VMEM(up, BF16), pltpu.VMEM(up, BF16), pltpu.VMEM(down, BF16),
                            pltpu.VMEM(up, BF16), pltpu.VMEM(up, BF16), pltpu.VMEM(down, BF16)],
        ),
        compiler_params=_params(("arbitrary",)),
        name="moe_experts",
    )(tab, xs, w1, w3, w2, w1, w3, w2)


def _unpermute_kernel(idx_ref, slot_ref, x1_ref, mod_ref, ys_ref, *rest):
    outs, (buf_ref, sem_ref) = rest[:-2], rest[-2:]
    del idx_ref
    t = pl.program_id(0)
    cur = t % 2

    def fetch(tile, which):
        def issue(i, carry):
            _row_copy(ys_ref, slot_ref[tile * TM + i], buf_ref.at[which], i, sem_ref.at[which]).start()
            return carry
        lax.fori_loop(0, TM, issue, 0, unroll=ROW_UNROLL)

    @pl.when(t == 0)
    def _():
        fetch(0, 0)

    @pl.when(t + 1 < pl.num_programs(0))
    def _():
        fetch(t + 1, 1 - cur)

    def wait(i, carry):
        _row_copy(ys_ref, 0, buf_ref.at[cur], 0, sem_ref.at[cur]).wait()
        return carry

    lax.fori_loop(0, TM, wait, 0, unroll=ROW_UNROLL)
    y = x1_ref[...] + mod_ref[5:6, :] * buf_ref[cur]
    if len(outs) == 1:
        outs[0][...] = y
    else:
        @pl.when(t < N_CTX // TM)
        def _():
            outs[0][...] = y

        @pl.when(t >= N_CTX // TM)
        def _():
            outs[1][...] = y


def _unpermute(tile_mod, slot, x1, mod, ys, split):
    tok = pl.BlockSpec((TM, D_MODEL), lambda t, idx, s: (t, 0))
    if split:
        out_shape = [jax.ShapeDtypeStruct((N_CTX, D_MODEL), F32), jax.ShapeDtypeStruct((N_LAT, D_MODEL), F32)]
        out_specs = _token_specs((None, None))
    else:
        out_shape, out_specs = jax.ShapeDtypeStruct((N_TOK, D_MODEL), F32), tok
    return pl.pallas_call(
        _unpermute_kernel,
        out_shape=out_shape,
        grid_spec=pltpu.PrefetchScalarGridSpec(
            num_scalar_prefetch=2,
            grid=(N_TOK // TM,),
            in_specs=[tok, pl.BlockSpec((None, 6, D_MODEL), lambda t, idx, s: (idx[t], 0, 0)),
                      pl.BlockSpec(memory_space=pl.ANY)],
            out_specs=out_specs,
            scratch_shapes=[pltpu.VMEM((2, TM, D_MODEL), F32), pltpu.SemaphoreType.DMA((2,))],
        ),
        compiler_params=_params(("arbitrary",)),
        name="moe_unpermute",
    )(tile_mod, slot, x1, mod, ys)


def _tile_mod_index(tile):
    t = np.arange(N_TOK // tile) * tile
    return jnp.asarray(np.where(t < N_CTX, 0, 1 + (t - N_CTX) // DEC_SEQ), jnp.int32)


def kernel(x_prompt, x_sample, c, cache_k, cache_v, state_ssm, c_ctx, ada_w, ada_b, norm1_g, norm2_g, w_in, w_out, na_qn_g, na_kn_g, na_rpb, gdn_conv_w, gdn_a_log, gdn_dt_bias, gdn_norm_g, moe_w_rg, moe_b_rg, moe_w_re, moe_b_re, moe_w1, moe_w3, moe_w2):
    x = (x_prompt.reshape(N_CTX, D_MODEL), x_sample.reshape(N_LAT, D_MODEL))
    cv = jnp.concatenate([c_ctx[None, :], c, jnp.zeros((N_MOD_PAD - N_MOD, D_MODEL), F32)], axis=0)
    mod_all = _modulation(cv, ada_w, ada_b).reshape(DEPTH, N_MOD_PAD, 6, D_MODEL)

    idx_tm = _tile_mod_index(TM)
    hh = np.arange(NA_W) // NA_HD
    ones_bd = jnp.asarray(hh[:, None] == hh[None, :], BF16)
    rope = _rope_tables()
    cache_k4 = cache_k.reshape(DEC_BATCH, DEPTH, PAST_LEN, NA_W)
    cache_v4 = cache_v.reshape(DEC_BATCH, DEPTH, PAST_LEN, NA_W)
    lane_pad = lambda a, at: jnp.zeros((1, LANE), F32).at[0, at:at + a.size].set(a.reshape(-1))
    zeros_state = jnp.zeros((BATCH, 2 * GDN_HEADS, GDN_DK, GDN_DK), F32)

    caches, ss = None, []
    for l in range(DEPTH):
        mod = mod_all[l]
        w_main = w_in[l, :, :N_MAIN].astype(BF16)
        w_ba = jnp.zeros((D_MODEL, LANE), BF16).at[:, :N_IN - N_MAIN].set(w_in[l, :, N_MAIN:].astype(BF16))
        qg = jnp.tile(na_qn_g[l], NA_HEADS)[None, :]
        kg = jnp.tile(na_kn_g[l], NA_HEADS)[None, :]
        q, k, v, zqkv, zg, zba, *caches = _inproj(l, idx_tm, x, mod, norm1_g[l][None, :], w_main, w_ba,
                                                  ones_bd, qg, kg, caches)

        bias = _na_bias_table(na_rpb[l])
        o_na = _na_attention(l, q, k, v, cache_k4, cache_v4, bias, _ctx_attention(q, k, v))

        conv_w = jnp.zeros((8, 3 * GDN_W), F32).at[:CONV_K].set(gdn_conv_w[l])
        a_row = lane_pad(gdn_a_log[l], 2 * GDN_HEADS)
        dt_row = lane_pad(gdn_dt_bias[l], 2 * GDN_HEADS)
        a_c, *ops_c = _gdn_prep(False, zqkv, zba, conv_w, a_row, dt_row, None)
        a_l, *ops_l = _gdn_prep(True, zqkv, zba, conv_w, a_row, dt_row, rope)
        og_c, s_ctx = _gdn_scan(SEQ, BATCH, _tri_inverse(a_c), *ops_c, zeros_state)
        s0_lat = state_ssm[:, l].reshape(DEC_BATCH, N_STREAM, GDN_DK, GDN_DK)
        o_gdn, _ = _gdn_scan(DEC_SEQ, DEC_BATCH, _tri_inverse(a_l), *ops_l, s0_lat, og_c)

        w_r = jnp.zeros((D_MODEL, LANE), F32).at[:, :N_GROUPS].set(moe_w_rg[l])
        w_r = w_r.at[:, N_GROUPS:N_GROUPS + N_EXPERTS].set(moe_w_re[l])
        b_r = lane_pad(jnp.concatenate([moe_b_rg[l], moe_b_re[l]]), 0)
        x1, h2x, cls = _outproj(idx_tm, x, o_na, o_gdn, zg, mod, w_out[l].astype(BF16),
                                gdn_norm_g[l][None, :], norm2_g[l][None, :], w_r, b_r)
        slot, tab = _route_positions(cls[:, 0, :].reshape(CLS_ROWS, LANE))
        slot = slot.reshape(N_TOK)
        ys = _experts(l, tab, _permute(slot, tab, h2x), moe_w1, moe_w3, moe_w2)
        x = _unpermute(idx_tm, slot, x1, mod, ys, split=l == DEPTH - 1)

        ss.append(s_ctx.reshape(BATCH, 2, GDN_HEADS, GDN_DK, GDN_DK))

    y_prompt = x[0].reshape(BATCH, SEQ, D_MODEL)
    y_sample = x[1].reshape(DEC_BATCH, DEC_SEQ, D_MODEL)
    new_k, new_v = (a.reshape(BATCH, DEPTH, SEQ, NA_HEADS, NA_HD) for a in caches)
    return (y_prompt, y_sample, new_k, new_v, jnp.stack(ss, axis=1))
```

```python
import functools

import jax
import jax.numpy as jnp
import numpy as np
from jax import lax
from jax.experimental import pallas as pl
from jax.experimental.pallas import tpu as pltpu

F32 = jnp.float32
BF16 = jnp.bfloat16

D_MODEL = 1024
BATCH = 16
SEQ = 256
DEPTH = 2
DEC_BATCH = 8
DEC_SEQ = 1024
PAST_LEN = 512
GRID_W = 64
GRID_ROWS = DEC_SEQ // GRID_W
NA_HEADS = 8
NA_HD = 64
NA_W = NA_HEADS * NA_HD
NA_KH = 8
NA_KW = 16
GDN_HEADS = 4
GDN_DK = 128
GDN_W = GDN_HEADS * GDN_DK
CONV_K = 5
CHUNK = 64
ROPE_BASE = 10000.0
N_GROUPS = 4
EXP_PER_GROUP = 4
N_EXPERTS = 16
EXPERT_FF = 512
EPS = 1e-6
N_IN = 3 * NA_W + 4 * GDN_W + 4 * GDN_HEADS

N_CTX = BATCH * SEQ
N_LAT = DEC_BATCH * DEC_SEQ
N_TOK = N_CTX + N_LAT
N_MOD = 1 + DEC_BATCH
N_MOD_PAD = 16
LANE = 128
N_MAIN = 3 * NA_W + 4 * GDN_W
TM = 256
VMEM_LIMIT = 56 * 1024 * 1024


def _dot(a, b):
    return jnp.dot(a, b, preferred_element_type=F32)


def _dot_nt(a, b):
    return lax.dot_general(a, b, (((1,), (1,)), ((), ())), preferred_element_type=F32)


def _dot_tn(a, b):
    return lax.dot_general(a, b, (((0,), (0,)), ((), ())), preferred_element_type=F32)


def _split2(x):
    hi = x.astype(BF16)
    lo = (x - hi.astype(F32)).astype(BF16)
    return hi, lo


def _split3(x):
    hi = x.astype(BF16)
    r = x - hi.astype(F32)
    mid = r.astype(BF16)
    lo = (r - mid.astype(F32)).astype(BF16)
    return hi, mid, lo


def _dot3(a, b):
    ah, al = _split2(a)
    bh, bl = _split2(b)
    return _dot(ah, bh) + (_dot(ah, bl) + _dot(al, bh))


def _dot_exact_lhs(mask_bf16, x):
    hi, mid, lo = _split3(x)
    return _dot(mask_bf16, hi) + (_dot(mask_bf16, mid) + _dot(mask_bf16, lo))


def _silu(x):
    return x * (1.0 / (1.0 + jnp.exp(-x)))


def _params(sem):
    return pltpu.CompilerParams(dimension_semantics=sem, vmem_limit_bytes=VMEM_LIMIT)


def _mod_kernel(cv_ref, w_ref, b_ref, o_ref):
    cv = cv_ref[...]
    s = _silu(cv)
    o_ref[...] = jnp.dot(s, w_ref[...], preferred_element_type=F32,
                         precision=lax.Precision.HIGHEST) + b_ref[...]


def _modulation(cv, ada_w, ada_b):
    nblk = 512
    return pl.pallas_call(
        _mod_kernel,
        out_shape=jax.ShapeDtypeStruct((DEPTH, N_MOD_PAD, 6 * D_MODEL), F32),
        grid=(DEPTH, 6 * D_MODEL // nblk),
        in_specs=[
            pl.BlockSpec((N_MOD_PAD, D_MODEL), lambda l, j: (0, 0)),
            pl.BlockSpec((None, D_MODEL, nblk), lambda l, j: (l, 0, j)),
            pl.BlockSpec((None, 1, nblk), lambda l, j: (l, 0, j)),
        ],
        out_specs=pl.BlockSpec((None, N_MOD_PAD, nblk), lambda l, j: (l, 0, j)),
        compiler_params=_params(("parallel", "parallel")),
        name="modulation",
    )(cv, ada_w, ada_b.reshape(DEPTH, 1, 6 * D_MODEL))


def _token_specs(x):
    if not isinstance(x, tuple):
        return [pl.BlockSpec((TM, D_MODEL), lambda t, *_: (t, 0))]
    return [pl.BlockSpec((TM, D_MODEL), lambda t, *_: (jnp.minimum(t, N_CTX // TM - 1), 0)),
            pl.BlockSpec((TM, D_MODEL), lambda t, *_: (jnp.maximum(t - N_CTX // TM, 0), 0))]


def _token_tile(nx, refs):
    if nx == 1:
        return refs[0][...]
    return jnp.where(pl.program_id(0) < N_CTX // TM, refs[0][...], refs[1][...])


def _inproj_kernel(nx, idx_ref, *refs):
    del idx_ref
    _inproj_body(_token_tile(nx, refs), refs[nx:nx + 7], refs[-8:])


def _inproj_body(x, params, outs):
    mod_ref, n1g_ref, win_ref, wba_ref, ones_ref, qg_ref, kg_ref = params
    q_ref, k_ref, v_ref, zqkv_ref, zg_ref, zba_ref, kcache_ref, vcache_ref = outs
    y = x * lax.rsqrt(jnp.mean(x * x, axis=-1, keepdims=True) + EPS) * n1g_ref[...]
    m = mod_ref[...]
    h = (y * (1.0 + m[1:2, :]) + m[0:1, :]).astype(BF16)
    z = _dot(h, win_ref[...])
    zba_ref[...] = _dot(h, wba_ref[...])
    ones = ones_ref[...]

    def head_rms(zz, g):
        hi, lo = _split2(zz * zz)
        ms = (_dot(hi, ones) + _dot(lo, ones)) * (1.0 / NA_HD)
        return zz * lax.rsqrt(ms + EPS) * g

    q_ref[...] = head_rms(z[:, 0:NA_W], qg_ref[...]) * (NA_HD ** -0.5)
    k = head_rms(z[:, NA_W:2 * NA_W], kg_ref[...])
    v = z[:, 2 * NA_W:3 * NA_W]
    k_ref[...] = k
    v_ref[...] = v
    zqkv_ref[...] = z[:, 3 * NA_W:3 * NA_W + 3 * GDN_W]
    zg_ref[...] = z[:, 3 * NA_W + 3 * GDN_W:N_MAIN]

    @pl.when(pl.program_id(0) < BATCH)
    def _():
        kcache_ref[...] = k
        vcache_ref[...] = v


def _inproj(layer, tile_mod, x, mod, n1g, w_main, w_ba, ones_bd, qg, kg, caches):
    assert TM == SEQ
    tok = lambda w: pl.BlockSpec((TM, w), lambda t, idx: (t, 0))
    full = lambda a: pl.BlockSpec(a.shape, lambda t, idx: (0,) * a.ndim)
    out_w = (NA_W, NA_W, NA_W, 3 * GDN_W, GDN_W, LANE)
    cache_blk = pl.BlockSpec((None, None, SEQ, NA_W), lambda t, idx: (jnp.minimum(t, BATCH - 1), layer, 0, 0))
    cache_shape = jax.ShapeDtypeStruct((BATCH, DEPTH, SEQ, NA_W), F32)
    xs = list(x) if isinstance(x, tuple) else [x]
    ins = [tile_mod, *xs, mod, n1g, w_main, w_ba, ones_bd, qg, kg]
    specs = _token_specs(x) + [pl.BlockSpec((None, 6, D_MODEL), lambda t, idx: (idx[t], 0, 0)),
                               full(n1g), full(w_main), full(w_ba), full(ones_bd), full(qg), full(kg)]
    aliases = {}
    if caches is not None:
        aliases = {len(ins): len(out_w), len(ins) + 1: len(out_w) + 1}
        ins += list(caches)
        specs += [pl.BlockSpec(memory_space=pl.ANY)] * 2
    return pl.pallas_call(
        functools.partial(_inproj_kernel, len(xs)),
        out_shape=[jax.ShapeDtypeStruct((N_TOK, w), F32) for w in out_w] + [cache_shape, cache_shape],
        grid_spec=pltpu.PrefetchScalarGridSpec(
            num_scalar_prefetch=1,
            grid=(N_TOK // TM,),
            in_specs=specs,
            out_specs=[tok(w) for w in out_w] + [cache_blk, cache_blk],
        ),
        input_output_aliases=aliases,
        compiler_params=_params(("arbitrary",)),
        name="inproj",
    )(*ins)


def _row_copy(src, src_row, dst, dst_row, sem):
    return pltpu.make_async_copy(src.at[pl.ds(src_row, 1), :], dst.at[pl.ds(dst_row, 1), :], sem)


def _inproj_gather_kernel(idx_ref, slot_ref, x1_ref, modp_ref, ys_ref, *refs):
    del idx_ref
    params, outs, (buf_ref, sem_ref) = refs[:7], refs[-11:-2], refs[-2:]
    t = pl.program_id(0)
    last = pl.num_programs(0) - 1
    cur = t % 2

    def fetch(tile, which):
        for i in range(TM):
            _row_copy(ys_ref, slot_ref[tile * TM + i], buf_ref.at[which], i, sem_ref.at[which]).start()

    def drain(which):
        for i in range(TM):
            _row_copy(ys_ref, 0, buf_ref.at[which], 0, sem_ref.at[which]).wait()

    @pl.when(t == 0)
    def _():
        fetch(0, 0)

    fetch(jnp.minimum(t + 1, last), 1 - cur)
    drain(cur)
    x = x1_ref[...] + modp_ref[5:6, :] * buf_ref[cur]
    outs[0][...] = x
    _inproj_body(x, params, outs[1:])

    @pl.when(t == last)
    def _():
        drain(1 - cur)


def _inproj_gather(layer, tile_mod, slot, x1, mod_prev, ys, mod, n1g, w_main, w_ba, ones_bd, qg, kg, caches):
    tok = lambda w: pl.BlockSpec((TM, w), lambda t, idx, s: (t, 0))
    full = lambda a: pl.BlockSpec(a.shape, lambda t, idx, s: (0,) * a.ndim)
    modblk = pl.BlockSpec((None, 6, D_MODEL), lambda t, idx, s: (idx[t], 0, 0))
    out_w = (D_MODEL, NA_W, NA_W, NA_W, 3 * GDN_W, GDN_W, LANE)
    cache_blk = pl.BlockSpec((None, None, SEQ, NA_W), lambda t, idx, s: (jnp.minimum(t, BATCH - 1), layer, 0, 0))
    cache_shape = jax.ShapeDtypeStruct((BATCH, DEPTH, SEQ, NA_W), F32)
    ins = [tile_mod, slot, x1, mod_prev, ys, mod, n1g, w_main, w_ba, ones_bd, qg, kg, *caches]
    specs = [tok(D_MODEL), modblk, pl.BlockSpec(memory_space=pl.ANY), modblk,
             full(n1g), full(w_main), full(w_ba), full(ones_bd), full(qg), full(kg)]
    specs += [pl.BlockSpec(memory_space=pl.ANY)] * 2
    return pl.pallas_call(
        _inproj_gather_kernel,
        out_shape=[jax.ShapeDtypeStruct((N_TOK, w), F32) for w in out_w] + [cache_shape, cache_shape],
        grid_spec=pltpu.PrefetchScalarGridSpec(
            num_scalar_prefetch=2,
            grid=(N_TOK // TM,),
            in_specs=specs,
            out_specs=[tok(w) for w in out_w] + [cache_blk, cache_blk],
            scratch_shapes=[pltpu.VMEM((2, TM, D_MODEL), F32), pltpu.SemaphoreType.DMA((2,))],
        ),
        input_output_aliases={len(ins) - 2: len(out_w), len(ins) - 1: len(out_w) + 1},
        compiler_params=_params(("arbitrary",)),
        name="inproj_gather",
    )(*ins)


def _pair_masks(shape):
    lane = lax.broadcasted_iota(jnp.int32, shape, 1)
    return lane < NA_HD


def _ctx_attn_kernel(q_ref, k_ref, v_ref, o_ref):
    for p in range(NA_HEADS // 2):
        cols = slice(p * LANE, (p + 1) * LANE)
        q = q_ref[:, cols]
        kb = k_ref[:, cols].astype(BF16)
        vb = v_ref[:, cols].astype(BF16)
        first = _pair_masks(q.shape)
        outs = []
        for hm in (first, jnp.logical_not(first)):
            s = _dot_nt(jnp.where(hm, q, 0.0).astype(BF16), kb)
            e = jnp.exp(s - jnp.max(s, axis=-1, keepdims=True))
            o = _dot(e.astype(BF16), vb)
            outs.append(o / jnp.sum(e, axis=-1, keepdims=True))
        o_ref[:, cols] = jnp.where(first, outs[0], outs[1])


def _ctx_attention(q, k, v):
    blk = pl.BlockSpec((SEQ, NA_W), lambda b: (b, 0))
    return pl.pallas_call(
        _ctx_attn_kernel,
        out_shape=jax.ShapeDtypeStruct((N_TOK, NA_W), F32),
        grid=(BATCH,),
        in_specs=[blk, blk, blk],
        out_specs=blk,
        compiler_params=_params(("parallel",)),
        name="ctx_attention",
    )(q, k, v)


NA_PAIR_GROUP = 4


def _na_kernel(q_ref, k_ref, v_ref, kc_ref, vc_ref, bias_ref, o_all_ref, o_ref):
    del o_all_ref
    r = pl.program_id(1)
    base = jnp.clip(r - NA_KH // 2, 0, GRID_ROWS - NA_KH)
    dr0 = base - r + (NA_KH - 1)
    row0 = pl.multiple_of(base * GRID_W, GRID_W)
    nwin = NA_KH * GRID_W
    first = _pair_masks((GRID_W, LANE))
    col = lambda p: slice(p * LANE, (p + 1) * LANE)
    for g0 in range(0, NA_HEADS // 2, NA_PAIR_GROUP):
        pairs = range(g0, g0 + NA_PAIR_GROUP)
        heads = range(2 * g0, 2 * (g0 + NA_PAIR_GROUP))
        qh = {h: jnp.where(first if h % 2 == 0 else jnp.logical_not(first), q_ref[:, col(h // 2)], 0.0).astype(BF16)
              for h in heads}
        kw = {p: k_ref[pl.ds(row0, nwin), col(p)].astype(BF16) for p in pairs}
        kc = {p: kc_ref[:, col(p)].astype(BF16) for p in pairs}
        vw = {p: v_ref[pl.ds(row0, nwin), col(p)].astype(BF16) for p in pairs}
        vc = {p: vc_ref[:, col(p)].astype(BF16) for p in pairs}
        s_loc = {h: _dot_nt(qh[h], kw[h // 2]) + bias_ref[h, dr0] for h in heads}
        s_ctx = {h: _dot_nt(qh[h], kc[h // 2]) for h in heads}
        mx = {h: jnp.maximum(jnp.max(s_loc[h], axis=-1, keepdims=True), jnp.max(s_ctx[h], axis=-1, keepdims=True))
              for h in heads}
        e_loc = {h: jnp.exp(s_loc[h] - mx[h]) for h in heads}
        e_ctx = {h: jnp.exp(s_ctx[h] - mx[h]) for h in heads}
        den = {h: jnp.sum(e_loc[h], axis=-1, keepdims=True) + jnp.sum(e_ctx[h], axis=-1, keepdims=True)
               for h in heads}
        o = {h: _dot(e_loc[h].astype(BF16), vw[h // 2]) + _dot(e_ctx[h].astype(BF16), vc[h // 2]) for h in heads}
        for p in pairs:
            o_ref[:, col(p)] = jnp.where(first, o[2 * p] / den[2 * p], o[2 * p + 1] / den[2 * p + 1])


def _na_attention(layer, q, k, v, cache_k, cache_v, bias, o_all):
    lat0 = N_CTX // DEC_SEQ
    qblk = pl.BlockSpec((GRID_W, NA_W), lambda b, r: (N_CTX // GRID_W + b * GRID_ROWS + r, 0))
    kvblk = pl.BlockSpec((DEC_SEQ, NA_W), lambda b, r: (lat0 + b, 0))
    cblk = pl.BlockSpec((None, None, PAST_LEN, NA_W), lambda b, r: (b, layer, 0, 0))
    return pl.pallas_call(
        _na_kernel,
        out_shape=jax.ShapeDtypeStruct((N_TOK, NA_W), F32),
        grid=(DEC_BATCH, GRID_ROWS),
        in_specs=[qblk, kvblk, kvblk, cblk, cblk,
                  pl.BlockSpec(bias.shape, lambda b, r: (0, 0, 0, 0)),
                  pl.BlockSpec(memory_space=pl.ANY)],
        out_specs=qblk,
        input_output_aliases={6: 0},
        compiler_params=_params(("parallel", "arbitrary")),
        name="na_attention",
    )(q, k, v, cache_k, cache_v, bias, o_all)


N_DC = 2 * NA_KW - 1


def _na_bias_kernel(r_ref, o_ref):
    shape = (GRID_W, NA_KH * GRID_W)
    qc = lax.broadcasted_iota(jnp.int32, shape, 0)
    kc = lax.broadcasted_iota(jnp.int32, shape, 1) & (GRID_W - 1)
    ws = jnp.clip(qc - NA_KW // 2, 0, GRID_W - NA_KW)
    valid = (kc >= ws) & (kc < ws + NA_KW)
    dc = jnp.where(valid, kc - qc + (NA_KW - 1), -1)
    for d0 in range(NA_KH):
        acc = jnp.full(shape, -jnp.inf, F32)
        for d in range(N_DC):
            acc = jnp.where(dc == d, r_ref[d0, d:d + 1, :], acc)
        o_ref[d0] = acc


def _na_bias_table(rpb):
    win = jnp.stack([rpb[:, d0:d0 + NA_KH, :] for d0 in range(NA_KH)], axis=1)
    rexp = jnp.repeat(win.transpose(0, 1, 3, 2), GRID_W, axis=-1)
    rexp = jnp.pad(rexp, ((0, 0), (0, 0), (0, 32 - N_DC), (0, 0)))
    blk = lambda rows: pl.BlockSpec((None, NA_KH, rows, NA_KH * GRID_W), lambda h: (h, 0, 0, 0))
    return pl.pallas_call(
        _na_bias_kernel,
        out_shape=jax.ShapeDtypeStruct((NA_HEADS, NA_KH, GRID_W, NA_KH * GRID_W), F32),
        grid=(NA_HEADS,),
        in_specs=[blk(32)],
        out_specs=blk(GRID_W),
        compiler_params=_params(("parallel",)),
        name="na_bias",
    )(rexp)


def _rope_tables():
    t = np.arange(DEC_SEQ)
    half = GDN_DK // 2
    inv_freq = (np.float32(ROPE_BASE) ** (-np.arange(0, half, 2, dtype=np.float32) / np.float32(half)))
    ang_r = (t // GRID_W).astype(np.float32)[:, None] * inv_freq
    ang_c = (t % GRID_W).astype(np.float32)[:, None] * inv_freq
    cr, sr, cc, sc = np.cos(ang_r), np.sin(ang_r), np.cos(ang_c), np.sin(ang_c)
    z = np.zeros_like(sr)
    cos = np.concatenate([cr, cr, cc, cc], axis=1)
    s_up = np.concatenate([-sr, z, -sc, z], axis=1)
    s_dn = np.concatenate([z, sr, z, sc], axis=1)
    return (jnp.asarray(cos, F32), jnp.asarray(s_up, F32), jnp.asarray(s_dn, F32))


HALO = 8
PREP_CHUNKS = 4
N_STREAM = 2 * GDN_HEADS
LOG_COL = N_STREAM


def _gdn_prep_kernel(axial, seq, *refs):
    if axial:
        (zqkv_ref, zba_ref, cw_ref, alog_ref, dtb_ref, cos_ref, sup_ref, sdn_ref,
         a_ref, qk_ref, r_ref, qd_ref, kd_ref, gl_ref, xs_ref) = refs
    else:
        (zqkv_ref, zba_ref, cw_ref, alog_ref, dtb_ref,
         a_ref, qk_ref, r_ref, qd_ref, kd_ref, gl_ref, xs_ref) = refs
    width = 3 * GDN_W
    part = pl.program_id(1)

    @pl.when(part == 0)
    def _():
        xs_ref[0:HALO, :] = jnp.zeros((HALO, width), F32)
        xs_ref[HALO + seq:HALO + seq + HALO, :] = jnp.zeros((HALO, width), F32)
        xs_ref[HALO:HALO + seq, :] = zqkv_ref[...]

    ri = lax.broadcasted_iota(jnp.int32, (CHUNK, CHUNK), 0)
    ci = lax.broadcasted_iota(jnp.int32, (CHUNK, CHUNK), 1)
    incl = (ri >= ci, ri <= ci)
    strict = (ri > ci, ri < ci)
    lane = lax.broadcasted_iota(jnp.int32, (CHUNK, LANE), 1)

    def body(i, carry):
        r0 = pl.multiple_of((part * PREP_CHUNKS + i) * CHUNK, CHUNK)
        rows = pl.ds(r0, CHUNK)
        win = xs_ref[pl.ds(r0, CHUNK + 2 * HALO), :]
        acc = win[HALO:HALO + CHUNK, :] * cw_ref[CONV_K // 2:CONV_K // 2 + 1, :]
        for j in range(CONV_K):
            if j != CONV_K // 2:
                lo = HALO + j - CONV_K // 2
                acc = acc + win[lo:lo + CHUNK, :] * cw_ref[j:j + 1, :]
        act = _silu(acc)

        zba = zba_ref[rows, :]
        beta_all = 1.0 / (1.0 + jnp.exp(-zba))
        xa = zba + dtb_ref[...]
        softplus = jnp.maximum(xa, 0.0) + jnp.log(1.0 + jnp.exp(-jnp.abs(xa)))
        bg = jnp.where(lane < LOG_COL, beta_all, -jnp.exp(alog_ref[...]) * softplus)

        slabs = [act[:, s * GDN_DK:(s + 1) * GDN_DK] for s in range(2 * GDN_HEADS)]
        ssq = [jnp.sum(x * x, axis=-1, keepdims=True) for x in slabs]
        slabs = [x * lax.rsqrt(s + EPS) for x, s in zip(slabs, ssq)]
        if axial:
            cos, s_up, s_dn = cos_ref[rows, :], sup_ref[rows, :], sdn_ref[rows, :]
            up = [pltpu.roll(x, GDN_DK - GDN_DK // 4, 1) for x in slabs]
            dn = [pltpu.roll(x, GDN_DK // 4, 1) for x in slabs]
            slabs = [x * cos + u * s_up + w * s_dn for x, u, w in zip(slabs, up, dn)]
        qs = [x * (GDN_DK ** -0.5) for x in slabs[:GDN_HEADS]]
        ks = slabs[GDN_HEADS:]
        kbfs = [k.astype(BF16) for k in ks]
        kks = [_dot_nt(kb_, kb_) for kb_ in kbfs]
        qks = [_dot_nt(q.astype(BF16), kb_) for q, kb_ in zip(qs, kbfs)]
        heads = [(qs[hd], ks[hd], act[:, 2 * GDN_W + hd * GDN_DK:2 * GDN_W + (hd + 1) * GDN_DK], kks[hd], qks[hd])
                 for hd in range(GDN_HEADS)]

        gcums = [_dot_exact_lhs(incl[d].astype(BF16), bg) for d in range(2)]
        gcum_ts = [g.T for g in gcums]
        for d in range(2):
            gcum, gcum_t = gcums[d], gcum_ts[d]
            last = gcum[0:1, :] if d else gcum[CHUNK - 1:CHUNK, :]
            e_in = jnp.exp(gcum)
            e_out = jnp.exp(last - gcum)
            gl_ref[i, d:d + 1, :] = jnp.exp(last)
            for hd in range(GDN_HEADS):
                q, k, v, kk, qk = heads[hd]
                cb = d * GDN_HEADS + hd
                cg = LOG_COL + cb
                beta = bg[:, cb:cb + 1]
                eg = e_in[:, cg:cg + 1]
                el = e_out[:, cg:cg + 1]
                diff = gcum[:, cg:cg + 1] - gcum_t[cg:cg + 1, :]
                decay = jnp.where(incl[d], jnp.exp(jnp.where(incl[d], diff, 0.0)), 0.0)
                a_ref[d, i, hd] = jnp.where(strict[d], beta * kk * decay, 0.0)
                qk_ref[d, i, hd] = (qk * decay).astype(BF16)
                kb = k * beta
                r_ref[d, i, hd, :, 0:GDN_DK] = (v * beta).astype(BF16)
                r_ref[d, i, hd, :, GDN_DK:2 * GDN_DK] = (kb * eg).astype(BF16)
                qd_ref[d, i, hd] = (q * eg).astype(BF16)
                kd_ref[d, i, hd] = (k * el).astype(BF16)
        return carry

    lax.fori_loop(0, PREP_CHUNKS, body, 0)


def _gdn_prep(axial, zqkv, zba, conv_w, a_log, dt_bias, rope):
    seq = DEC_SEQ if axial else SEQ
    nseq = DEC_BATCH if axial else BATCH
    nparts = seq // (PREP_CHUNKS * CHUNK)
    nchunk = nseq * seq // CHUNK
    blk0 = N_CTX // DEC_SEQ if axial else 0
    tok = lambda w: pl.BlockSpec((seq, w), lambda b, p: (blk0 + b, 0))
    full = lambda a: pl.BlockSpec(a.shape, lambda b, p: (0,) * a.ndim)
    ins = [zqkv, zba, conv_w, a_log, dt_bias]
    specs = [tok(3 * GDN_W), tok(LANE), full(conv_w), full(a_log), full(dt_bias)]
    if axial:
        ins += list(rope)
        specs += [full(t) for t in rope]
    tile = lambda w: pl.BlockSpec((2, PREP_CHUNKS, GDN_HEADS, CHUNK, w), lambda b, p: (0, b * nparts + p, 0, 0, 0))
    shape = lambda w, dt: jax.ShapeDtypeStruct((2, nchunk, GDN_HEADS, CHUNK, w), dt)
    return pl.pallas_call(
        functools.partial(_gdn_prep_kernel, axial, seq),
        out_shape=[shape(CHUNK, F32), shape(CHUNK, BF16), shape(2 * GDN_DK, BF16),
                   shape(GDN_DK, BF16), shape(GDN_DK, BF16),
                   jax.ShapeDtypeStruct((nchunk, 2, LANE), F32)],
        grid=(nseq, nparts),
        in_specs=specs,
        out_specs=[tile(CHUNK), tile(CHUNK), tile(2 * GDN_DK), tile(GDN_DK), tile(GDN_DK),
                   pl.BlockSpec((PREP_CHUNKS, 2, LANE), lambda b, p: (b * nparts + p, 0, 0))],
        scratch_shapes=[pltpu.VMEM((seq + 2 * HALO, 3 * GDN_W), F32)],
        compiler_params=_params(("parallel", "arbitrary")),
        name="gdn_prep_lat" if axial else "gdn_prep_ctx",
    )(*ins)


TRI_BLK = 8
TRI_ELEMS = CHUNK * CHUNK


def _tri_inverse_kernel(groups_per_dir, a_ref, o_ref, at_ref, tt_ref):
    backward = pl.program_id(0) >= groups_per_dir
    for blk in range(TRI_ELEMS // LANE):
        cols = slice(blk * LANE, (blk + 1) * LANE)
        at_ref[cols, :] = a_ref[:, cols].T
    tt_ref[...] = jnp.zeros(tt_ref.shape, F32)
    nblk = CHUNK // TRI_BLK

    def substitute(mirror):
        phys = (lambda idx: CHUNK - 1 - idx) if mirror else (lambda idx: idx)
        for ib in range(nblk):
            c_lo, width = ((nblk - 1 - ib) * TRI_BLK if mirror else 0), (ib + 1) * TRI_BLK

            def row_body(ii, carry, ib=ib, c_lo=c_lo, width=width):
                i = phys(ib * TRI_BLK + ii)
                cidx = lax.broadcasted_iota(jnp.int32, (width, LANE), 0) + c_lo
                acc = (cidx == i).astype(F32)
                for jb in range(ib + 1):
                    w_j = (jb + 1) * TRI_BLK
                    cj_lo = (nblk - 1 - jb) * TRI_BLK if mirror else 0
                    sub = slice(cj_lo - c_lo, cj_lo - c_lo + w_j)
                    part = acc[sub]
                    for jl in range(jb * TRI_BLK, (jb + 1) * TRI_BLK):
                        j = phys(jl)
                        arow = at_ref[pl.ds(i * CHUNK + j, 1), :]
                        part = part - arow * tt_ref[j * CHUNK + cj_lo:j * CHUNK + cj_lo + w_j, :]
                    acc = part if w_j == width else (
                        jnp.concatenate([part, acc[w_j:]], axis=0) if not mirror
                        else jnp.concatenate([acc[:width - w_j], part], axis=0))
                tt_ref[pl.ds(pl.multiple_of(i * CHUNK + c_lo, TRI_BLK), width), :] = acc
                return carry

            lax.fori_loop(0, TRI_BLK, row_body, 0)

    @pl.when(jnp.logical_not(backward))
    def _():
        substitute(False)

    @pl.when(backward)
    def _():
        substitute(True)

    for blk in range(TRI_ELEMS // LANE):
        cols = slice(blk * LANE, (blk + 1) * LANE)
        o_ref[:, cols] = tt_ref[cols, :].T.astype(BF16)


def _tri_inverse(a):
    nprob = a.shape[1] * a.shape[2]
    groups_per_dir = nprob // LANE
    blk = pl.BlockSpec((LANE, TRI_ELEMS), lambda g: (g, 0))
    out = pl.pallas_call(
        functools.partial(_tri_inverse_kernel, groups_per_dir),
        out_shape=jax.ShapeDtypeStruct((2 * nprob, TRI_ELEMS), BF16),
        grid=(2 * groups_per_dir,),
        in_specs=[blk],
        out_specs=blk,
        scratch_shapes=[pltpu.VMEM((TRI_ELEMS, LANE), F32), pltpu.VMEM((TRI_ELEMS, LANE), F32)],
        compiler_params=_params(("parallel",)),
        name="tri_inverse",
    )(a.reshape(2 * nprob, TRI_ELEMS))
    return out.reshape(a.shape)


def _gdn_scan_kernel(nchunk, t_ref, qk_ref, r_ref, qd_ref, kd_ref, gl_ref, s0_ref, *rest):
    o_ref, sfin_ref, state_ref = rest[-3:]
    state_ref[...] = s0_ref[...]
    streams = [(d, hd) for d in range(2) for hd in range(GDN_HEADS)]

    def body(c, carry):
        cc = (c, nchunk - 1 - c)
        uws = [_dot(t_ref[d, cc[d], hd], r_ref[d, cc[d], hd]).astype(BF16) for d, hd in streams]
        mbs = [_dot_tn(kd_ref[d, cc[d], hd], uw) for (d, hd), uw in zip(streams, uws)]
        qos = [_dot(qk_ref[d, cc[d], hd], uw) for (d, hd), uw in zip(streams, uws)]
        sts = [state_ref[d * GDN_HEADS + hd] for d, hd in streams]
        sbs = [s.astype(BF16) for s in sts]
        for (d, hd), qo, sb in zip(streams, qos, sbs):
            rows = pl.ds(pl.multiple_of(cc[d] * CHUNK, CHUNK), CHUNK)
            qp = (qd_ref[d, cc[d], hd].astype(F32) - qo[:, GDN_DK:]).astype(BF16)
            o_ref[d, rows, hd * GDN_DK:(hd + 1) * GDN_DK] = _dot(qp, sb) + qo[:, :GDN_DK]
        for (d, hd), mb, s, sb in zip(streams, mbs, sts, sbs):
            sidx = d * GDN_HEADS + hd
            gl = gl_ref[cc[d]][d:d + 1, LOG_COL + sidx:LOG_COL + sidx + 1]
            state_ref[sidx] = s * gl - _dot(mb[:, GDN_DK:].astype(BF16), sb) + mb[:, :GDN_DK]
        return carry

    lax.fori_loop(0, nchunk, body, 0)
    sfin_ref[...] = state_ref[...]


def _gdn_scan(seq, nseq, tinv, qk, r, qd, kd, gl, s0, o_all=None):
    nc = seq // CHUNK
    blk0 = 0 if o_all is None else N_CTX // seq
    tile = lambda w: pl.BlockSpec((2, nc, GDN_HEADS, CHUNK, w), lambda b: (0, b, 0, 0, 0))
    sblk = pl.BlockSpec((None, N_STREAM, GDN_DK, GDN_DK), lambda b: (b, 0, 0, 0))
    oblk = pl.BlockSpec((2, seq, GDN_W), lambda b: (0, blk0 + b, 0))
    ins = [tinv, qk, r, qd, kd, gl, s0]
    specs = [tile(CHUNK), tile(CHUNK), tile(2 * GDN_DK), tile(GDN_DK), tile(GDN_DK),
             pl.BlockSpec((nc, 2, LANE), lambda b: (b, 0, 0)), sblk]
    aliases = {}
    if o_all is not None:
        ins.append(o_all)
        specs.append(pl.BlockSpec(memory_space=pl.ANY))
        aliases = {len(ins) - 1: 0}
    return pl.pallas_call(
        functools.partial(_gdn_scan_kernel, nc),
        out_shape=[jax.ShapeDtypeStruct((2, N_TOK, GDN_W), F32),
                   jax.ShapeDtypeStruct((nseq, N_STREAM, GDN_DK, GDN_DK), F32)],
        grid=(nseq,),
        in_specs=specs,
        out_specs=[oblk, sblk],
        scratch_shapes=[pltpu.VMEM((N_STREAM, GDN_DK, GDN_DK), F32)],
        input_output_aliases=aliases,
        compiler_params=_params(("parallel",)),
        name="gdn_scan_lat" if seq == DEC_SEQ else "gdn_scan_ctx",
    )(*ins)


def _first_max(vals):
    sel = []
    taken = None
    for a, va in enumerate(vals):
        is_max = None
        for b, vb in enumerate(vals):
            if a == b:
                continue
            c = va >= vb
            is_max = c if is_max is None else (is_max & c)
        if taken is not None:
            is_max = is_max & jnp.logical_not(taken)
        sel.append(is_max)
        taken = is_max if taken is None else (taken | is_max)
    return sel


PAIR_SLOTS = ((0, 1), (0, 2), (0, 3), (1, 3), (1, 2), (3, 2))
N_CLASS = N_GROUPS * len(PAIR_SLOTS)
CLASS_EXPERTS = tuple((g * EXP_PER_GROUP + a, g * EXP_PER_GROUP + b)
                      for g in range(N_GROUPS) for a, b in PAIR_SLOTS)


def _route_rows(lt):
    gl = [lt[a:a + 1, :] for a in range(N_GROUPS)]
    gsel = _first_max(gl)
    gmax = functools.reduce(jnp.maximum, gl)
    gden = functools.reduce(lambda x, y: x + y, [jnp.exp(x - gmax) for x in gl])
    g_w = 1.0 / gden
    el = []
    for kx in range(EXP_PER_GROUP):
        acc = None
        for a in range(N_GROUPS):
            row = N_GROUPS + a * EXP_PER_GROUP + kx
            term = jnp.where(gsel[a], lt[row:row + 1, :], 0.0)
            acc = term if acc is None else acc + term
        el.append(acc)
    emax = functools.reduce(jnp.maximum, el)
    ee = [jnp.exp(x - emax) for x in el]
    eden = functools.reduce(lambda x, y: x + y, ee)
    ep = [x / eden for x in ee]
    top1 = _first_max(ep)
    ep2 = [jnp.where(top1[kx], -1.0, ep[kx]) for kx in range(EXP_PER_GROUP)]
    top2 = _first_max(ep2)
    chosen = [top1[kx] | top2[kx] for kx in range(EXP_PER_GROUP)]
    wsum = functools.reduce(lambda x, y: x + y,
                            [jnp.where(chosen[kx], ep[kx], 0.0) for kx in range(EXP_PER_GROUP)])
    within = [jnp.where(chosen[kx], ep[kx] / wsum, 0.0) for kx in range(EXP_PER_GROUP)]
    cls = jnp.zeros(g_w.shape, jnp.int32)
    w_a = jnp.zeros(g_w.shape, F32)
    w_b = jnp.zeros(g_w.shape, F32)
    for g in range(N_GROUPS):
        for kx, (a, b) in enumerate(PAIR_SLOTS):
            hit = gsel[g] & chosen[a] & chosen[b]
            cls = jnp.where(hit, g * len(PAIR_SLOTS) + kx, cls)
            w_a = jnp.where(hit, g_w * within[a], w_a)
            w_b = jnp.where(hit, g_w * within[b], w_b)
    return cls, w_a, w_b


H2W = D_MODEL + LANE
COL_WA = D_MODEL
COL_WB = D_MODEL + 1


def _outproj_kernel(nx, idx_ref, *refs):
    x = _token_tile(nx, refs)
    (ona_ref, og_ref, zg_ref, mod_ref, wo_ref, gng_ref, n2g_ref, wr_ref, br_ref,
     x1_ref, h2x_ref, cls_ref) = refs[nx:]
    del idx_ref
    m = mod_ref[...]
    og = og_ref[0] + og_ref[1]
    zg = zg_ref[...]
    parts = []
    for hd in range(GDN_HEADS):
        cols = slice(hd * GDN_DK, (hd + 1) * GDN_DK)
        oh = og[:, cols]
        oh = oh * lax.rsqrt(jnp.mean(oh * oh, axis=-1, keepdims=True) + EPS) * gng_ref[...]
        parts.append((oh * _silu(zg[:, cols])).astype(BF16))
    mix = _dot(ona_ref[...].astype(BF16), wo_ref[0:NA_W, :])
    for hd in range(GDN_HEADS):
        r0 = NA_W + hd * GDN_DK
        mix = mix + _dot(parts[hd], wo_ref[r0:r0 + GDN_DK, :])
    x1 = x + m[2:3, :] * mix
    x1_ref[...] = x1
    y = x1 * lax.rsqrt(jnp.mean(x1 * x1, axis=-1, keepdims=True) + EPS) * n2g_ref[...]
    h2 = y * (1.0 + m[4:5, :]) + m[3:4, :]
    h2x_ref[:, 0:D_MODEL] = h2.astype(BF16).astype(F32)
    logits = _dot3(h2, wr_ref[...]) + br_ref[...]
    cls, w_a, w_b = _route_rows(logits.T)
    rid = lax.broadcasted_iota(jnp.int32, (8, TM), 0)
    wt = jnp.where(rid == COL_WA - D_MODEL, w_a, jnp.where(rid == COL_WB - D_MODEL, w_b, 0.0))
    wt = jnp.concatenate([wt, jnp.zeros((LANE - 8, TM), F32)], axis=0)
    h2x_ref[:, D_MODEL:H2W] = wt.T
    cls_ref[...] = jnp.broadcast_to(cls, (8, TM))


def _outproj(tile_mod, x, o_na, o_gdn, zg, mod, w_out, gng, n2g, w_r, b_r):
    tok = lambda w: pl.BlockSpec((TM, w), lambda t, idx: (t, 0))
    full = lambda a: pl.BlockSpec(a.shape, lambda t, idx: (0,) * a.ndim)
    xs = list(x) if isinstance(x, tuple) else [x]
    return pl.pallas_call(
        functools.partial(_outproj_kernel, len(xs)),
        out_shape=[jax.ShapeDtypeStruct((N_TOK, D_MODEL), F32),
                   jax.ShapeDtypeStruct((N_TOK, H2W), F32),
                   jax.ShapeDtypeStruct((N_TOK // TM, 8, TM), jnp.int32)],
        grid_spec=pltpu.PrefetchScalarGridSpec(
            num_scalar_prefetch=1,
            grid=(N_TOK // TM,),
            in_specs=_token_specs(x) + [
                tok(NA_W),
                pl.BlockSpec((2, TM, GDN_W), lambda t, idx: (0, t, 0)), tok(GDN_W),
                pl.BlockSpec((None, 6, D_MODEL), lambda t, idx: (idx[t], 0, 0)),
                full(w_out), full(gng), full(n2g), full(w_r), full(b_r),
            ],
            out_specs=[tok(D_MODEL), tok(H2W), pl.BlockSpec((None, 8, TM), lambda t, idx: (t, 0, 0))],
        ),
        compiler_params=_params(("parallel",)),
        name="outproj",
    )(tile_mod, *xs, o_na, o_gdn, zg, mod, w_out, gng, n2g, w_r, b_r)


TM_X = 256
N_XTILE = N_TOK // TM_X + N_CLASS
N_SLOT = N_XTILE * TM_X
CLS_ROWS = N_TOK // LANE
TAB_EA, TAB_EB, TAB_USED, TAB_LAST = 0, 1, 2, 3


def _route_pos_kernel(cls_ref, slot_ref, tab_ref):
    cls = cls_ref[...]
    li = lax.broadcasted_iota(jnp.int32, (LANE, LANE), 0)
    lj = lax.broadcasted_iota(jnp.int32, (LANE, LANE), 1)
    before_lane = (li < lj).astype(BF16)
    ri = lax.broadcasted_iota(jnp.int32, (CLS_ROWS, CLS_ROWS), 0)
    rj = lax.broadcasted_iota(jnp.int32, (CLS_ROWS, CLS_ROWS), 1)
    before_row = (rj < ri).astype(BF16)
    lane = lax.broadcasted_iota(jnp.int32, (1, LANE), 1)
    tile_start = lane * TM_X
    off = jnp.zeros((1, 1), jnp.int32)
    slot = jnp.zeros(cls.shape, jnp.int32)
    tab_a = jnp.zeros((1, LANE), jnp.int32)
    tab_b = jnp.zeros((1, LANE), jnp.int32)
    tab_last = jnp.full((1, LANE), -1, jnp.int32)
    for c in range(N_CLASS):
        hit = cls == c
        one = hit.astype(F32)
        in_row = _dot(one.astype(BF16), before_lane)
        row_tot = jnp.broadcast_to(jnp.sum(one, axis=1, keepdims=True), one.shape)
        rank = in_row + _dot(before_row, row_tot.astype(BF16))
        count = jnp.sum(row_tot[:, 0:1], axis=0, keepdims=True).astype(jnp.int32)
        slot = jnp.where(hit, off + rank.astype(jnp.int32), slot)
        nxt = off + (((count + (TM_X - 1)) >> 8) << 8)
        mine = (tile_start >= off) & (tile_start < nxt)
        tab_a = jnp.where(mine, CLASS_EXPERTS[c][0], tab_a)
        tab_b = jnp.where(mine, CLASS_EXPERTS[c][1], tab_b)
        tab_last = jnp.where((lane == c) & (nxt > off), (nxt >> 8) - 1, tab_last)
        off = nxt
    slot_ref[...] = slot
    row = lax.broadcasted_iota(jnp.int32, (8, LANE), 0)
    used = jnp.broadcast_to(off >> 8, (8, LANE))
    tab_ref[...] = jnp.where(row == TAB_EA, tab_a, jnp.where(row == TAB_EB, tab_b,
                                                               jnp.where(row == TAB_LAST, tab_last, used)))


def _route_positions(cls):
    assert TM_X == 256
    return pl.pallas_call(
        _route_pos_kernel,
        out_shape=[jax.ShapeDtypeStruct((CLS_ROWS, LANE), jnp.int32),
                   jax.ShapeDtypeStruct((8, LANE), jnp.int32)],
        name="route_positions",
    )(cls)


ROW_UNROLL = 8


def _permute_kernel(slot_ref, tab_ref, h_ref, xs_ref, buf_ref, zero_ref, sem_ref, zsem_ref):
    t = pl.program_id(0)
    cur = t % 2

    @pl.when(t == 0)
    def _():
        zero_ref[...] = jnp.zeros(zero_ref.shape, F32)

        def tile_copy(c):
            return pltpu.make_async_copy(
                zero_ref, xs_ref.at[pl.ds(pl.multiple_of(tab_ref[TAB_LAST, c] * TM_X, TM_X), TM_X), :], zsem_ref.at[0])

        for c in range(N_CLASS):
            @pl.when(tab_ref[TAB_LAST, c] >= 0)
            def _():
                tile_copy(c).start()
        for c in range(N_CLASS):
            @pl.when(tab_ref[TAB_LAST, c] >= 0)
            def _():
                tile_copy(c).wait()

    buf_ref[cur] = h_ref[...]

    def issue(i, carry):
        _row_copy(buf_ref.at[cur], i, xs_ref, slot_ref[t * TM + i], sem_ref.at[cur]).start()
        return carry

    lax.fori_loop(0, TM, issue, 0, unroll=ROW_UNROLL)

    def drain(which):
        def wait(i, carry):
            _row_copy(buf_ref.at[which], 0, xs_ref, 0, sem_ref.at[which]).wait()
            return carry
        lax.fori_loop(0, TM, wait, 0, unroll=ROW_UNROLL)

    @pl.when(t > 0)
    def _():
        drain(1 - cur)

    @pl.when(t == pl.num_programs(0) - 1)
    def _():
        drain(cur)


def _permute(slot, tab, h2x):
    assert TM_X == TM
    return pl.pallas_call(
        _permute_kernel,
        out_shape=jax.ShapeDtypeStruct((N_SLOT, H2W), F32),
        grid_spec=pltpu.PrefetchScalarGridSpec(
            num_scalar_prefetch=2,
            grid=(N_TOK // TM,),
            in_specs=[pl.BlockSpec((TM, H2W), lambda t, s, tab: (t, 0))],
            out_specs=pl.BlockSpec(memory_space=pl.ANY),
            scratch_shapes=[pltpu.VMEM((2, TM, H2W), F32), pltpu.VMEM((TM_X, H2W), F32),
                            pltpu.SemaphoreType.DMA((2,)), pltpu.SemaphoreType.DMA((1,))],
        ),
        compiler_params=_params(("arbitrary",)),
        name="moe_permute",
    )(slot, tab, h2x)


def _experts_kernel(tab_ref, xs_ref, w1a_ref, w3a_ref, w2a_ref, w1b_ref, w3b_ref, w2b_ref, ys_ref,
                    c1a_ref, c3a_ref, c2a_ref, c1b_ref, c3b_ref, c2b_ref):
    t = pl.program_id(0)
    prev = jnp.maximum(t - 1, 0)

    @pl.when(t < tab_ref[TAB_USED, 0])
    def _():
        @pl.when((t == 0) | (tab_ref[TAB_EA, t] != tab_ref[TAB_EA, prev]))
        def _():
            c1a_ref[...] = w1a_ref[...].astype(BF16)
            c3a_ref[...] = w3a_ref[...].astype(BF16)
            c2a_ref[...] = w2a_ref[...].astype(BF16)

        @pl.when((t == 0) | (tab_ref[TAB_EB, t] != tab_ref[TAB_EB, prev]))
        def _():
            c1b_ref[...] = w1b_ref[...].astype(BF16)
            c3b_ref[...] = w3b_ref[...].astype(BF16)
            c2b_ref[...] = w2b_ref[...].astype(BF16)

        x = xs_ref[:, 0:D_MODEL].astype(BF16)
        gates = xs_ref[:, D_MODEL:H2W]
        w_a = gates[:, COL_WA - D_MODEL:COL_WA - D_MODEL + 1]
        w_b = gates[:, COL_WB - D_MODEL:COL_WB - D_MODEL + 1]
        a1 = _dot(x, c1a_ref[...])
        a3 = _dot(x, c3a_ref[...])
        b1 = _dot(x, c1b_ref[...])
        b3 = _dot(x, c3b_ref[...])
        sa = (_silu(a1) * a3 * w_a).astype(BF16)
        sb = (_silu(b1) * b3 * w_b).astype(BF16)
        ys_ref[...] = _dot(sa, c2a_ref[...]) + _dot(sb, c2b_ref[...])


def _experts(layer, tab, xs, w1, w3, w2):
    last = lambda t, tab: jnp.minimum(t, tab[TAB_USED, 0] - 1)
    tile = lambda w: pl.BlockSpec((TM_X, w), lambda t, tab: (last(t, tab), 0))
    wspec = lambda shape, row: pl.BlockSpec((None, None) + shape,
                                            lambda t, tab: (layer, tab[row, last(t, tab)], 0, 0))
    up, down = (D_MODEL, EXPERT_FF), (EXPERT_FF, D_MODEL)
    return pl.pallas_call(
        _experts_kernel,
        out_shape=jax.ShapeDtypeStruct((N_SLOT, D_MODEL), F32),
        grid_spec=pltpu.PrefetchScalarGridSpec(
            num_scalar_prefetch=1,
            grid=(N_XTILE,),
            in_specs=[tile(H2W),
                      wspec(up, TAB_EA), wspec(up, TAB_EA), wspec(down, TAB_EA),
                      wspec(up, TAB_EB), wspec(up, TAB_EB), wspec(down, TAB_EB)],
            out_specs=tile(D_MODEL),
            scratch_shapes=[pltpu.VMEM(up, BF16), pltpu.VMEM(up, BF16), pltpu.VMEM(down, BF16),
                            pltpu.VMEM(up, BF16), pltpu.VMEM(up, BF16), pltpu.VMEM(down, BF16)],
        ),
        compiler_params=_params(("arbitrary",)),
        name="moe_experts",
    )(tab, xs, w1, w3, w2, w1, w3, w2)


def _unpermute_kernel(idx_ref, slot_ref, x1_ref, mod_ref, ys_ref, *rest):
    outs, (buf_ref, sem_ref) = rest[:-2], rest[-2:]
    del idx_ref
    t = pl.program_id(0)
    cur = t % 2

    def fetch(tile, which):
        def issue(i, carry):
            _row_copy(ys_ref, slot_ref[tile * TM + i], buf_ref.at[which], i, sem_ref.at[which]).start()
            return carry
        lax.fori_loop(0, TM, issue, 0, unroll=ROW_UNROLL)

    @pl.when(t == 0)
    def _():
        fetch(0, 0)

    @pl.when(t + 1 < pl.num_programs(0))
    def _():
        fetch(t + 1, 1 - cur)

    def wait(i, carry):
        _row_copy(ys_ref, 0, buf_ref.at[cur], 0, sem_ref.at[cur]).wait()
        return carry

    lax.fori_loop(0, TM, wait, 0, unroll=ROW_UNROLL)
    y = x1_ref[...] + mod_ref[5:6, :] * buf_ref[cur]

    @pl.when(t < N_CTX // TM)
    def _():
        outs[0][...] = y

    @pl.when(t >= N_CTX // TM)
    def _():
        outs[1][...] = y


def _unpermute(tile_mod, slot, x1, mod, ys):
    tok = pl.BlockSpec((TM, D_MODEL), lambda t, idx, s: (t, 0))
    out_shape = [jax.ShapeDtypeStruct((N_CTX, D_MODEL), F32), jax.ShapeDtypeStruct((N_LAT, D_MODEL), F32)]
    out_specs = _token_specs((None, None))
    return pl.pallas_call(
        _unpermute_kernel,
        out_shape=out_shape,
        grid_spec=pltpu.PrefetchScalarGridSpec(
            num_scalar_prefetch=2,
            grid=(N_TOK // TM,),
            in_specs=[tok, pl.BlockSpec((None, 6, D_MODEL), lambda t, idx, s: (idx[t], 0, 0)),
                      pl.BlockSpec(memory_space=pl.ANY)],
            out_specs=out_specs,
            scratch_shapes=[pltpu.VMEM((2, TM, D_MODEL), F32), pltpu.SemaphoreType.DMA((2,))],
        ),
        compiler_params=_params(("arbitrary",)),
        name="moe_unpermute",
    )(tile_mod, slot, x1, mod, ys)


def _tile_mod_index(tile):
    t = np.arange(N_TOK // tile) * tile
    return jnp.asarray(np.where(t < N_CTX, 0, 1 + (t - N_CTX) // DEC_SEQ), jnp.int32)


def kernel(x_prompt, x_sample, c, cache_k, cache_v, state_ssm, c_ctx, ada_w, ada_b, norm1_g, norm2_g, w_in, w_out, na_qn_g, na_kn_g, na_rpb, gdn_conv_w, gdn_a_log, gdn_dt_bias, gdn_norm_g, moe_w_rg, moe_b_rg, moe_w_re, moe_b_re, moe_w1, moe_w3, moe_w2):
    x = (x_prompt.reshape(N_CTX, D_MODEL), x_sample.reshape(N_LAT, D_MODEL))
    cv = jnp.concatenate([c_ctx[None, :], c, jnp.zeros((N_MOD_PAD - N_MOD, D_MODEL), F32)], axis=0)
    mod_all = _modulation(cv, ada_w, ada_b).reshape(DEPTH, N_MOD_PAD, 6, D_MODEL)

    idx_tm = _tile_mod_index(TM)
    hh = np.arange(NA_W) // NA_HD
    ones_bd = jnp.asarray(hh[:, None] == hh[None, :], BF16)
    rope = _rope_tables()
    cache_k4 = cache_k.reshape(DEC_BATCH, DEPTH, PAST_LEN, NA_W)
    cache_v4 = cache_v.reshape(DEC_BATCH, DEPTH, PAST_LEN, NA_W)
    lane_pad = lambda a, at: jnp.zeros((1, LANE), F32).at[0, at:at + a.size].set(a.reshape(-1))
    zeros_state = jnp.zeros((BATCH, 2 * GDN_HEADS, GDN_DK, GDN_DK), F32)

    caches, ss, pending = None, [], None
    for l in range(DEPTH):
        mod = mod_all[l]
        w_main = w_in[l, :, :N_MAIN].astype(BF16)
        w_ba = jnp.zeros((D_MODEL, LANE), BF16).at[:, :N_IN - N_MAIN].set(w_in[l, :, N_MAIN:].astype(BF16))
        qg = jnp.tile(na_qn_g[l], NA_HEADS)[None, :]
        kg = jnp.tile(na_kn_g[l], NA_HEADS)[None, :]
        proj = (mod, norm1_g[l][None, :], w_main, w_ba, ones_bd, qg, kg, caches)
        if pending is None:
            q, k, v, zqkv, zg, zba, *caches = _inproj(l, idx_tm, x, *proj)
        else:
            x, q, k, v, zqkv, zg, zba, *caches = _inproj_gather(l, idx_tm, *pending, *proj)

        bias = _na_bias_table(na_rpb[l])
        o_na = _na_attention(l, q, k, v, cache_k4, cache_v4, bias, _ctx_attention(q, k, v))

        conv_w = jnp.zeros((8, 3 * GDN_W), F32).at[:CONV_K].set(gdn_conv_w[l])
        a_row = lane_pad(gdn_a_log[l], 2 * GDN_HEADS)
        dt_row = lane_pad(gdn_dt_bias[l], 2 * GDN_HEADS)
        a_c, *ops_c = _gdn_prep(False, zqkv, zba, conv_w, a_row, dt_row, None)
        a_l, *ops_l = _gdn_prep(True, zqkv, zba, conv_w, a_row, dt_row, rope)
        og_c, s_ctx = _gdn_scan(SEQ, BATCH, _tri_inverse(a_c), *ops_c, zeros_state)
        s0_lat = state_ssm[:, l].reshape(DEC_BATCH, N_STREAM, GDN_DK, GDN_DK)
        o_gdn, _ = _gdn_scan(DEC_SEQ, DEC_BATCH, _tri_inverse(a_l), *ops_l, s0_lat, og_c)

        w_r = jnp.zeros((D_MODEL, LANE), F32).at[:, :N_GROUPS].set(moe_w_rg[l])
        w_r = w_r.at[:, N_GROUPS:N_GROUPS + N_EXPERTS].set(moe_w_re[l])
        b_r = lane_pad(jnp.concatenate([moe_b_rg[l], moe_b_re[l]]), 0)
        x1, h2x, cls = _outproj(idx_tm, x, o_na, o_gdn, zg, mod, w_out[l].astype(BF16),
                                gdn_norm_g[l][None, :], norm2_g[l][None, :], w_r, b_r)
        slot, tab = _route_positions(cls[:, 0, :].reshape(CLS_ROWS, LANE))
        slot = slot.reshape(N_TOK)
        ys = _experts(l, tab, _permute(slot, tab, h2x), moe_w1, moe_w3, moe_w2)
        if l == DEPTH - 1:
            x = _unpermute(idx_tm, slot, x1, mod, ys)
        else:
            pending = (slot, x1, mod, ys)

        ss.append(s_ctx.reshape(BATCH, 2, GDN_HEADS, GDN_DK, GDN_DK))

    y_prompt = x[0].reshape(BATCH, SEQ, D_MODEL)
    y_sample = x[1].reshape(DEC_BATCH, DEC_SEQ, D_MODEL)
    new_k, new_v = (a.reshape(BATCH, DEPTH, SEQ, NA_HEADS, NA_HD) for a in caches)
    return (y_prompt, y_sample, new_k, new_v, jnp.stack(ss, axis=1))
```

```python
import functools

import jax
import jax.numpy as jnp
import numpy as np
from jax import lax
from jax.experimental import pallas as pl
from jax.experimental.pallas import tpu as pltpu

F32 = jnp.float32
BF16 = jnp.bfloat16

D_MODEL = 1024
BATCH = 16
SEQ = 256
DEPTH = 2
DEC_BATCH = 8
DEC_SEQ = 1024
PAST_LEN = 512
GRID_W = 64
GRID_ROWS = DEC_SEQ // GRID_W
NA_HEADS = 8
NA_HD = 64
NA_W = NA_HEADS * NA_HD
NA_KH = 8
NA_KW = 16
GDN_HEADS = 4
GDN_DK = 128
GDN_W = GDN_HEADS * GDN_DK
CONV_K = 5
CHUNK = 64
ROPE_BASE = 10000.0
N_GROUPS = 4
EXP_PER_GROUP = 4
N_EXPERTS = 16
EXPERT_FF = 512
EPS = 1e-6
N_IN = 3 * NA_W + 4 * GDN_W + 4 * GDN_HEADS

N_CTX = BATCH * SEQ
N_LAT = DEC_BATCH * DEC_SEQ
N_TOK = N_CTX + N_LAT
N_MOD = 1 + DEC_BATCH
N_MOD_PAD = 16
LANE = 128
N_MAIN = 3 * NA_W + 4 * GDN_W
TM = 256
VMEM_LIMIT = 56 * 1024 * 1024


def _dot(a, b):
    return jnp.dot(a, b, preferred_element_type=F32)


def _dot_nt(a, b):
    return lax.dot_general(a, b, (((1,), (1,)), ((), ())), preferred_element_type=F32)


def _dot_tn(a, b):
    return lax.dot_general(a, b, (((0,), (0,)), ((), ())), preferred_element_type=F32)


def _split2(x):
    hi = x.astype(BF16)
    lo = (x - hi.astype(F32)).astype(BF16)
    return hi, lo


def _split3(x):
    hi = x.astype(BF16)
    r = x - hi.astype(F32)
    mid = r.astype(BF16)
    lo = (r - mid.astype(F32)).astype(BF16)
    return hi, mid, lo


def _dot3(a, b):
    ah, al = _split2(a)
    bh, bl = _split2(b)
    return _dot(ah, bh) + (_dot(ah, bl) + _dot(al, bh))


def _dot_exact_lhs(mask_bf16, x):
    hi, mid, lo = _split3(x)
    return _dot(mask_bf16, hi) + (_dot(mask_bf16, mid) + _dot(mask_bf16, lo))


def _silu(x):
    return x * (1.0 / (1.0 + jnp.exp(-x)))


def _params(sem):
    return pltpu.CompilerParams(dimension_semantics=sem, vmem_limit_bytes=VMEM_LIMIT)


def _mod_kernel(cv_ref, w_ref, b_ref, o_ref):
    cv = cv_ref[...]
    s = _silu(cv)
    o_ref[...] = jnp.dot(s, w_ref[...], preferred_element_type=F32,
                         precision=lax.Precision.HIGHEST) + b_ref[...]


def _modulation(cv, ada_w, ada_b):
    nblk = 1024
    return pl.pallas_call(
        _mod_kernel,
        out_shape=jax.ShapeDtypeStruct((DEPTH, N_MOD_PAD, 6 * D_MODEL), F32),
        grid=(DEPTH, 6 * D_MODEL // nblk),
        in_specs=[
            pl.BlockSpec((N_MOD_PAD, D_MODEL), lambda l, j: (0, 0)),
            pl.BlockSpec((None, D_MODEL, nblk), lambda l, j: (l, 0, j)),
            pl.BlockSpec((None, 1, nblk), lambda l, j: (l, 0, j)),
        ],
        out_specs=pl.BlockSpec((None, N_MOD_PAD, nblk), lambda l, j: (l, 0, j)),
        compiler_params=_params(("parallel", "parallel")),
        name="modulation",
    )(cv, ada_w, ada_b.reshape(DEPTH, 1, 6 * D_MODEL))


def _token_specs(x):
    if not isinstance(x, tuple):
        return [pl.BlockSpec((TM, D_MODEL), lambda t, *_: (t, 0))]
    return [pl.BlockSpec((TM, D_MODEL), lambda t, *_: (jnp.minimum(t, N_CTX // TM - 1), 0)),
            pl.BlockSpec((TM, D_MODEL), lambda t, *_: (jnp.maximum(t - N_CTX // TM, 0), 0))]


def _token_tile(nx, refs):
    if nx == 1:
        return refs[0][...]
    return jnp.where(pl.program_id(0) < N_CTX // TM, refs[0][...], refs[1][...])


def _inproj_kernel(nx, idx_ref, *refs):
    del idx_ref
    _inproj_body(_token_tile(nx, refs), refs[nx:nx + 7], refs[-9:-1], refs[-1])


def _inproj_body(x, params, outs, wbf_ref):
    mod_ref, n1g_ref, win_ref, wba_ref, ones_ref, qg_ref, kg_ref = params
    q_ref, k_ref, v_ref, zqkv_ref, zg_ref, zba_ref, kcache_ref, vcache_ref = outs

    @pl.when(pl.program_id(0) == 0)
    def _():
        wbf_ref[...] = win_ref[...].astype(BF16)

    y = x * lax.rsqrt(jnp.mean(x * x, axis=-1, keepdims=True) + EPS) * n1g_ref[...]
    m = mod_ref[...]
    h = (y * (1.0 + m[1:2, :]) + m[0:1, :]).astype(BF16)
    z = _dot(h, wbf_ref[...])
    zba_ref[...] = _dot(h, wba_ref[...])
    ones = ones_ref[...]

    def head_rms(zz, g):
        hi, lo = _split2(zz * zz)
        ms = (_dot(hi, ones) + _dot(lo, ones)) * (1.0 / NA_HD)
        return zz * lax.rsqrt(ms + EPS) * g

    q_ref[...] = head_rms(z[:, 0:NA_W], qg_ref[...]) * (NA_HD ** -0.5)
    k = head_rms(z[:, NA_W:2 * NA_W], kg_ref[...])
    v = z[:, 2 * NA_W:3 * NA_W]
    k_ref[...] = k
    v_ref[...] = v
    zqkv_ref[...] = z[:, 3 * NA_W:3 * NA_W + 3 * GDN_W]
    zg_ref[...] = z[:, 3 * NA_W + 3 * GDN_W:N_MAIN]

    @pl.when(pl.program_id(0) < BATCH)
    def _():
        kcache_ref[...] = k
        vcache_ref[...] = v


def _win_spec(layer):
    return pl.BlockSpec((None, D_MODEL, N_MAIN), lambda t, *_: (layer, 0, 0), pipeline_mode=pl.Buffered(1))


WBF_SCRATCH = pltpu.VMEM((D_MODEL, N_MAIN), BF16)


def _inproj(layer, tile_mod, x, mod, n1g, w_in, w_ba, ones_bd, qg, kg, caches):
    assert TM == SEQ
    tok = lambda w: pl.BlockSpec((TM, w), lambda t, idx: (t, 0))
    full = lambda a: pl.BlockSpec(a.shape, lambda t, idx: (0,) * a.ndim)
    out_w = (NA_W, NA_W, NA_W, 3 * GDN_W, GDN_W, LANE)
    cache_blk = pl.BlockSpec((None, None, SEQ, NA_W), lambda t, idx: (jnp.minimum(t, BATCH - 1), layer, 0, 0))
    cache_shape = jax.ShapeDtypeStruct((BATCH, DEPTH, SEQ, NA_W), F32)
    xs = list(x) if isinstance(x, tuple) else [x]
    ins = [tile_mod, *xs, mod, n1g, w_in, w_ba, ones_bd, qg, kg]
    specs = _token_specs(x) + [pl.BlockSpec((None, 6, D_MODEL), lambda t, idx: (idx[t], 0, 0)),
                               full(n1g), _win_spec(layer), full(w_ba), full(ones_bd), full(qg), full(kg)]
    aliases = {}
    if caches is not None:
        aliases = {len(ins): len(out_w), len(ins) + 1: len(out_w) + 1}
        ins += list(caches)
        specs += [pl.BlockSpec(memory_space=pl.ANY)] * 2
    return pl.pallas_call(
        functools.partial(_inproj_kernel, len(xs)),
        out_shape=[jax.ShapeDtypeStruct((N_TOK, w), F32) for w in out_w] + [cache_shape, cache_shape],
        grid_spec=pltpu.PrefetchScalarGridSpec(
            num_scalar_prefetch=1,
            grid=(N_TOK // TM,),
            in_specs=specs,
            out_specs=[tok(w) for w in out_w] + [cache_blk, cache_blk],
            scratch_shapes=[WBF_SCRATCH],
        ),
        input_output_aliases=aliases,
        compiler_params=_params(("arbitrary",)),
        name="inproj",
    )(*ins)


def _row_copy(src, src_row, dst, dst_row, sem):
    return pltpu.make_async_copy(src.at[pl.ds(src_row, 1), :], dst.at[pl.ds(dst_row, 1), :], sem)


def _inproj_gather_kernel(idx_ref, slot_ref, x1_ref, modp_ref, ys_ref, *refs):
    del idx_ref
    params, outs, (buf_ref, sem_ref, wbf_ref) = refs[:7], refs[-12:-3], refs[-3:]
    t = pl.program_id(0)
    last = pl.num_programs(0) - 1
    cur = t % 2

    def fetch(tile, which):
        for i in range(TM):
            _row_copy(ys_ref, slot_ref[tile * TM + i], buf_ref.at[which], i, sem_ref.at[which]).start()

    def drain(which):
        for i in range(TM):
            _row_copy(ys_ref, 0, buf_ref.at[which], 0, sem_ref.at[which]).wait()

    @pl.when(t == 0)
    def _():
        fetch(0, 0)

    fetch(jnp.minimum(t + 1, last), 1 - cur)
    drain(cur)
    x = x1_ref[...] + modp_ref[5:6, :] * buf_ref[cur]
    outs[0][...] = x
    _inproj_body(x, params, outs[1:], wbf_ref)

    @pl.when(t == last)
    def _():
        drain(1 - cur)


def _inproj_gather(layer, tile_mod, slot, x1, mod_prev, ys, mod, n1g, w_in, w_ba, ones_bd, qg, kg, caches):
    tok = lambda w: pl.BlockSpec((TM, w), lambda t, idx, s: (t, 0))
    full = lambda a: pl.BlockSpec(a.shape, lambda t, idx, s: (0,) * a.ndim)
    modblk = pl.BlockSpec((None, 6, D_MODEL), lambda t, idx, s: (idx[t], 0, 0))
    out_w = (D_MODEL, NA_W, NA_W, NA_W, 3 * GDN_W, GDN_W, LANE)
    cache_blk = pl.BlockSpec((None, None, SEQ, NA_W), lambda t, idx, s: (jnp.minimum(t, BATCH - 1), layer, 0, 0))
    cache_shape = jax.ShapeDtypeStruct((BATCH, DEPTH, SEQ, NA_W), F32)
    ins = [tile_mod, slot, x1, mod_prev, ys, mod, n1g, w_in, w_ba, ones_bd, qg, kg, *caches]
    specs = [tok(D_MODEL), modblk, pl.BlockSpec(memory_space=pl.ANY), modblk,
             full(n1g), _win_spec(layer), full(w_ba), full(ones_bd), full(qg), full(kg)]
    specs += [pl.BlockSpec(memory_space=pl.ANY)] * 2
    return pl.pallas_call(
        _inproj_gather_kernel,
        out_shape=[jax.ShapeDtypeStruct((N_TOK, w), F32) for w in out_w] + [cache_shape, cache_shape],
        grid_spec=pltpu.PrefetchScalarGridSpec(
            num_scalar_prefetch=2,
            grid=(N_TOK // TM,),
            in_specs=specs,
            out_specs=[tok(w) for w in out_w] + [cache_blk, cache_blk],
            scratch_shapes=[pltpu.VMEM((2, TM, D_MODEL), F32), pltpu.SemaphoreType.DMA((2,)), WBF_SCRATCH],
        ),
        input_output_aliases={len(ins) - 2: len(out_w), len(ins) - 1: len(out_w) + 1},
        compiler_params=_params(("arbitrary",)),
        name="inproj_gather",
    )(*ins)


def _pair_masks(shape):
    lane = lax.broadcasted_iota(jnp.int32, shape, 1)
    return lane < NA_HD


def _ctx_attn_kernel(q_ref, k_ref, v_ref, o_ref):
    first = _pair_masks((SEQ, LANE))
    col = lambda p: slice(p * LANE, (p + 1) * LANE)
    heads = range(NA_HEADS)
    pairs = range(NA_HEADS // 2)
    kb = [k_ref[:, col(p)].astype(BF16) for p in pairs]
    vb = [v_ref[:, col(p)].astype(BF16) for p in pairs]
    qh = [jnp.where(first if h % 2 == 0 else jnp.logical_not(first), q_ref[:, col(h // 2)], 0.0).astype(BF16)
          for h in heads]
    s = [_dot_nt(qh[h], kb[h // 2]) for h in heads]
    e = [jnp.exp(s[h] - jnp.max(s[h], axis=-1, keepdims=True)) for h in heads]
    den = [jnp.sum(e[h], axis=-1, keepdims=True) for h in heads]
    o = [_dot(e[h].astype(BF16), vb[h // 2]) for h in heads]
    for p in pairs:
        o_ref[:, col(p)] = jnp.where(first, o[2 * p] / den[2 * p], o[2 * p + 1] / den[2 * p + 1])


def _ctx_attention(q, k, v):
    blk = pl.BlockSpec((SEQ, NA_W), lambda b: (b, 0))
    return pl.pallas_call(
        _ctx_attn_kernel,
        out_shape=jax.ShapeDtypeStruct((N_TOK, NA_W), F32),
        grid=(BATCH,),
        in_specs=[blk, blk, blk],
        out_specs=blk,
        compiler_params=_params(("parallel",)),
        name="ctx_attention",
    )(q, k, v)


NA_PAIR_GROUP = 4


def _na_kernel(q_ref, k_ref, v_ref, kc_ref, vc_ref, bias_ref, o_all_ref, o_ref):
    del o_all_ref
    r = pl.program_id(1)
    base = jnp.clip(r - NA_KH // 2, 0, GRID_ROWS - NA_KH)
    dr0 = base - r + (NA_KH - 1)
    row0 = pl.multiple_of(base * GRID_W, GRID_W)
    nwin = NA_KH * GRID_W
    first = _pair_masks((GRID_W, LANE))
    col = lambda p: slice(p * LANE, (p + 1) * LANE)
    for g0 in range(0, NA_HEADS // 2, NA_PAIR_GROUP):
        pairs = range(g0, g0 + NA_PAIR_GROUP)
        heads = range(2 * g0, 2 * (g0 + NA_PAIR_GROUP))
        qh = {h: jnp.where(first if h % 2 == 0 else jnp.logical_not(first), q_ref[:, col(h // 2)], 0.0).astype(BF16)
              for h in heads}
        kw = {p: k_ref[pl.ds(row0, nwin), col(p)].astype(BF16) for p in pairs}
        kc = {p: kc_ref[:, col(p)].astype(BF16) for p in pairs}
        vw = {p: v_ref[pl.ds(row0, nwin), col(p)].astype(BF16) for p in pairs}
        vc = {p: vc_ref[:, col(p)].astype(BF16) for p in pairs}
        s_loc = {h: _dot_nt(qh[h], kw[h // 2]) + bias_ref[h, dr0] for h in heads}
        s_ctx = {h: _dot_nt(qh[h], kc[h // 2]) for h in heads}
        mx = {h: jnp.maximum(jnp.max(s_loc[h], axis=-1, keepdims=True), jnp.max(s_ctx[h], axis=-1, keepdims=True))
              for h in heads}
        e_loc = {h: jnp.exp(s_loc[h] - mx[h]) for h in heads}
        e_ctx = {h: jnp.exp(s_ctx[h] - mx[h]) for h in heads}
        den = {h: jnp.sum(e_loc[h], axis=-1, keepdims=True) + jnp.sum(e_ctx[h], axis=-1, keepdims=True)
               for h in heads}
        o = {h: _dot(e_loc[h].astype(BF16), vw[h // 2]) + _dot(e_ctx[h].astype(BF16), vc[h // 2]) for h in heads}
        for p in pairs:
            o_ref[:, col(p)] = jnp.where(first, o[2 * p] / den[2 * p], o[2 * p + 1] / den[2 * p + 1])


def _na_attention(layer, q, k, v, cache_k, cache_v, bias, o_all):
    lat0 = N_CTX // DEC_SEQ
    qblk = pl.BlockSpec((GRID_W, NA_W), lambda b, r: (N_CTX // GRID_W + b * GRID_ROWS + r, 0))
    kvblk = pl.BlockSpec((DEC_SEQ, NA_W), lambda b, r: (lat0 + b, 0))
    cblk = pl.BlockSpec((None, None, PAST_LEN, NA_W), lambda b, r: (b, layer, 0, 0))
    return pl.pallas_call(
        _na_kernel,
        out_shape=jax.ShapeDtypeStruct((N_TOK, NA_W), F32),
        grid=(DEC_BATCH, GRID_ROWS),
        in_specs=[qblk, kvblk, kvblk, cblk, cblk,
                  pl.BlockSpec(bias.shape, lambda b, r: (0, 0, 0, 0)),
                  pl.BlockSpec(memory_space=pl.ANY)],
        out_specs=qblk,
        input_output_aliases={6: 0},
        compiler_params=_params(("parallel", "arbitrary")),
        name="na_attention",
    )(q, k, v, cache_k, cache_v, bias, o_all)


N_DC = 2 * NA_KW - 1


def _na_bias_kernel(r_ref, o_ref):
    shape = (GRID_W, NA_KH * GRID_W)
    qc = lax.broadcasted_iota(jnp.int32, shape, 0)
    kc = lax.broadcasted_iota(jnp.int32, shape, 1) & (GRID_W - 1)
    ws = jnp.clip(qc - NA_KW // 2, 0, GRID_W - NA_KW)
    valid = (kc >= ws) & (kc < ws + NA_KW)
    dc = jnp.where(valid, kc - qc + (NA_KW - 1), -1)
    for d0 in range(NA_KH):
        acc = jnp.full(shape, -jnp.inf, F32)
        for d in range(N_DC):
            acc = jnp.where(dc == d, r_ref[d0, d:d + 1, :], acc)
        o_ref[d0] = acc


def _na_bias_table(rpb):
    win = jnp.stack([rpb[:, d0:d0 + NA_KH, :] for d0 in range(NA_KH)], axis=1)
    rexp = jnp.repeat(win.transpose(0, 1, 3, 2), GRID_W, axis=-1)
    rexp = jnp.pad(rexp, ((0, 0), (0, 0), (0, 32 - N_DC), (0, 0)))
    blk = lambda rows: pl.BlockSpec((None, NA_KH, rows, NA_KH * GRID_W), lambda h: (h, 0, 0, 0))
    return pl.pallas_call(
        _na_bias_kernel,
        out_shape=jax.ShapeDtypeStruct((NA_HEADS, NA_KH, GRID_W, NA_KH * GRID_W), F32),
        grid=(NA_HEADS,),
        in_specs=[blk(32)],
        out_specs=blk(GRID_W),
        compiler_params=_params(("parallel",)),
        name="na_bias",
    )(rexp)


def _rope_tables():
    t = np.arange(DEC_SEQ)
    half = GDN_DK // 2
    inv_freq = (np.float32(ROPE_BASE) ** (-np.arange(0, half, 2, dtype=np.float32) / np.float32(half)))
    ang_r = (t // GRID_W).astype(np.float32)[:, None] * inv_freq
    ang_c = (t % GRID_W).astype(np.float32)[:, None] * inv_freq
    cr, sr, cc, sc = np.cos(ang_r), np.sin(ang_r), np.cos(ang_c), np.sin(ang_c)
    z = np.zeros_like(sr)
    cos = np.concatenate([cr, cr, cc, cc], axis=1)
    s_up = np.concatenate([-sr, z, -sc, z], axis=1)
    s_dn = np.concatenate([z, sr, z, sc], axis=1)
    return (jnp.asarray(cos, F32), jnp.asarray(s_up, F32), jnp.asarray(s_dn, F32))


HALO = 8
PREP_CHUNKS = 4
N_STREAM = 2 * GDN_HEADS
LOG_COL = N_STREAM


def _gdn_prep_kernel(axial, seq, *refs):
    if axial:
        (zqkv_ref, zba_ref, cw_ref, alog_ref, dtb_ref, cos_ref, sup_ref, sdn_ref,
         a_ref, qk_ref, r_ref, qd_ref, kd_ref, gl_ref, xs_ref) = refs
    else:
        (zqkv_ref, zba_ref, cw_ref, alog_ref, dtb_ref,
         a_ref, qk_ref, r_ref, qd_ref, kd_ref, gl_ref, xs_ref) = refs
    width = 3 * GDN_W
    part = pl.program_id(1)

    @pl.when(part == 0)
    def _():
        xs_ref[0:HALO, :] = jnp.zeros((HALO, width), F32)
        xs_ref[HALO + seq:HALO + seq + HALO, :] = jnp.zeros((HALO, width), F32)
        xs_ref[HALO:HALO + seq, :] = zqkv_ref[...]

    ri = lax.broadcasted_iota(jnp.int32, (CHUNK, CHUNK), 0)
    ci = lax.broadcasted_iota(jnp.int32, (CHUNK, CHUNK), 1)
    incl = (ri >= ci, ri <= ci)
    strict = (ri > ci, ri < ci)
    lane = lax.broadcasted_iota(jnp.int32, (CHUNK, LANE), 1)

    def body(i, carry):
        r0 = pl.multiple_of((part * PREP_CHUNKS + i) * CHUNK, CHUNK)
        rows = pl.ds(r0, CHUNK)
        win = xs_ref[pl.ds(r0, CHUNK + 2 * HALO), :]
        acc = win[HALO:HALO + CHUNK, :] * cw_ref[CONV_K // 2:CONV_K // 2 + 1, :]
        for j in range(CONV_K):
            if j != CONV_K // 2:
                lo = HALO + j - CONV_K // 2
                acc = acc + win[lo:lo + CHUNK, :] * cw_ref[j:j + 1, :]
        act = _silu(acc)

        zba = zba_ref[rows, :]
        beta_all = 1.0 / (1.0 + jnp.exp(-zba))
        xa = zba + dtb_ref[...]
        softplus = jnp.maximum(xa, 0.0) + jnp.log(1.0 + jnp.exp(-jnp.abs(xa)))
        bg = jnp.where(lane < LOG_COL, beta_all, -jnp.exp(alog_ref[...]) * softplus)

        slabs = [act[:, s * GDN_DK:(s + 1) * GDN_DK] for s in range(2 * GDN_HEADS)]
        ssq = [jnp.sum(x * x, axis=-1, keepdims=True) for x in slabs]
        slabs = [x * lax.rsqrt(s + EPS) for x, s in zip(slabs, ssq)]
        if axial:
            cos, s_up, s_dn = cos_ref[rows, :], sup_ref[rows, :], sdn_ref[rows, :]
            up = [pltpu.roll(x, GDN_DK - GDN_DK // 4, 1) for x in slabs]
            dn = [pltpu.roll(x, GDN_DK // 4, 1) for x in slabs]
            slabs = [x * cos + u * s_up + w * s_dn for x, u, w in zip(slabs, up, dn)]
        qs = [x * (GDN_DK ** -0.5) for x in slabs[:GDN_HEADS]]
        ks = slabs[GDN_HEADS:]
        kbfs = [k.astype(BF16) for k in ks]
        kks = [_dot_nt(kb_, kb_) for kb_ in kbfs]
        qks = [_dot_nt(q.astype(BF16), kb_) for q, kb_ in zip(qs, kbfs)]
        heads = [(qs[hd], ks[hd], act[:, 2 * GDN_W + hd * GDN_DK:2 * GDN_W + (hd + 1) * GDN_DK], kks[hd], qks[hd])
                 for hd in range(GDN_HEADS)]

        gcums = [_dot_exact_lhs(incl[d].astype(BF16), bg) for d in range(2)]
        gcum_ts = [g.T for g in gcums]
        for d in range(2):
            gcum, gcum_t = gcums[d], gcum_ts[d]
            last = gcum[0:1, :] if d else gcum[CHUNK - 1:CHUNK, :]
            e_in = jnp.exp(gcum)
            e_out = jnp.exp(last - gcum)
            gl_ref[i, d:d + 1, :] = jnp.exp(last)
            for hd in range(GDN_HEADS):
                q, k, v, kk, qk = heads[hd]
                cb = d * GDN_HEADS + hd
                cg = LOG_COL + cb
                beta = bg[:, cb:cb + 1]
                eg = e_in[:, cg:cg + 1]
                el = e_out[:, cg:cg + 1]
                diff = gcum[:, cg:cg + 1] - gcum_t[cg:cg + 1, :]
                decay = jnp.where(incl[d], jnp.exp(jnp.where(incl[d], diff, 0.0)), 0.0)
                a_ref[d, i, hd] = jnp.where(strict[d], beta * kk * decay, 0.0)
                qk_ref[d, i, hd] = (qk * decay).astype(BF16)
                kb = k * beta
                r_ref[d, i, hd, :, 0:GDN_DK] = (v * beta).astype(BF16)
                r_ref[d, i, hd, :, GDN_DK:2 * GDN_DK] = (kb * eg).astype(BF16)
                qd_ref[d, i, hd] = (q * eg).astype(BF16)
                kd_ref[d, i, hd] = (k * el).astype(BF16)
        return carry

    lax.fori_loop(0, PREP_CHUNKS, body, 0)


def _gdn_prep(axial, zqkv, zba, conv_w, a_log, dt_bias, rope):
    seq = DEC_SEQ if axial else SEQ
    nseq = DEC_BATCH if axial else BATCH
    nparts = seq // (PREP_CHUNKS * CHUNK)
    nchunk = nseq * seq // CHUNK
    blk0 = N_CTX // DEC_SEQ if axial else 0
    tok = lambda w: pl.BlockSpec((seq, w), lambda b, p: (blk0 + b, 0))
    full = lambda a: pl.BlockSpec(a.shape, lambda b, p: (0,) * a.ndim)
    ins = [zqkv, zba, conv_w, a_log, dt_bias]
    specs = [tok(3 * GDN_W), tok(LANE), full(conv_w), full(a_log), full(dt_bias)]
    if axial:
        ins += list(rope)
        specs += [full(t) for t in rope]
    tile = lambda w: pl.BlockSpec((2, PREP_CHUNKS, GDN_HEADS, CHUNK, w), lambda b, p: (0, b * nparts + p, 0, 0, 0))
    shape = lambda w, dt: jax.ShapeDtypeStruct((2, nchunk, GDN_HEADS, CHUNK, w), dt)
    return pl.pallas_call(
        functools.partial(_gdn_prep_kernel, axial, seq),
        out_shape=[shape(CHUNK, F32), shape(CHUNK, BF16), shape(2 * GDN_DK, BF16),
                   shape(GDN_DK, BF16), shape(GDN_DK, BF16),
                   jax.ShapeDtypeStruct((nchunk, 2, LANE), F32)],
        grid=(nseq, nparts),
        in_specs=specs,
        out_specs=[tile(CHUNK), tile(CHUNK), tile(2 * GDN_DK), tile(GDN_DK), tile(GDN_DK),
                   pl.BlockSpec((PREP_CHUNKS, 2, LANE), lambda b, p: (b * nparts + p, 0, 0))],
        scratch_shapes=[pltpu.VMEM((seq + 2 * HALO, 3 * GDN_W), F32)],
        compiler_params=_params(("parallel", "arbitrary")),
        name="gdn_prep_lat" if axial else "gdn_prep_ctx",
    )(*ins)


TRI_BLK = 8
TRI_ELEMS = CHUNK * CHUNK


def _tri_inverse_kernel(groups_per_dir, a_ref, o_ref, at_ref, tt_ref):
    backward = pl.program_id(0) >= groups_per_dir
    for blk in range(TRI_ELEMS // LANE):
        cols = slice(blk * LANE, (blk + 1) * LANE)
        at_ref[cols, :] = a_ref[:, cols].T
    tt_ref[...] = jnp.zeros(tt_ref.shape, F32)
    nblk = CHUNK // TRI_BLK

    def substitute(mirror):
        phys = (lambda idx: CHUNK - 1 - idx) if mirror else (lambda idx: idx)
        for ib in range(nblk):
            c_lo, width = ((nblk - 1 - ib) * TRI_BLK if mirror else 0), (ib + 1) * TRI_BLK

            def row_body(ii, carry, ib=ib, c_lo=c_lo, width=width):
                i = phys(ib * TRI_BLK + ii)
                cidx = lax.broadcasted_iota(jnp.int32, (width, LANE), 0) + c_lo
                acc = (cidx == i).astype(F32)
                for jb in range(ib + 1):
                    w_j = (jb + 1) * TRI_BLK
                    cj_lo = (nblk - 1 - jb) * TRI_BLK if mirror else 0
                    sub = slice(cj_lo - c_lo, cj_lo - c_lo + w_j)
                    part = acc[sub]
                    for jl in range(jb * TRI_BLK, (jb + 1) * TRI_BLK):
                        j = phys(jl)
                        arow = at_ref[pl.ds(i * CHUNK + j, 1), :]
                        part = part - arow * tt_ref[j * CHUNK + cj_lo:j * CHUNK + cj_lo + w_j, :]
                    acc = part if w_j == width else (
                        jnp.concatenate([part, acc[w_j:]], axis=0) if not mirror
                        else jnp.concatenate([acc[:width - w_j], part], axis=0))
                tt_ref[pl.ds(pl.multiple_of(i * CHUNK + c_lo, TRI_BLK), width), :] = acc
                return carry

            lax.fori_loop(0, TRI_BLK, row_body, 0)

    @pl.when(jnp.logical_not(backward))
    def _():
        substitute(False)

    @pl.when(backward)
    def _():
        substitute(True)

    for blk in range(TRI_ELEMS // LANE):
        cols = slice(blk * LANE, (blk + 1) * LANE)
        o_ref[:, cols] = tt_ref[cols, :].T.astype(BF16)


def _tri_inverse(a):
    nprob = a.shape[1] * a.shape[2]
    groups_per_dir = nprob // LANE
    blk = pl.BlockSpec((LANE, TRI_ELEMS), lambda g: (g, 0))
    out = pl.pallas_call(
        functools.partial(_tri_inverse_kernel, groups_per_dir),
        out_shape=jax.ShapeDtypeStruct((2 * nprob, TRI_ELEMS), BF16),
        grid=(2 * groups_per_dir,),
        in_specs=[blk],
        out_specs=blk,
        scratch_shapes=[pltpu.VMEM((TRI_ELEMS, LANE), F32), pltpu.VMEM((TRI_ELEMS, LANE), F32)],
        compiler_params=_params(("parallel",)),
        name="tri_inverse",
    )(a.reshape(2 * nprob, TRI_ELEMS))
    return out.reshape(a.shape)


def _gdn_scan_kernel(nchunk, t_ref, qk_ref, r_ref, qd_ref, kd_ref, gl_ref, s0_ref, *rest):
    o_ref, sfin_ref, state_ref = rest[-3:]
    state_ref[...] = s0_ref[...]
    streams = [(d, hd) for d in range(2) for hd in range(GDN_HEADS)]

    def body(c, carry):
        cc = (c, nchunk - 1 - c)
        uws = [_dot(t_ref[d, cc[d], hd], r_ref[d, cc[d], hd]).astype(BF16) for d, hd in streams]
        mbs = [_dot_tn(kd_ref[d, cc[d], hd], uw) for (d, hd), uw in zip(streams, uws)]
        qos = [_dot(qk_ref[d, cc[d], hd], uw) for (d, hd), uw in zip(streams, uws)]
        sts = [state_ref[d * GDN_HEADS + hd] for d, hd in streams]
        sbs = [s.astype(BF16) for s in sts]
        for (d, hd), qo, sb in zip(streams, qos, sbs):
            rows = pl.ds(pl.multiple_of(cc[d] * CHUNK, CHUNK), CHUNK)
            qp = (qd_ref[d, cc[d], hd].astype(F32) - qo[:, GDN_DK:]).astype(BF16)
            o_ref[d, rows, hd * GDN_DK:(hd + 1) * GDN_DK] = _dot(qp, sb) + qo[:, :GDN_DK]
        for (d, hd), mb, s, sb in zip(streams, mbs, sts, sbs):
            sidx = d * GDN_HEADS + hd
            gl = gl_ref[cc[d]][d:d + 1, LOG_COL + sidx:LOG_COL + sidx + 1]
            state_ref[sidx] = s * gl - _dot(mb[:, GDN_DK:].astype(BF16), sb) + mb[:, :GDN_DK]
        return carry

    lax.fori_loop(0, nchunk, body, 0)
    sfin_ref[...] = state_ref[...]


def _gdn_scan(seq, nseq, tinv, qk, r, qd, kd, gl, s0, o_all=None):
    nc = seq // CHUNK
    blk0 = 0 if o_all is None else N_CTX // seq
    tile = lambda w: pl.BlockSpec((2, nc, GDN_HEADS, CHUNK, w), lambda b: (0, b, 0, 0, 0))
    sblk = pl.BlockSpec((None, N_STREAM, GDN_DK, GDN_DK), lambda b: (b, 0, 0, 0))
    oblk = pl.BlockSpec((2, seq, GDN_W), lambda b: (0, blk0 + b, 0))
    ins = [tinv, qk, r, qd, kd, gl, s0]
    specs = [tile(CHUNK), tile(CHUNK), tile(2 * GDN_DK), tile(GDN_DK), tile(GDN_DK),
             pl.BlockSpec((nc, 2, LANE), lambda b: (b, 0, 0)), sblk]
    aliases = {}
    if o_all is not None:
        ins.append(o_all)
        specs.append(pl.BlockSpec(memory_space=pl.ANY))
        aliases = {len(ins) - 1: 0}
    return pl.pallas_call(
        functools.partial(_gdn_scan_kernel, nc),
        out_shape=[jax.ShapeDtypeStruct((2, N_TOK, GDN_W), F32),
                   jax.ShapeDtypeStruct((nseq, N_STREAM, GDN_DK, GDN_DK), F32)],
        grid=(nseq,),
        in_specs=specs,
        out_specs=[oblk, sblk],
        scratch_shapes=[pltpu.VMEM((N_STREAM, GDN_DK, GDN_DK), F32)],
        input_output_aliases=aliases,
        compiler_params=_params(("parallel",)),
        name="gdn_scan_lat" if seq == DEC_SEQ else "gdn_scan_ctx",
    )(*ins)


def _first_max(vals):
    sel = []
    taken = None
    for a, va in enumerate(vals):
        is_max = None
        for b, vb in enumerate(vals):
            if a == b:
                continue
            c = va >= vb
            is_max = c if is_max is None else (is_max & c)
        if taken is not None:
            is_max = is_max & jnp.logical_not(taken)
        sel.append(is_max)
        taken = is_max if taken is None else (taken | is_max)
    return sel


PAIR_SLOTS = ((0, 1), (0, 2), (0, 3), (1, 3), (1, 2), (3, 2))
N_CLASS = N_GROUPS * len(PAIR_SLOTS)
CLASS_EXPERTS = tuple((g * EXP_PER_GROUP + a, g * EXP_PER_GROUP + b)
                      for g in range(N_GROUPS) for a, b in PAIR_SLOTS)


def _route_rows(lt):
    gl = [lt[a:a + 1, :] for a in range(N_GROUPS)]
    gsel = _first_max(gl)
    gmax = functools.reduce(jnp.maximum, gl)
    gden = functools.reduce(lambda x, y: x + y, [jnp.exp(x - gmax) for x in gl])
    g_w = 1.0 / gden
    el = []
    for kx in range(EXP_PER_GROUP):
        acc = None
        for a in range(N_GROUPS):
            row = N_GROUPS + a * EXP_PER_GROUP + kx
            term = jnp.where(gsel[a], lt[row:row + 1, :], 0.0)
            acc = term if acc is None else acc + term
        el.append(acc)
    emax = functools.reduce(jnp.maximum, el)
    ee = [jnp.exp(x - emax) for x in el]
    eden = functools.reduce(lambda x, y: x + y, ee)
    ep = [x / eden for x in ee]
    top1 = _first_max(ep)
    ep2 = [jnp.where(top1[kx], -1.0, ep[kx]) for kx in range(EXP_PER_GROUP)]
    top2 = _first_max(ep2)
    chosen = [top1[kx] | top2[kx] for kx in range(EXP_PER_GROUP)]
    wsum = functools.reduce(lambda x, y: x + y,
                            [jnp.where(chosen[kx], ep[kx], 0.0) for kx in range(EXP_PER_GROUP)])
    within = [jnp.where(chosen[kx], ep[kx] / wsum, 0.0) for kx in range(EXP_PER_GROUP)]
    cls = jnp.zeros(g_w.shape, jnp.int32)
    w_a = jnp.zeros(g_w.shape, F32)
    w_b = jnp.zeros(g_w.shape, F32)
    for g in range(N_GROUPS):
        for kx, (a, b) in enumerate(PAIR_SLOTS):
            hit = gsel[g] & chosen[a] & chosen[b]
            cls = jnp.where(hit, g * len(PAIR_SLOTS) + kx, cls)
            w_a = jnp.where(hit, g_w * within[a], w_a)
            w_b = jnp.where(hit, g_w * within[b], w_b)
    return cls, w_a, w_b


H2W = D_MODEL + LANE
COL_WA = D_MODEL
COL_WB = D_MODEL + 1


def _outproj_kernel(nx, idx_ref, *refs):
    x = _token_tile(nx, refs)
    (ona_ref, og_ref, zg_ref, mod_ref, wo_ref, gng_ref, n2g_ref, wr_ref, br_ref,
     x1_ref, h2x_ref, cls_ref) = refs[nx:]
    del idx_ref
    m = mod_ref[...]
    og = og_ref[0] + og_ref[1]
    zg = zg_ref[...]
    parts = []
    for hd in range(GDN_HEADS):
        cols = slice(hd * GDN_DK, (hd + 1) * GDN_DK)
        oh = og[:, cols]
        oh = oh * lax.rsqrt(jnp.mean(oh * oh, axis=-1, keepdims=True) + EPS) * gng_ref[...]
        parts.append((oh * _silu(zg[:, cols])).astype(BF16))
    mix = _dot(ona_ref[...].astype(BF16), wo_ref[0:NA_W, :])
    for hd in range(GDN_HEADS):
        r0 = NA_W + hd * GDN_DK
        mix = mix + _dot(parts[hd], wo_ref[r0:r0 + GDN_DK, :])
    x1 = x + m[2:3, :] * mix
    x1_ref[...] = x1
    y = x1 * lax.rsqrt(jnp.mean(x1 * x1, axis=-1, keepdims=True) + EPS) * n2g_ref[...]
    h2 = y * (1.0 + m[4:5, :]) + m[3:4, :]
    h2x_ref[:, 0:D_MODEL] = h2.astype(BF16).astype(F32)
    logits = _dot3(h2, wr_ref[...]) + br_ref[...]
    cls, w_a, w_b = _route_rows(logits.T)
    rid = lax.broadcasted_iota(jnp.int32, (8, TM), 0)
    wt = jnp.where(rid == COL_WA - D_MODEL, w_a, jnp.where(rid == COL_WB - D_MODEL, w_b, 0.0))
    wt = jnp.concatenate([wt, jnp.zeros((LANE - 8, TM), F32)], axis=0)
    h2x_ref[:, D_MODEL:H2W] = wt.T
    cls_ref[...] = jnp.broadcast_to(cls, (8, TM))


def _outproj(tile_mod, x, o_na, o_gdn, zg, mod, w_out, gng, n2g, w_r, b_r):
    tok = lambda w: pl.BlockSpec((TM, w), lambda t, idx: (t, 0))
    full = lambda a: pl.BlockSpec(a.shape, lambda t, idx: (0,) * a.ndim)
    xs = list(x) if isinstance(x, tuple) else [x]
    return pl.pallas_call(
        functools.partial(_outproj_kernel, len(xs)),
        out_shape=[jax.ShapeDtypeStruct((N_TOK, D_MODEL), F32),
                   jax.ShapeDtypeStruct((N_TOK, H2W), F32),
                   jax.ShapeDtypeStruct((N_TOK // TM, 8, TM), jnp.int32)],
        grid_spec=pltpu.PrefetchScalarGridSpec(
            num_scalar_prefetch=1,
            grid=(N_TOK // TM,),
            in_specs=_token_specs(x) + [
                tok(NA_W),
                pl.BlockSpec((2, TM, GDN_W), lambda t, idx: (0, t, 0)), tok(GDN_W),
                pl.BlockSpec((None, 6, D_MODEL), lambda t, idx: (idx[t], 0, 0)),
                full(w_out), full(gng), full(n2g), full(w_r), full(b_r),
            ],
            out_specs=[tok(D_MODEL), tok(H2W), pl.BlockSpec((None, 8, TM), lambda t, idx: (t, 0, 0))],
        ),
        compiler_params=_params(("parallel",)),
        name="outproj",
    )(tile_mod, *xs, o_na, o_gdn, zg, mod, w_out, gng, n2g, w_r, b_r)


TM_X = 256
N_XTILE = N_TOK // TM_X + N_CLASS
N_SLOT = N_XTILE * TM_X
CLS_ROWS = N_TOK // LANE
TAB_EA, TAB_EB, TAB_USED, TAB_LAST = 0, 1, 2, 3


def _route_pos_kernel(cls_ref, slot_ref, tab_ref):
    cls = cls_ref[...]
    li = lax.broadcasted_iota(jnp.int32, (LANE, LANE), 0)
    lj = lax.broadcasted_iota(jnp.int32, (LANE, LANE), 1)
    before_lane = (li < lj).astype(BF16)
    ri = lax.broadcasted_iota(jnp.int32, (CLS_ROWS, CLS_ROWS), 0)
    rj = lax.broadcasted_iota(jnp.int32, (CLS_ROWS, CLS_ROWS), 1)
    before_row = (rj < ri).astype(BF16)
    lane = lax.broadcasted_iota(jnp.int32, (1, LANE), 1)
    tile_start = lane * TM_X
    off = jnp.zeros((1, 1), jnp.int32)
    slot = jnp.zeros(cls.shape, jnp.int32)
    tab_a = jnp.zeros((1, LANE), jnp.int32)
    tab_b = jnp.zeros((1, LANE), jnp.int32)
    tab_last = jnp.full((1, LANE), -1, jnp.int32)
    for c in range(N_CLASS):
        hit = cls == c
        one = hit.astype(F32)
        in_row = _dot(one.astype(BF16), before_lane)
        row_tot = jnp.broadcast_to(jnp.sum(one, axis=1, keepdims=True), one.shape)
        rank = in_row + _dot(before_row, row_tot.astype(BF16))
        count = jnp.sum(row_tot[:, 0:1], axis=0, keepdims=True).astype(jnp.int32)
        slot = jnp.where(hit, off + rank.astype(jnp.int32), slot)
        nxt = off + (((count + (TM_X - 1)) >> 8) << 8)
        mine = (tile_start >= off) & (tile_start < nxt)
        tab_a = jnp.where(mine, CLASS_EXPERTS[c][0], tab_a)
        tab_b = jnp.where(mine, CLASS_EXPERTS[c][1], tab_b)
        tab_last = jnp.where((lane == c) & (nxt > off), (nxt >> 8) - 1, tab_last)
        off = nxt
    slot_ref[...] = slot
    row = lax.broadcasted_iota(jnp.int32, (8, LANE), 0)
    used = jnp.broadcast_to(off >> 8, (8, LANE))
    tab_ref[...] = jnp.where(row == TAB_EA, tab_a, jnp.where(row == TAB_EB, tab_b,
                                                               jnp.where(row == TAB_LAST, tab_last, used)))


def _route_positions(cls):
    assert TM_X == 256
    return pl.pallas_call(
        _route_pos_kernel,
        out_shape=[jax.ShapeDtypeStruct((CLS_ROWS, LANE), jnp.int32),
                   jax.ShapeDtypeStruct((8, LANE), jnp.int32)],
        name="route_positions",
    )(cls)


ROW_UNROLL = 8


def _permute_kernel(slot_ref, tab_ref, h_ref, xs_ref, buf_ref, zero_ref, sem_ref, zsem_ref):
    t = pl.program_id(0)
    cur = t % 2

    @pl.when(t == 0)
    def _():
        zero_ref[...] = jnp.zeros(zero_ref.shape, F32)

        def tile_copy(c):
            return pltpu.make_async_copy(
                zero_ref, xs_ref.at[pl.ds(pl.multiple_of(tab_ref[TAB_LAST, c] * TM_X, TM_X), TM_X), :], zsem_ref.at[0])

        for c in range(N_CLASS):
            @pl.when(tab_ref[TAB_LAST, c] >= 0)
            def _():
                tile_copy(c).start()
        for c in range(N_CLASS):
            @pl.when(tab_ref[TAB_LAST, c] >= 0)
            def _():
                tile_copy(c).wait()

    buf_ref[cur] = h_ref[...]

    def issue(i, carry):
        _row_copy(buf_ref.at[cur], i, xs_ref, slot_ref[t * TM + i], sem_ref.at[cur]).start()
        return carry

    lax.fori_loop(0, TM, issue, 0, unroll=ROW_UNROLL)

    def drain(which):
        def wait(i, carry):
            _row_copy(buf_ref.at[which], 0, xs_ref, 0, sem_ref.at[which]).wait()
            return carry
        lax.fori_loop(0, TM, wait, 0, unroll=ROW_UNROLL)

    @pl.when(t > 0)
    def _():
        drain(1 - cur)

    @pl.when(t == pl.num_programs(0) - 1)
    def _():
        drain(cur)


def _permute(slot, tab, h2x):
    assert TM_X == TM
    return pl.pallas_call(
        _permute_kernel,
        out_shape=jax.ShapeDtypeStruct((N_SLOT, H2W), F32),
        grid_spec=pltpu.PrefetchScalarGridSpec(
            num_scalar_prefetch=2,
            grid=(N_TOK // TM,),
            in_specs=[pl.BlockSpec((TM, H2W), lambda t, s, tab: (t, 0))],
            out_specs=pl.BlockSpec(memory_space=pl.ANY),
            scratch_shapes=[pltpu.VMEM((2, TM, H2W), F32), pltpu.VMEM((TM_X, H2W), F32),
                            pltpu.SemaphoreType.DMA((2,)), pltpu.SemaphoreType.DMA((1,))],
        ),
        compiler_params=_params(("arbitrary",)),
        name="moe_permute",
    )(slot, tab, h2x)


def _experts_kernel(tab_ref, xs_ref, w1a_ref, w3a_ref, w2a_ref, w1b_ref, w3b_ref, w2b_ref, ys_ref,
                    c1a_ref, c3a_ref, c2a_ref, c1b_ref, c3b_ref, c2b_ref):
    t = pl.program_id(0)
    prev = jnp.maximum(t - 1, 0)

    @pl.when(t < tab_ref[TAB_USED, 0])
    def _():
        @pl.when((t == 0) | (tab_ref[TAB_EA, t] != tab_ref[TAB_EA, prev]))
        def _():
            c1a_ref[...] = w1a_ref[...].astype(BF16)
            c3a_ref[...] = w3a_ref[...].astype(BF16)
            c2a_ref[...] = w2a_ref[...].astype(BF16)

        @pl.when((t == 0) | (tab_ref[TAB_EB, t] != tab_ref[TAB_EB, prev]))
        def _():
            c1b_ref[...] = w1b_ref[...].astype(BF16)
            c3b_ref[...] = w3b_ref[...].astype(BF16)
            c2b_ref[...] = w2b_ref[...].astype(BF16)

        x = xs_ref[:, 0:D_MODEL].astype(BF16)
        gates = xs_ref[:, D_MODEL:H2W]
        w_a = gates[:, COL_WA - D_MODEL:COL_WA - D_MODEL + 1]
        w_b = gates[:, COL_WB - D_MODEL:COL_WB - D_MODEL + 1]
        a1 = _dot(x, c1a_ref[...])
        a3 = _dot(x, c3a_ref[...])
        b1 = _dot(x, c1b_ref[...])
        b3 = _dot(x, c3b_ref[...])
        sa = (_silu(a1) * a3 * w_a).astype(BF16)
        sb = (_silu(b1) * b3 * w_b).astype(BF16)
        ys_ref[...] = _dot(sa, c2a_ref[...]) + _dot(sb, c2b_ref[...])


def _experts(layer, tab, xs, w1, w3, w2):
    last = lambda t, tab: jnp.minimum(t, tab[TAB_USED, 0] - 1)
    tile = lambda w: pl.BlockSpec((TM_X, w), lambda t, tab: (last(t, tab), 0))
    wspec = lambda shape, row: pl.BlockSpec((None, None) + shape,
                                            lambda t, tab: (layer, tab[row, last(t, tab)], 0, 0))
    up, down = (D_MODEL, EXPERT_FF), (EXPERT_FF, D_MODEL)
    return pl.pallas_call(
        _experts_kernel,
        out_shape=jax.ShapeDtypeStruct((N_SLOT, D_MODEL), F32),
        grid_spec=pltpu.PrefetchScalarGridSpec(
            num_scalar_prefetch=1,
            grid=(N_XTILE,),
            in_specs=[tile(H2W),
                      wspec(up, TAB_EA), wspec(up, TAB_EA), wspec(down, TAB_EA),
                      wspec(up, TAB_EB), wspec(up, TAB_EB), wspec(down, TAB_EB)],
            out_specs=tile(D_MODEL),
            scratch_shapes=[pltpu.VMEM(up, BF16), pltpu.VMEM(up, BF16), pltpu.VMEM(down, BF16),
                            pltpu.VMEM(up, BF16), pltpu.VMEM(up, BF16), pltpu.VMEM(down, BF16)],
        ),
        compiler_params=_params(("arbitrary",)),
        name="moe_experts",
    )(tab, xs, w1, w3, w2, w1, w3, w2)


def _unpermute_kernel(idx_ref, slot_ref, x1_ref, mod_ref, ys_ref, *rest):
    outs, (buf_ref, sem_ref) = rest[:-2], rest[-2:]
    del idx_ref
    t = pl.program_id(0)
    cur = t % 2

    def fetch(tile, which):
        def issue(i, carry):
            _row_copy(ys_ref, slot_ref[tile * TM + i], buf_ref.at[which], i, sem_ref.at[which]).start()
            return carry
        lax.fori_loop(0, TM, issue, 0, unroll=ROW_UNROLL)

    @pl.when(t == 0)
    def _():
        fetch(0, 0)

    @pl.when(t + 1 < pl.num_programs(0))
    def _():
        fetch(t + 1, 1 - cur)

    def wait(i, carry):
        _row_copy(ys_ref, 0, buf_ref.at[cur], 0, sem_ref.at[cur]).wait()
        return carry

    lax.fori_loop(0, TM, wait, 0, unroll=ROW_UNROLL)
    y = x1_ref[...] + mod_ref[5:6, :] * buf_ref[cur]

    @pl.when(t < N_CTX // TM)
    def _():
        outs[0][...] = y

    @pl.when(t >= N_CTX // TM)
    def _():
        outs[1][...] = y


def _unpermute(tile_mod, slot, x1, mod, ys):
    tok = pl.BlockSpec((TM, D_MODEL), lambda t, idx, s: (t, 0))
    out_shape = [jax.ShapeDtypeStruct((N_CTX, D_MODEL), F32), jax.ShapeDtypeStruct((N_LAT, D_MODEL), F32)]
    out_specs = _token_specs((None, None))
    return pl.pallas_call(
        _unpermute_kernel,
        out_shape=out_shape,
        grid_spec=pltpu.PrefetchScalarGridSpec(
            num_scalar_prefetch=2,
            grid=(N_TOK // TM,),
            in_specs=[tok, pl.BlockSpec((None, 6, D_MODEL), lambda t, idx, s: (idx[t], 0, 0)),
                      pl.BlockSpec(memory_space=pl.ANY)],
            out_specs=out_specs,
            scratch_shapes=[pltpu.VMEM((2, TM, D_MODEL), F32), pltpu.SemaphoreType.DMA((2,))],
        ),
        compiler_params=_params(("arbitrary",)),
        name="moe_unpermute",
    )(tile_mod, slot, x1, mod, ys)


def _tile_mod_index(tile):
    t = np.arange(N_TOK // tile) * tile
    return jnp.asarray(np.where(t < N_CTX, 0, 1 + (t - N_CTX) // DEC_SEQ), jnp.int32)


def kernel(x_prompt, x_sample, c, cache_k, cache_v, state_ssm, c_ctx, ada_w, ada_b, norm1_g, norm2_g, w_in, w_out, na_qn_g, na_kn_g, na_rpb, gdn_conv_w, gdn_a_log, gdn_dt_bias, gdn_norm_g, moe_w_rg, moe_b_rg, moe_w_re, moe_b_re, moe_w1, moe_w3, moe_w2):
    x = (x_prompt.reshape(N_CTX, D_MODEL), x_sample.reshape(N_LAT, D_MODEL))
    cv = jnp.concatenate([c_ctx[None, :], c, jnp.zeros((N_MOD_PAD - N_MOD, D_MODEL), F32)], axis=0)
    mod_all = _modulation(cv, ada_w, ada_b).reshape(DEPTH, N_MOD_PAD, 6, D_MODEL)

    idx_tm = _tile_mod_index(TM)
    hh = np.arange(NA_W) // NA_HD
    ones_bd = jnp.asarray(hh[:, None] == hh[None, :], BF16)
    rope = _rope_tables()
    cache_k4 = cache_k.reshape(DEC_BATCH, DEPTH, PAST_LEN, NA_W)
    cache_v4 = cache_v.reshape(DEC_BATCH, DEPTH, PAST_LEN, NA_W)
    lane_pad = lambda a, at: jnp.zeros((1, LANE), F32).at[0, at:at + a.size].set(a.reshape(-1))
    zeros_state = jnp.zeros((BATCH, 2 * GDN_HEADS, GDN_DK, GDN_DK), F32)

    caches, ss, pending = None, [], None
    for l in range(DEPTH):
        mod = mod_all[l]
        w_ba = jnp.zeros((D_MODEL, LANE), BF16).at[:, :N_IN - N_MAIN].set(w_in[l, :, N_MAIN:].astype(BF16))
        qg = jnp.tile(na_qn_g[l], NA_HEADS)[None, :]
        kg = jnp.tile(na_kn_g[l], NA_HEADS)[None, :]
        proj = (mod, norm1_g[l][None, :], w_in, w_ba, ones_bd, qg, kg, caches)
        if pending is None:
            q, k, v, zqkv, zg, zba, *caches = _inproj(l, idx_tm, x, *proj)
        else:
            x, q, k, v, zqkv, zg, zba, *caches = _inproj_gather(l, idx_tm, *pending, *proj)

        bias = _na_bias_table(na_rpb[l])
        o_na = _na_attention(l, q, k, v, cache_k4, cache_v4, bias, _ctx_attention(q, k, v))

        conv_w = jnp.zeros((8, 3 * GDN_W), F32).at[:CONV_K].set(gdn_conv_w[l])
        a_row = lane_pad(gdn_a_log[l], 2 * GDN_HEADS)
        dt_row = lane_pad(gdn_dt_bias[l], 2 * GDN_HEADS)
        a_c, *ops_c = _gdn_prep(False, zqkv, zba, conv_w, a_row, dt_row, None)
        a_l, *ops_l = _gdn_prep(True, zqkv, zba, conv_w, a_row, dt_row, rope)
        og_c, s_ctx = _gdn_scan(SEQ, BATCH, _tri_inverse(a_c), *ops_c, zeros_state)
        s0_lat = state_ssm[:, l].reshape(DEC_BATCH, N_STREAM, GDN_DK, GDN_DK)
        o_gdn, _ = _gdn_scan(DEC_SEQ, DEC_BATCH, _tri_inverse(a_l), *ops_l, s0_lat, og_c)

        w_r = jnp.zeros((D_MODEL, LANE), F32).at[:, :N_GROUPS].set(moe_w_rg[l])
        w_r = w_r.at[:, N_GROUPS:N_GROUPS + N_EXPERTS].set(moe_w_re[l])
        b_r = lane_pad(jnp.concatenate([moe_b_rg[l], moe_b_re[l]]), 0)
        x1, h2x, cls = _outproj(idx_tm, x, o_na, o_gdn, zg, mod, w_out[l].astype(BF16),
                                gdn_norm_g[l][None, :], norm2_g[l][None, :], w_r, b_r)
        slot, tab = _route_positions(cls[:, 0, :].reshape(CLS_ROWS, LANE))
        slot = slot.reshape(N_TOK)
        ys = _experts(l, tab, _permute(slot, tab, h2x), moe_w1, moe_w3, moe_w2)
        if l == DEPTH - 1:
            x = _unpermute(idx_tm, slot, x1, mod, ys)
        else:
            pending = (slot, x1, mod, ys)

        ss.append(s_ctx.reshape(BATCH, 2, GDN_HEADS, GDN_DK, GDN_DK))

    y_prompt = x[0].reshape(BATCH, SEQ, D_MODEL)
    y_sample = x[1].reshape(DEC_BATCH, DEC_SEQ, D_MODEL)
    new_k, new_v = (a.reshape(BATCH, DEPTH, SEQ, NA_HEADS, NA_HD) for a in caches)
    return (y_prompt, y_sample, new_k, new_v, jnp.stack(ss, axis=1))
```

```python
import functools

import jax
import jax.numpy as jnp
import numpy as np
from jax import lax
from jax.experimental import pallas as pl
from jax.experimental.pallas import tpu as pltpu

F32 = jnp.float32
BF16 = jnp.bfloat16

D_MODEL = 1024
BATCH = 16
SEQ = 256
DEPTH = 2
DEC_BATCH = 8
DEC_SEQ = 1024
PAST_LEN = 512
GRID_W = 64
GRID_ROWS = DEC_SEQ // GRID_W
NA_HEADS = 8
NA_HD = 64
NA_W = NA_HEADS * NA_HD
NA_KH = 8
NA_KW = 16
GDN_HEADS = 4
GDN_DK = 128
GDN_W = GDN_HEADS * GDN_DK
CONV_K = 5
CHUNK = 64
ROPE_BASE = 10000.0
N_GROUPS = 4
EXP_PER_GROUP = 4
N_EXPERTS = 16
EXPERT_FF = 512
EPS = 1e-6
N_IN = 3 * NA_W + 4 * GDN_W + 4 * GDN_HEADS

N_CTX = BATCH * SEQ
N_LAT = DEC_BATCH * DEC_SEQ
N_TOK = N_CTX + N_LAT
N_MOD = 1 + DEC_BATCH
N_MOD_PAD = 16
LANE = 128
N_MAIN = 3 * NA_W + 4 * GDN_W
TM = 256
VMEM_LIMIT = 56 * 1024 * 1024


def _dot(a, b):
    return jnp.dot(a, b, preferred_element_type=F32)


def _dot_nt(a, b):
    return lax.dot_general(a, b, (((1,), (1,)), ((), ())), preferred_element_type=F32)


def _dot_tn(a, b):
    return lax.dot_general(a, b, (((0,), (0,)), ((), ())), preferred_element_type=F32)


def _split2(x):
    hi = x.astype(BF16)
    lo = (x - hi.astype(F32)).astype(BF16)
    return hi, lo


def _split3(x):
    hi = x.astype(BF16)
    r = x - hi.astype(F32)
    mid = r.astype(BF16)
    lo = (r - mid.astype(F32)).astype(BF16)
    return hi, mid, lo


def _dot3(a, b):
    ah, al = _split2(a)
    bh, bl = _split2(b)
    return _dot(ah, bh) + (_dot(ah, bl) + _dot(al, bh))


def _dot_exact_lhs(mask_bf16, x):
    hi, mid, lo = _split3(x)
    return _dot(mask_bf16, hi) + (_dot(mask_bf16, mid) + _dot(mask_bf16, lo))


def _silu(x):
    return x * (1.0 / (1.0 + jnp.exp(-x)))


def _params(sem):
    return pltpu.CompilerParams(dimension_semantics=sem, vmem_limit_bytes=VMEM_LIMIT)


def _mod_kernel(cv_ref, w_ref, b_ref, o_ref):
    cv = cv_ref[...]
    s = _silu(cv)
    o_ref[...] = jnp.dot(s, w_ref[...], preferred_element_type=F32,
                         precision=lax.Precision.HIGHEST) + b_ref[...]


def _modulation(cv, ada_w, ada_b):
    nblk = 1024
    return pl.pallas_call(
        _mod_kernel,
        out_shape=jax.ShapeDtypeStruct((DEPTH, N_MOD_PAD, 6 * D_MODEL), F32),
        grid=(DEPTH, 6 * D_MODEL // nblk),
        in_specs=[
            pl.BlockSpec((N_MOD_PAD, D_MODEL), lambda l, j: (0, 0)),
            pl.BlockSpec((None, D_MODEL, nblk), lambda l, j: (l, 0, j)),
            pl.BlockSpec((None, 1, nblk), lambda l, j: (l, 0, j)),
        ],
        out_specs=pl.BlockSpec((None, N_MOD_PAD, nblk), lambda l, j: (l, 0, j)),
        compiler_params=_params(("parallel", "parallel")),
        name="modulation",
    )(cv, ada_w, ada_b.reshape(DEPTH, 1, 6 * D_MODEL))


def _token_specs(x):
    if not isinstance(x, tuple):
        return [pl.BlockSpec((TM, D_MODEL), lambda t, *_: (t, 0))]
    return [pl.BlockSpec((TM, D_MODEL), lambda t, *_: (jnp.minimum(t, N_CTX // TM - 1), 0)),
            pl.BlockSpec((TM, D_MODEL), lambda t, *_: (jnp.maximum(t - N_CTX // TM, 0), 0))]


def _token_tile(nx, refs):
    if nx == 1:
        return refs[0][...]
    return jnp.where(pl.program_id(0) < N_CTX // TM, refs[0][...], refs[1][...])


def _inproj_kernel(nx, idx_ref, *refs):
    del idx_ref
    _inproj_body(_token_tile(nx, refs), refs[nx:nx + 7], refs[-8:])


def _inproj_body(x, params, outs):
    mod_ref, n1g_ref, win_ref, wba_ref, ones_ref, qg_ref, kg_ref = params
    q_ref, k_ref, v_ref, zqkv_ref, zg_ref, zba_ref, kcache_ref, vcache_ref = outs
    y = x * lax.rsqrt(jnp.mean(x * x, axis=-1, keepdims=True) + EPS) * n1g_ref[...]
    m = mod_ref[...]
    h = (y * (1.0 + m[1:2, :]) + m[0:1, :]).astype(BF16)
    z = _dot(h, win_ref[...])
    zba_ref[...] = _dot(h, wba_ref[...])
    ones = ones_ref[...]

    def head_rms(zz, g):
        hi, lo = _split2(zz * zz)
        ms = (_dot(hi, ones) + _dot(lo, ones)) * (1.0 / NA_HD)
        return zz * lax.rsqrt(ms + EPS) * g

    q_ref[...] = head_rms(z[:, 0:NA_W], qg_ref[...]) * (NA_HD ** -0.5)
    k = head_rms(z[:, NA_W:2 * NA_W], kg_ref[...])
    v = z[:, 2 * NA_W:3 * NA_W]
    k_ref[...] = k
    v_ref[...] = v
    zqkv_ref[...] = z[:, 3 * NA_W:3 * NA_W + 3 * GDN_W]
    zg_ref[...] = z[:, 3 * NA_W + 3 * GDN_W:N_MAIN]

    @pl.when(pl.program_id(0) < BATCH)
    def _():
        kcache_ref[...] = k
        vcache_ref[...] = v


def _inproj(layer, tile_mod, x, mod, n1g, w_main, w_ba, ones_bd, qg, kg, caches):
    assert TM == SEQ
    tok = lambda w: pl.BlockSpec((TM, w), lambda t, idx: (t, 0))
    full = lambda a: pl.BlockSpec(a.shape, lambda t, idx: (0,) * a.ndim)
    out_w = (NA_W, NA_W, NA_W, 3 * GDN_W, GDN_W, LANE)
    cache_blk = pl.BlockSpec((None, None, SEQ, NA_W), lambda t, idx: (jnp.minimum(t, BATCH - 1), layer, 0, 0))
    cache_shape = jax.ShapeDtypeStruct((BATCH, DEPTH, SEQ, NA_W), F32)
    xs = list(x) if isinstance(x, tuple) else [x]
    ins = [tile_mod, *xs, mod, n1g, w_main, w_ba, ones_bd, qg, kg]
    specs = _token_specs(x) + [pl.BlockSpec((None, 6, D_MODEL), lambda t, idx: (idx[t], 0, 0)),
                               full(n1g), full(w_main), full(w_ba), full(ones_bd), full(qg), full(kg)]
    aliases = {}
    if caches is not None:
        aliases = {len(ins): len(out_w), len(ins) + 1: len(out_w) + 1}
        ins += list(caches)
        specs += [pl.BlockSpec(memory_space=pl.ANY)] * 2
    return pl.pallas_call(
        functools.partial(_inproj_kernel, len(xs)),
        out_shape=[jax.ShapeDtypeStruct((N_TOK, w), F32) for w in out_w] + [cache_shape, cache_shape],
        grid_spec=pltpu.PrefetchScalarGridSpec(
            num_scalar_prefetch=1,
            grid=(N_TOK // TM,),
            in_specs=specs,
            out_specs=[tok(w) for w in out_w] + [cache_blk, cache_blk],
        ),
        input_output_aliases=aliases,
        compiler_params=_params(("arbitrary",)),
        name="inproj",
    )(*ins)


def _row_copy(src, src_row, dst, dst_row, sem):
    return pltpu.make_async_copy(src.at[pl.ds(src_row, 1), :], dst.at[pl.ds(dst_row, 1), :], sem)


def _inproj_gather_kernel(idx_ref, slot_ref, x1_ref, modp_ref, ys_ref, *refs):
    del idx_ref
    params, outs, (buf_ref, sem_ref) = refs[:7], refs[-11:-2], refs[-2:]
    t = pl.program_id(0)
    last = pl.num_programs(0) - 1
    cur = t % 2

    def fetch(tile, which):
        for i in range(TM):
            _row_copy(ys_ref, slot_ref[tile * TM + i], buf_ref.at[which], i, sem_ref.at[which]).start()

    def drain(which):
        for i in range(TM):
            _row_copy(ys_ref, 0, buf_ref.at[which], 0, sem_ref.at[which]).wait()

    @pl.when(t == 0)
    def _():
        fetch(0, 0)

    fetch(jnp.minimum(t + 1, last), 1 - cur)
    drain(cur)
    x = x1_ref[...] + modp_ref[5:6, :] * buf_ref[cur]
    outs[0][...] = x
    _inproj_body(x, params, outs[1:])

    @pl.when(t == last)
    def _():
        drain(1 - cur)


def _inproj_gather(layer, tile_mod, slot, x1, mod_prev, ys, mod, n1g, w_main, w_ba, ones_bd, qg, kg, caches):
    tok = lambda w: pl.BlockSpec((TM, w), lambda t, idx, s: (t, 0))
    full = lambda a: pl.BlockSpec(a.shape, lambda t, idx, s: (0,) * a.ndim)
    modblk = pl.BlockSpec((None, 6, D_MODEL), lambda t, idx, s: (idx[t], 0, 0))
    out_w = (D_MODEL, NA_W, NA_W, NA_W, 3 * GDN_W, GDN_W, LANE)
    cache_blk = pl.BlockSpec((None, None, SEQ, NA_W), lambda t, idx, s: (jnp.minimum(t, BATCH - 1), layer, 0, 0))
    cache_shape = jax.ShapeDtypeStruct((BATCH, DEPTH, SEQ, NA_W), F32)
    ins = [tile_mod, slot, x1, mod_prev, ys, mod, n1g, w_main, w_ba, ones_bd, qg, kg, *caches]
    specs = [tok(D_MODEL), modblk, pl.BlockSpec(memory_space=pl.ANY), modblk,
             full(n1g), full(w_main), full(w_ba), full(ones_bd), full(qg), full(kg)]
    specs += [pl.BlockSpec(memory_space=pl.ANY)] * 2
    return pl.pallas_call(
        _inproj_gather_kernel,
        out_shape=[jax.ShapeDtypeStruct((N_TOK, w), F32) for w in out_w] + [cache_shape, cache_shape],
        grid_spec=pltpu.PrefetchScalarGridSpec(
            num_scalar_prefetch=2,
            grid=(N_TOK // TM,),
            in_specs=specs,
            out_specs=[tok(w) for w in out_w] + [cache_blk, cache_blk],
            scratch_shapes=[pltpu.VMEM((2, TM, D_MODEL), F32), pltpu.SemaphoreType.DMA((2,))],
        ),
        input_output_aliases={len(ins) - 2: len(out_w), len(ins) - 1: len(out_w) + 1},
        compiler_params=_params(("arbitrary",)),
        name="inproj_gather",
    )(*ins)


def _pair_masks(shape):
    lane = lax.broadcasted_iota(jnp.int32, shape, 1)
    return lane < NA_HD


def _ctx_attn_kernel(q_ref, k_ref, v_ref, o_ref):
    first = _pair_masks((SEQ, LANE))
    col = lambda p: slice(p * LANE, (p + 1) * LANE)
    heads = range(NA_HEADS)
    pairs = range(NA_HEADS // 2)
    kb = [k_ref[:, col(p)].astype(BF16) for p in pairs]
    vb = [v_ref[:, col(p)].astype(BF16) for p in pairs]
    qh = [jnp.where(first if h % 2 == 0 else jnp.logical_not(first), q_ref[:, col(h // 2)], 0.0).astype(BF16)
          for h in heads]
    s = [_dot_nt(qh[h], kb[h // 2]) for h in heads]
    e = [jnp.exp(s[h] - jnp.max(s[h], axis=-1, keepdims=True)) for h in heads]
    den = [jnp.sum(e[h], axis=-1, keepdims=True) for h in heads]
    o = [_dot(e[h].astype(BF16), vb[h // 2]) for h in heads]
    for p in pairs:
        o_ref[:, col(p)] = jnp.where(first, o[2 * p] / den[2 * p], o[2 * p + 1] / den[2 * p + 1])


def _ctx_attention(q, k, v):
    blk = pl.BlockSpec((SEQ, NA_W), lambda b: (b, 0))
    return pl.pallas_call(
        _ctx_attn_kernel,
        out_shape=jax.ShapeDtypeStruct((N_TOK, NA_W), F32),
        grid=(BATCH,),
        in_specs=[blk, blk, blk],
        out_specs=blk,
        compiler_params=_params(("parallel",)),
        name="ctx_attention",
    )(q, k, v)


NA_PAIR_GROUP = 4


def _na_kernel(q_ref, k_ref, v_ref, kc_ref, vc_ref, bias_ref, o_all_ref, o_ref, kb_ref, vb_ref, kcb_ref, vcb_ref):
    del o_all_ref
    r = pl.program_id(1)

    @pl.when(r == 0)
    def _():
        kb_ref[...] = k_ref[...].astype(BF16)
        vb_ref[...] = v_ref[...].astype(BF16)
        kcb_ref[...] = kc_ref[...].astype(BF16)
        vcb_ref[...] = vc_ref[...].astype(BF16)

    base = jnp.clip(r - NA_KH // 2, 0, GRID_ROWS - NA_KH)
    dr0 = base - r + (NA_KH - 1)
    row0 = pl.multiple_of(base * GRID_W, GRID_W)
    nwin = NA_KH * GRID_W
    first = _pair_masks((GRID_W, LANE))
    col = lambda p: slice(p * LANE, (p + 1) * LANE)
    for g0 in range(0, NA_HEADS // 2, NA_PAIR_GROUP):
        pairs = range(g0, g0 + NA_PAIR_GROUP)
        heads = range(2 * g0, 2 * (g0 + NA_PAIR_GROUP))
        qh = {h: jnp.where(first if h % 2 == 0 else jnp.logical_not(first), q_ref[:, col(h // 2)], 0.0).astype(BF16)
              for h in heads}
        kw = {p: kb_ref[pl.ds(row0, nwin), col(p)] for p in pairs}
        kc = {p: kcb_ref[:, col(p)] for p in pairs}
        vw = {p: vb_ref[pl.ds(row0, nwin), col(p)] for p in pairs}
        vc = {p: vcb_ref[:, col(p)] for p in pairs}
        s_loc = {h: _dot_nt(qh[h], kw[h // 2]) + bias_ref[h, dr0] for h in heads}
        s_ctx = {h: _dot_nt(qh[h], kc[h // 2]) for h in heads}
        mx = {h: jnp.maximum(jnp.max(s_loc[h], axis=-1, keepdims=True), jnp.max(s_ctx[h], axis=-1, keepdims=True))
              for h in heads}
        e_loc = {h: jnp.exp(s_loc[h] - mx[h]) for h in heads}
        e_ctx = {h: jnp.exp(s_ctx[h] - mx[h]) for h in heads}
        den = {h: jnp.sum(e_loc[h], axis=-1, keepdims=True) + jnp.sum(e_ctx[h], axis=-1, keepdims=True)
               for h in heads}
        o = {h: _dot(e_loc[h].astype(BF16), vw[h // 2]) + _dot(e_ctx[h].astype(BF16), vc[h // 2]) for h in heads}
        for p in pairs:
            o_ref[:, col(p)] = jnp.where(first, o[2 * p] / den[2 * p], o[2 * p + 1] / den[2 * p + 1])


def _na_attention(layer, q, k, v, cache_k, cache_v, bias, o_all):
    lat0 = N_CTX // DEC_SEQ
    qblk = pl.BlockSpec((GRID_W, NA_W), lambda b, r: (N_CTX // GRID_W + b * GRID_ROWS + r, 0))
    kvblk = pl.BlockSpec((DEC_SEQ, NA_W), lambda b, r: (lat0 + b, 0))
    cblk = pl.BlockSpec((None, None, PAST_LEN, NA_W), lambda b, r: (b, layer, 0, 0))
    return pl.pallas_call(
        _na_kernel,
        out_shape=jax.ShapeDtypeStruct((N_TOK, NA_W), F32),
        grid=(DEC_BATCH, GRID_ROWS),
        in_specs=[qblk, kvblk, kvblk, cblk, cblk,
                  pl.BlockSpec(bias.shape, lambda b, r: (0, 0, 0, 0)),
                  pl.BlockSpec(memory_space=pl.ANY)],
        out_specs=qblk,
        scratch_shapes=[pltpu.VMEM((DEC_SEQ, NA_W), BF16), pltpu.VMEM((DEC_SEQ, NA_W), BF16),
                        pltpu.VMEM((PAST_LEN, NA_W), BF16), pltpu.VMEM((PAST_LEN, NA_W), BF16)],
        input_output_aliases={6: 0},
        compiler_params=_params(("parallel", "arbitrary")),
        name="na_attention",
    )(q, k, v, cache_k, cache_v, bias, o_all)


N_DC = 2 * NA_KW - 1


def _na_bias_kernel(r_ref, o_ref):
    shape = (GRID_W, NA_KH * GRID_W)
    qc = lax.broadcasted_iota(jnp.int32, shape, 0)
    kc = lax.broadcasted_iota(jnp.int32, shape, 1) & (GRID_W - 1)
    ws = jnp.clip(qc - NA_KW // 2, 0, GRID_W - NA_KW)
    valid = (kc >= ws) & (kc < ws + NA_KW)
    dc = jnp.where(valid, kc - qc + (NA_KW - 1), -1)
    for d0 in range(NA_KH):
        acc = jnp.full(shape, -jnp.inf, F32)
        for d in range(N_DC):
            acc = jnp.where(dc == d, r_ref[d0, d:d + 1, :], acc)
        o_ref[d0] = acc


def _na_bias_table(rpb):
    win = jnp.stack([rpb[:, d0:d0 + NA_KH, :] for d0 in range(NA_KH)], axis=1)
    rexp = jnp.repeat(win.transpose(0, 1, 3, 2), GRID_W, axis=-1)
    rexp = jnp.pad(rexp, ((0, 0), (0, 0), (0, 32 - N_DC), (0, 0)))
    blk = lambda rows: pl.BlockSpec((None, NA_KH, rows, NA_KH * GRID_W), lambda h: (h, 0, 0, 0))
    return pl.pallas_call(
        _na_bias_kernel,
        out_shape=jax.ShapeDtypeStruct((NA_HEADS, NA_KH, GRID_W, NA_KH * GRID_W), F32),
        grid=(NA_HEADS,),
        in_specs=[blk(32)],
        out_specs=blk(GRID_W),
        compiler_params=_params(("parallel",)),
        name="na_bias",
    )(rexp)


def _rope_tables():
    t = np.arange(DEC_SEQ)
    half = GDN_DK // 2
    inv_freq = (np.float32(ROPE_BASE) ** (-np.arange(0, half, 2, dtype=np.float32) / np.float32(half)))
    ang_r = (t // GRID_W).astype(np.float32)[:, None] * inv_freq
    ang_c = (t % GRID_W).astype(np.float32)[:, None] * inv_freq
    cr, sr, cc, sc = np.cos(ang_r), np.sin(ang_r), np.cos(ang_c), np.sin(ang_c)
    z = np.zeros_like(sr)
    cos = np.concatenate([cr, cr, cc, cc], axis=1)
    s_up = np.concatenate([-sr, z, -sc, z], axis=1)
    s_dn = np.concatenate([z, sr, z, sc], axis=1)
    return (jnp.asarray(cos, F32), jnp.asarray(s_up, F32), jnp.asarray(s_dn, F32))


HALO = 8
PREP_CHUNKS = 4
N_STREAM = 2 * GDN_HEADS
LOG_COL = N_STREAM


def _gdn_prep_kernel(axial, seq, *refs):
    if axial:
        (zqkv_ref, zba_ref, cw_ref, alog_ref, dtb_ref, cos_ref, sup_ref, sdn_ref,
         a_ref, qk_ref, r_ref, qd_ref, kd_ref, gl_ref, xs_ref) = refs
    else:
        (zqkv_ref, zba_ref, cw_ref, alog_ref, dtb_ref,
         a_ref, qk_ref, r_ref, qd_ref, kd_ref, gl_ref, xs_ref) = refs
    width = 3 * GDN_W
    part = pl.program_id(1)

    @pl.when(part == 0)
    def _():
        xs_ref[0:HALO, :] = jnp.zeros((HALO, width), F32)
        xs_ref[HALO + seq:HALO + seq + HALO, :] = jnp.zeros((HALO, width), F32)
        xs_ref[HALO:HALO + seq, :] = zqkv_ref[...]

    ri = lax.broadcasted_iota(jnp.int32, (CHUNK, CHUNK), 0)
    ci = lax.broadcasted_iota(jnp.int32, (CHUNK, CHUNK), 1)
    incl = (ri >= ci, ri <= ci)
    strict = (ri > ci, ri < ci)
    lane = lax.broadcasted_iota(jnp.int32, (CHUNK, LANE), 1)

    def body(i, carry):
        r0 = pl.multiple_of((part * PREP_CHUNKS + i) * CHUNK, CHUNK)
        rows = pl.ds(r0, CHUNK)
        win = xs_ref[pl.ds(r0, CHUNK + 2 * HALO), :]
        acc = win[HALO:HALO + CHUNK, :] * cw_ref[CONV_K // 2:CONV_K // 2 + 1, :]
        for j in range(CONV_K):
            if j != CONV_K // 2:
                lo = HALO + j - CONV_K // 2
                acc = acc + win[lo:lo + CHUNK, :] * cw_ref[j:j + 1, :]
        act = _silu(acc)

        zba = zba_ref[rows, :]
        beta_all = 1.0 / (1.0 + jnp.exp(-zba))
        xa = zba + dtb_ref[...]
        softplus = jnp.maximum(xa, 0.0) + jnp.log(1.0 + jnp.exp(-jnp.abs(xa)))
        bg = jnp.where(lane < LOG_COL, beta_all, -jnp.exp(alog_ref[...]) * softplus)

        slabs = [act[:, s * GDN_DK:(s + 1) * GDN_DK] for s in range(2 * GDN_HEADS)]
        ssq = [jnp.sum(x * x, axis=-1, keepdims=True) for x in slabs]
        slabs = [x * lax.rsqrt(s + EPS) for x, s in zip(slabs, ssq)]
        if axial:
            cos, s_up, s_dn = cos_ref[rows, :], sup_ref[rows, :], sdn_ref[rows, :]
            up = [pltpu.roll(x, GDN_DK - GDN_DK // 4, 1) for x in slabs]
            dn = [pltpu.roll(x, GDN_DK // 4, 1) for x in slabs]
            slabs = [x * cos + u * s_up + w * s_dn for x, u, w in zip(slabs, up, dn)]
        qs = [x * (GDN_DK ** -0.5) for x in slabs[:GDN_HEADS]]
        ks = slabs[GDN_HEADS:]
        kbfs = [k.astype(BF16) for k in ks]
        kks = [_dot_nt(kb_, kb_) for kb_ in kbfs]
        qks = [_dot_nt(q.astype(BF16), kb_) for q, kb_ in zip(qs, kbfs)]
        heads = [(qs[hd], ks[hd], act[:, 2 * GDN_W + hd * GDN_DK:2 * GDN_W + (hd + 1) * GDN_DK], kks[hd], qks[hd])
                 for hd in range(GDN_HEADS)]

        gcums = [_dot_exact_lhs(incl[d].astype(BF16), bg) for d in range(2)]
        gcum_ts = [g.T for g in gcums]
        for d in range(2):
            gcum, gcum_t = gcums[d], gcum_ts[d]
            last = gcum[0:1, :] if d else gcum[CHUNK - 1:CHUNK, :]
            e_in = jnp.exp(gcum)
            e_out = jnp.exp(last - gcum)
            gl_ref[i, d:d + 1, :] = jnp.exp(last)
            for hd in range(GDN_HEADS):
                q, k, v, kk, qk = heads[hd]
                cb = d * GDN_HEADS + hd
                cg = LOG_COL + cb
                beta = bg[:, cb:cb + 1]
                eg = e_in[:, cg:cg + 1]
                el = e_out[:, cg:cg + 1]
                diff = gcum[:, cg:cg + 1] - gcum_t[cg:cg + 1, :]
                decay = jnp.where(incl[d], jnp.exp(jnp.where(incl[d], diff, 0.0)), 0.0)
                a_ref[d, i, hd] = jnp.where(strict[d], beta * kk * decay, 0.0)
                qk_ref[d, i, hd] = (qk * decay).astype(BF16)
                kb = k * beta
                r_ref[d, i, hd, :, 0:GDN_DK] = (v * beta).astype(BF16)
                r_ref[d, i, hd, :, GDN_DK:2 * GDN_DK] = (kb * eg).astype(BF16)
                qd_ref[d, i, hd] = (q * eg).astype(BF16)
                kd_ref[d, i, hd] = (k * el).astype(BF16)
        return carry

    lax.fori_loop(0, PREP_CHUNKS, body, 0)


def _gdn_prep(axial, zqkv, zba, conv_w, a_log, dt_bias, rope):
    seq = DEC_SEQ if axial else SEQ
    nseq = DEC_BATCH if axial else BATCH
    nparts = seq // (PREP_CHUNKS * CHUNK)
    nchunk = nseq * seq // CHUNK
    blk0 = N_CTX // DEC_SEQ if axial else 0
    tok = lambda w: pl.BlockSpec((seq, w), lambda b, p: (blk0 + b, 0))
    full = lambda a: pl.BlockSpec(a.shape, lambda b, p: (0,) * a.ndim)
    ins = [zqkv, zba, conv_w, a_log, dt_bias]
    specs = [tok(3 * GDN_W), tok(LANE), full(conv_w), full(a_log), full(dt_bias)]
    if axial:
        ins += list(rope)
        specs += [full(t) for t in rope]
    tile = lambda w: pl.BlockSpec((2, PREP_CHUNKS, GDN_HEADS, CHUNK, w), lambda b, p: (0, b * nparts + p, 0, 0, 0))
    shape = lambda w, dt: jax.ShapeDtypeStruct((2, nchunk, GDN_HEADS, CHUNK, w), dt)
    return pl.pallas_call(
        functools.partial(_gdn_prep_kernel, axial, seq),
        out_shape=[shape(CHUNK, F32), shape(CHUNK, BF16), shape(2 * GDN_DK, BF16),
                   shape(GDN_DK, BF16), shape(GDN_DK, BF16),
                   jax.ShapeDtypeStruct((nchunk, 2, LANE), F32)],
        grid=(nseq, nparts),
        in_specs=specs,
        out_specs=[tile(CHUNK), tile(CHUNK), tile(2 * GDN_DK), tile(GDN_DK), tile(GDN_DK),
                   pl.BlockSpec((PREP_CHUNKS, 2, LANE), lambda b, p: (b * nparts + p, 0, 0))],
        scratch_shapes=[pltpu.VMEM((seq + 2 * HALO, 3 * GDN_W), F32)],
        compiler_params=_params(("parallel", "arbitrary")),
        name="gdn_prep_lat" if axial else "gdn_prep_ctx",
    )(*ins)


TRI_BLK = 8
TRI_ELEMS = CHUNK * CHUNK


def _tri_inverse_kernel(groups_per_dir, a_ref, o_ref, at_ref, tt_ref):
    backward = pl.program_id(0) >= groups_per_dir
    for blk in range(TRI_ELEMS // LANE):
        cols = slice(blk * LANE, (blk + 1) * LANE)
        at_ref[cols, :] = a_ref[:, cols].T
    tt_ref[...] = jnp.zeros(tt_ref.shape, F32)
    nblk = CHUNK // TRI_BLK

    def substitute(mirror):
        phys = (lambda idx: CHUNK - 1 - idx) if mirror else (lambda idx: idx)
        for ib in range(nblk):
            c_lo, width = ((nblk - 1 - ib) * TRI_BLK if mirror else 0), (ib + 1) * TRI_BLK

            def row_body(ii, carry, ib=ib, c_lo=c_lo, width=width):
                i = phys(ib * TRI_BLK + ii)
                cidx = lax.broadcasted_iota(jnp.int32, (width, LANE), 0) + c_lo
                acc = (cidx == i).astype(F32)
                for jb in range(ib + 1):
                    w_j = (jb + 1) * TRI_BLK
                    cj_lo = (nblk - 1 - jb) * TRI_BLK if mirror else 0
                    sub = slice(cj_lo - c_lo, cj_lo - c_lo + w_j)
                    part = acc[sub]
                    for jl in range(jb * TRI_BLK, (jb + 1) * TRI_BLK):
                        j = phys(jl)
                        arow = at_ref[pl.ds(i * CHUNK + j, 1), :]
                        part = part - arow * tt_ref[j * CHUNK + cj_lo:j * CHUNK + cj_lo + w_j, :]
                    acc = part if w_j == width else (
                        jnp.concatenate([part, acc[w_j:]], axis=0) if not mirror
                        else jnp.concatenate([acc[:width - w_j], part], axis=0))
                tt_ref[pl.ds(pl.multiple_of(i * CHUNK + c_lo, TRI_BLK), width), :] = acc
                return carry

            lax.fori_loop(0, TRI_BLK, row_body, 0)

    @pl.when(jnp.logical_not(backward))
    def _():
        substitute(False)

    @pl.when(backward)
    def _():
        substitute(True)

    for blk in range(TRI_ELEMS // LANE):
        cols = slice(blk * LANE, (blk + 1) * LANE)
        o_ref[:, cols] = tt_ref[cols, :].T.astype(BF16)


def _tri_inverse(a):
    nprob = a.shape[1] * a.shape[2]
    groups_per_dir = nprob // LANE
    blk = pl.BlockSpec((LANE, TRI_ELEMS), lambda g: (g, 0))
    out = pl.pallas_call(
        functools.partial(_tri_inverse_kernel, groups_per_dir),
        out_shape=jax.ShapeDtypeStruct((2 * nprob, TRI_ELEMS), BF16),
        grid=(2 * groups_per_dir,),
        in_specs=[blk],
        out_specs=blk,
        scratch_shapes=[pltpu.VMEM((TRI_ELEMS, LANE), F32), pltpu.VMEM((TRI_ELEMS, LANE), F32)],
        compiler_params=_params(("parallel",)),
        name="tri_inverse",
    )(a.reshape(2 * nprob, TRI_ELEMS))
    return out.reshape(a.shape)


def _gdn_scan_kernel(nchunk, t_ref, qk_ref, r_ref, qd_ref, kd_ref, gl_ref, s0_ref, *rest):
    o_ref, sfin_ref, state_ref = rest[-3:]
    state_ref[...] = s0_ref[...]
    streams = [(d, hd) for d in range(2) for hd in range(GDN_HEADS)]

    def body(c, carry):
        cc = (c, nchunk - 1 - c)
        uws = [_dot(t_ref[d, cc[d], hd], r_ref[d, cc[d], hd]).astype(BF16) for d, hd in streams]
        mbs = [_dot_tn(kd_ref[d, cc[d], hd], uw) for (d, hd), uw in zip(streams, uws)]
        qos = [_dot(qk_ref[d, cc[d], hd], uw) for (d, hd), uw in zip(streams, uws)]
        sts = [state_ref[d * GDN_HEADS + hd] for d, hd in streams]
        sbs = [s.astype(BF16) for s in sts]
        for (d, hd), qo, sb in zip(streams, qos, sbs):
            rows = pl.ds(pl.multiple_of(cc[d] * CHUNK, CHUNK), CHUNK)
            qp = (qd_ref[d, cc[d], hd].astype(F32) - qo[:, GDN_DK:]).astype(BF16)
            o_ref[d, rows, hd * GDN_DK:(hd + 1) * GDN_DK] = _dot(qp, sb) + qo[:, :GDN_DK]
        for (d, hd), mb, s, sb in zip(streams, mbs, sts, sbs):
            sidx = d * GDN_HEADS + hd
            gl = gl_ref[cc[d]][d:d + 1, LOG_COL + sidx:LOG_COL + sidx + 1]
            state_ref[sidx] = s * gl - _dot(mb[:, GDN_DK:].astype(BF16), sb) + mb[:, :GDN_DK]
        return carry

    lax.fori_loop(0, nchunk, body, 0)
    sfin_ref[...] = state_ref[...]


def _gdn_scan(seq, nseq, tinv, qk, r, qd, kd, gl, s0, o_all=None):
    nc = seq // CHUNK
    blk0 = 0 if o_all is None else N_CTX // seq
    tile = lambda w: pl.BlockSpec((2, nc, GDN_HEADS, CHUNK, w), lambda b: (0, b, 0, 0, 0))
    sblk = pl.BlockSpec((None, N_STREAM, GDN_DK, GDN_DK), lambda b: (b, 0, 0, 0))
    oblk = pl.BlockSpec((2, seq, GDN_W), lambda b: (0, blk0 + b, 0))
    ins = [tinv, qk, r, qd, kd, gl, s0]
    specs = [tile(CHUNK), tile(CHUNK), tile(2 * GDN_DK), tile(GDN_DK), tile(GDN_DK),
             pl.BlockSpec((nc, 2, LANE), lambda b: (b, 0, 0)), sblk]
    aliases = {}
    if o_all is not None:
        ins.append(o_all)
        specs.append(pl.BlockSpec(memory_space=pl.ANY))
        aliases = {len(ins) - 1: 0}
    return pl.pallas_call(
        functools.partial(_gdn_scan_kernel, nc),
        out_shape=[jax.ShapeDtypeStruct((2, N_TOK, GDN_W), F32),
                   jax.ShapeDtypeStruct((nseq, N_STREAM, GDN_DK, GDN_DK), F32)],
        grid=(nseq,),
        in_specs=specs,
        out_specs=[oblk, sblk],
        scratch_shapes=[pltpu.VMEM((N_STREAM, GDN_DK, GDN_DK), F32)],
        input_output_aliases=aliases,
        compiler_params=_params(("parallel",)),
        name="gdn_scan_lat" if seq == DEC_SEQ else "gdn_scan_ctx",
    )(*ins)


def _first_max(vals):
    sel = []
    taken = None
    for a, va in enumerate(vals):
        is_max = None
        for b, vb in enumerate(vals):
            if a == b:
                continue
            c = va >= vb
            is_max = c if is_max is None else (is_max & c)
        if taken is not None:
            is_max = is_max & jnp.logical_not(taken)
        sel.append(is_max)
        taken = is_max if taken is None else (taken | is_max)
    return sel


PAIR_SLOTS = ((0, 1), (0, 2), (0, 3), (1, 3), (1, 2), (3, 2))
N_CLASS = N_GROUPS * len(PAIR_SLOTS)
CLASS_EXPERTS = tuple((g * EXP_PER_GROUP + a, g * EXP_PER_GROUP + b)
                      for g in range(N_GROUPS) for a, b in PAIR_SLOTS)


def _route_rows(lt):
    gl = [lt[a:a + 1, :] for a in range(N_GROUPS)]
    gsel = _first_max(gl)
    gmax = functools.reduce(jnp.maximum, gl)
    gden = functools.reduce(lambda x, y: x + y, [jnp.exp(x - gmax) for x in gl])
    g_w = 1.0 / gden
    el = []
    for kx in range(EXP_PER_GROUP):
        acc = None
        for a in range(N_GROUPS):
            row = N_GROUPS + a * EXP_PER_GROUP + kx
            term = jnp.where(gsel[a], lt[row:row + 1, :], 0.0)
            acc = term if acc is None else acc + term
        el.append(acc)
    emax = functools.reduce(jnp.maximum, el)
    ee = [jnp.exp(x - emax) for x in el]
    eden = functools.reduce(lambda x, y: x + y, ee)
    ep = [x / eden for x in ee]
    top1 = _first_max(ep)
    ep2 = [jnp.where(top1[kx], -1.0, ep[kx]) for kx in range(EXP_PER_GROUP)]
    top2 = _first_max(ep2)
    chosen = [top1[kx] | top2[kx] for kx in range(EXP_PER_GROUP)]
    wsum = functools.reduce(lambda x, y: x + y,
                            [jnp.where(chosen[kx], ep[kx], 0.0) for kx in range(EXP_PER_GROUP)])
    within = [jnp.where(chosen[kx], ep[kx] / wsum, 0.0) for kx in range(EXP_PER_GROUP)]
    cls = jnp.zeros(g_w.shape, jnp.int32)
    w_a = jnp.zeros(g_w.shape, F32)
    w_b = jnp.zeros(g_w.shape, F32)
    for g in range(N_GROUPS):
        for kx, (a, b) in enumerate(PAIR_SLOTS):
            hit = gsel[g] & chosen[a] & chosen[b]
            cls = jnp.where(hit, g * len(PAIR_SLOTS) + kx, cls)
            w_a = jnp.where(hit, g_w * within[a], w_a)
            w_b = jnp.where(hit, g_w * within[b], w_b)
    return cls, w_a, w_b


H2W = D_MODEL + LANE
COL_WA = D_MODEL
COL_WB = D_MODEL + 1


def _outproj_kernel(nx, idx_ref, *refs):
    x = _token_tile(nx, refs)
    (ona_ref, og_ref, zg_ref, mod_ref, wo_ref, gng_ref, n2g_ref, wr_ref, br_ref,
     x1_ref, h2x_ref, cls_ref) = refs[nx:]
    del idx_ref
    m = mod_ref[...]
    og = og_ref[0] + og_ref[1]
    zg = zg_ref[...]
    parts = []
    for hd in range(GDN_HEADS):
        cols = slice(hd * GDN_DK, (hd + 1) * GDN_DK)
        oh = og[:, cols]
        oh = oh * lax.rsqrt(jnp.mean(oh * oh, axis=-1, keepdims=True) + EPS) * gng_ref[...]
        parts.append((oh * _silu(zg[:, cols])).astype(BF16))
    mix = _dot(ona_ref[...].astype(BF16), wo_ref[0:NA_W, :])
    for hd in range(GDN_HEADS):
        r0 = NA_W + hd * GDN_DK
        mix = mix + _dot(parts[hd], wo_ref[r0:r0 + GDN_DK, :])
    x1 = x + m[2:3, :] * mix
    x1_ref[...] = x1
    y = x1 * lax.rsqrt(jnp.mean(x1 * x1, axis=-1, keepdims=True) + EPS) * n2g_ref[...]
    h2 = y * (1.0 + m[4:5, :]) + m[3:4, :]
    h2x_ref[:, 0:D_MODEL] = h2.astype(BF16).astype(F32)
    logits = _dot3(h2, wr_ref[...]) + br_ref[...]
    cls, w_a, w_b = _route_rows(logits.T)
    rid = lax.broadcasted_iota(jnp.int32, (8, TM), 0)
    wt = jnp.where(rid == COL_WA - D_MODEL, w_a, jnp.where(rid == COL_WB - D_MODEL, w_b, 0.0))
    wt = jnp.concatenate([wt, jnp.zeros((LANE - 8, TM), F32)], axis=0)
    h2x_ref[:, D_MODEL:H2W] = wt.T
    cls_ref[...] = jnp.broadcast_to(cls, (8, TM))


def _outproj(tile_mod, x, o_na, o_gdn, zg, mod, w_out, gng, n2g, w_r, b_r):
    tok = lambda w: pl.BlockSpec((TM, w), lambda t, idx: (t, 0))
    full = lambda a: pl.BlockSpec(a.shape, lambda t, idx: (0,) * a.ndim)
    xs = list(x) if isinstance(x, tuple) else [x]
    return pl.pallas_call(
        functools.partial(_outproj_kernel, len(xs)),
        out_shape=[jax.ShapeDtypeStruct((N_TOK, D_MODEL), F32),
                   jax.ShapeDtypeStruct((N_TOK, H2W), F32),
                   jax.ShapeDtypeStruct((N_TOK // TM, 8, TM), jnp.int32)],
        grid_spec=pltpu.PrefetchScalarGridSpec(
            num_scalar_prefetch=1,
            grid=(N_TOK // TM,),
            in_specs=_token_specs(x) + [
                tok(NA_W),
                pl.BlockSpec((2, TM, GDN_W), lambda t, idx: (0, t, 0)), tok(GDN_W),
                pl.BlockSpec((None, 6, D_MODEL), lambda t, idx: (idx[t], 0, 0)),
                full(w_out), full(gng), full(n2g), full(w_r), full(b_r),
            ],
            out_specs=[tok(D_MODEL), tok(H2W), pl.BlockSpec((None, 8, TM), lambda t, idx: (t, 0, 0))],
        ),
        compiler_params=_params(("parallel",)),
        name="outproj",
    )(tile_mod, *xs, o_na, o_gdn, zg, mod, w_out, gng, n2g, w_r, b_r)


TM_X = 256
N_XTILE = N_TOK // TM_X + N_CLASS
N_SLOT = N_XTILE * TM_X
CLS_ROWS = N_TOK // LANE
TAB_EA, TAB_EB, TAB_USED, TAB_LAST = 0, 1, 2, 3


def _route_pos_kernel(cls_ref, slot_ref, tab_ref):
    cls = cls_ref[...]
    li = lax.broadcasted_iota(jnp.int32, (LANE, LANE), 0)
    lj = lax.broadcasted_iota(jnp.int32, (LANE, LANE), 1)
    before_lane = (li < lj).astype(BF16)
    ri = lax.broadcasted_iota(jnp.int32, (CLS_ROWS, CLS_ROWS), 0)
    rj = lax.broadcasted_iota(jnp.int32, (CLS_ROWS, CLS_ROWS), 1)
    before_row = (rj < ri).astype(BF16)
    lane = lax.broadcasted_iota(jnp.int32, (1, LANE), 1)
    tile_start = lane * TM_X
    off = jnp.zeros((1, 1), jnp.int32)
    slot = jnp.zeros(cls.shape, jnp.int32)
    tab_a = jnp.zeros((1, LANE), jnp.int32)
    tab_b = jnp.zeros((1, LANE), jnp.int32)
    tab_last = jnp.full((1, LANE), -1, jnp.int32)
    for c in range(N_CLASS):
        hit = cls == c
        one = hit.astype(F32)
        in_row = _dot(one.astype(BF16), before_lane)
        row_tot = jnp.broadcast_to(jnp.sum(one, axis=1, keepdims=True), one.shape)
        rank = in_row + _dot(before_row, row_tot.astype(BF16))
        count = jnp.sum(row_tot[:, 0:1], axis=0, keepdims=True).astype(jnp.int32)
        slot = jnp.where(hit, off + rank.astype(jnp.int32), slot)
        nxt = off + (((count + (TM_X - 1)) >> 8) << 8)
        mine = (tile_start >= off) & (tile_start < nxt)
        tab_a = jnp.where(mine, CLASS_EXPERTS[c][0], tab_a)
        tab_b = jnp.where(mine, CLASS_EXPERTS[c][1], tab_b)
        tab_last = jnp.where((lane == c) & (nxt > off), (nxt >> 8) - 1, tab_last)
        off = nxt
    slot_ref[...] = slot
    row = lax.broadcasted_iota(jnp.int32, (8, LANE), 0)
    used = jnp.broadcast_to(off >> 8, (8, LANE))
    tab_ref[...] = jnp.where(row == TAB_EA, tab_a, jnp.where(row == TAB_EB, tab_b,
                                                               jnp.where(row == TAB_LAST, tab_last, used)))


def _route_positions(cls):
    assert TM_X == 256
    return pl.pallas_call(
        _route_pos_kernel,
        out_shape=[jax.ShapeDtypeStruct((CLS_ROWS, LANE), jnp.int32),
                   jax.ShapeDtypeStruct((8, LANE), jnp.int32)],
        name="route_positions",
    )(cls)


ROW_UNROLL = 8


def _permute_kernel(slot_ref, tab_ref, h_ref, xs_ref, buf_ref, zero_ref, sem_ref, zsem_ref):
    t = pl.program_id(0)
    cur = t % 2

    @pl.when(t == 0)
    def _():
        zero_ref[...] = jnp.zeros(zero_ref.shape, F32)

        def tile_copy(c):
            return pltpu.make_async_copy(
                zero_ref, xs_ref.at[pl.ds(pl.multiple_of(tab_ref[TAB_LAST, c] * TM_X, TM_X), TM_X), :], zsem_ref.at[0])

        for c in range(N_CLASS):
            @pl.when(tab_ref[TAB_LAST, c] >= 0)
            def _():
                tile_copy(c).start()
        for c in range(N_CLASS):
            @pl.when(tab_ref[TAB_LAST, c] >= 0)
            def _():
                tile_copy(c).wait()

    buf_ref[cur] = h_ref[...]

    def issue(i, carry):
        _row_copy(buf_ref.at[cur], i, xs_ref, slot_ref[t * TM + i], sem_ref.at[cur]).start()
        return carry

    lax.fori_loop(0, TM, issue, 0, unroll=ROW_UNROLL)

    def drain(which):
        def wait(i, carry):
            _row_copy(buf_ref.at[which], 0, xs_ref, 0, sem_ref.at[which]).wait()
            return carry
        lax.fori_loop(0, TM, wait, 0, unroll=ROW_UNROLL)

    @pl.when(t > 0)
    def _():
        drain(1 - cur)

    @pl.when(t == pl.num_programs(0) - 1)
    def _():
        drain(cur)


def _permute(slot, tab, h2x):
    assert TM_X == TM
    return pl.pallas_call(
        _permute_kernel,
        out_shape=jax.ShapeDtypeStruct((N_SLOT, H2W), F32),
        grid_spec=pltpu.PrefetchScalarGridSpec(
            num_scalar_prefetch=2,
            grid=(N_TOK // TM,),
            in_specs=[pl.BlockSpec((TM, H2W), lambda t, s, tab: (t, 0))],
            out_specs=pl.BlockSpec(memory_space=pl.ANY),
            scratch_shapes=[pltpu.VMEM((2, TM, H2W), F32), pltpu.VMEM((TM_X, H2W), F32),
                            pltpu.SemaphoreType.DMA((2,)), pltpu.SemaphoreType.DMA((1,))],
        ),
        compiler_params=_params(("arbitrary",)),
        name="moe_permute",
    )(slot, tab, h2x)


def _experts_kernel(tab_ref, xs_ref, w1a_ref, w3a_ref, w2a_ref, w1b_ref, w3b_ref, w2b_ref, ys_ref,
                    c1a_ref, c3a_ref, c2a_ref, c1b_ref, c3b_ref, c2b_ref):
    t = pl.program_id(0)
    prev = jnp.maximum(t - 1, 0)

    @pl.when(t < tab_ref[TAB_USED, 0])
    def _():
        @pl.when((t == 0) | (tab_ref[TAB_EA, t] != tab_ref[TAB_EA, prev]))
        def _():
            c1a_ref[...] = w1a_ref[...].astype(BF16)
            c3a_ref[...] = w3a_ref[...].astype(BF16)
            c2a_ref[...] = w2a_ref[...].astype(BF16)

        @pl.when((t == 0) | (tab_ref[TAB_EB, t] != tab_ref[TAB_EB, prev]))
        def _():
            c1b_ref[...] = w1b_ref[...].astype(BF16)
            c3b_ref[...] = w3b_ref[...].astype(BF16)
            c2b_ref[...] = w2b_ref[...].astype(BF16)

        x = xs_ref[:, 0:D_MODEL].astype(BF16)
        gates = xs_ref[:, D_MODEL:H2W]
        w_a = gates[:, COL_WA - D_MODEL:COL_WA - D_MODEL + 1]
        w_b = gates[:, COL_WB - D_MODEL:COL_WB - D_MODEL + 1]
        a1 = _dot(x, c1a_ref[...])
        a3 = _dot(x, c3a_ref[...])
        b1 = _dot(x, c1b_ref[...])
        b3 = _dot(x, c3b_ref[...])
        sa = (_silu(a1) * a3 * w_a).astype(BF16)
        sb = (_silu(b1) * b3 * w_b).astype(BF16)
        ys_ref[...] = _dot(sa, c2a_ref[...]) + _dot(sb, c2b_ref[...])


def _experts(layer, tab, xs, w1, w3, w2):
    last = lambda t, tab: jnp.minimum(t, tab[TAB_USED, 0] - 1)
    tile = lambda w: pl.BlockSpec((TM_X, w), lambda t, tab: (last(t, tab), 0))
    wspec = lambda shape, row: pl.BlockSpec((None, None) + shape,
                                            lambda t, tab: (layer, tab[row, last(t, tab)], 0, 0))
    up, down = (D_MODEL, EXPERT_FF), (EXPERT_FF, D_MODEL)
    return pl.pallas_call(
        _experts_kernel,
        out_shape=jax.ShapeDtypeStruct((N_SLOT, D_MODEL), F32),
        grid_spec=pltpu.PrefetchScalarGridSpec(
            num_scalar_prefetch=1,
            grid=(N_XTILE,),
            in_specs=[tile(H2W),
                      wspec(up, TAB_EA), wspec(up, TAB_EA), wspec(down, TAB_EA),
                      wspec(up, TAB_EB), wspec(up, TAB_EB), wspec(down, TAB_EB)],
            out_specs=tile(D_MODEL),
            scratch_shapes=[pltpu.VMEM(up, BF16), pltpu.VMEM(up, BF16), pltpu.VMEM(down, BF16),
                            pltpu.VMEM(up, BF16), pltpu.VMEM(up, BF16), pltpu.VMEM(down, BF16)],
        ),
        compiler_params=_params(("arbitrary",)),
        name="moe_experts",
    )(tab, xs, w1, w3, w2, w1, w3, w2)


def _unpermute_kernel(idx_ref, slot_ref, x1_ref, mod_ref, ys_ref, *rest):
    outs, (buf_ref, sem_ref) = rest[:-2], rest[-2:]
    del idx_ref
    t = pl.program_id(0)
    cur = t % 2

    def fetch(tile, which):
        def issue(i, carry):
            _row_copy(ys_ref, slot_ref[tile * TM + i], buf_ref.at[which], i, sem_ref.at[which]).start()
            return carry
        lax.fori_loop(0, TM, issue, 0, unroll=ROW_UNROLL)

    @pl.when(t == 0)
    def _():
        fetch(0, 0)

    @pl.when(t + 1 < pl.num_programs(0))
    def _():
        fetch(t + 1, 1 - cur)

    def wait(i, carry):
        _row_copy(ys_ref, 0, buf_ref.at[cur], 0, sem_ref.at[cur]).wait()
        return carry

    lax.fori_loop(0, TM, wait, 0, unroll=ROW_UNROLL)
    y = x1_ref[...] + mod_ref[5:6, :] * buf_ref[cur]

    @pl.when(t < N_CTX // TM)
    def _():
        outs[0][...] = y

    @pl.when(t >= N_CTX // TM)
    def _():
        outs[1][...] = y


def _unpermute(tile_mod, slot, x1, mod, ys):
    tok = pl.BlockSpec((TM, D_MODEL), lambda t, idx, s: (t, 0))
    out_shape = [jax.ShapeDtypeStruct((N_CTX, D_MODEL), F32), jax.ShapeDtypeStruct((N_LAT, D_MODEL), F32)]
    out_specs = _token_specs((None, None))
    return pl.pallas_call(
        _unpermute_kernel,
        out_shape=out_shape,
        grid_spec=pltpu.PrefetchScalarGridSpec(
            num_scalar_prefetch=2,
            grid=(N_TOK // TM,),
            in_specs=[tok, pl.BlockSpec((None, 6, D_MODEL), lambda t, idx, s: (idx[t], 0, 0)),
                      pl.BlockSpec(memory_space=pl.ANY)],
            out_specs=out_specs,
            scratch_shapes=[pltpu.VMEM((2, TM, D_MODEL), F32), pltpu.SemaphoreType.DMA((2,))],
        ),
        compiler_params=_params(("arbitrary",)),
        name="moe_unpermute",
    )(tile_mod, slot, x1, mod, ys)


def _tile_mod_index(tile):
    t = np.arange(N_TOK // tile) * tile
    return jnp.asarray(np.where(t < N_CTX, 0, 1 + (t - N_CTX) // DEC_SEQ), jnp.int32)


def kernel(x_prompt, x_sample, c, cache_k, cache_v, state_ssm, c_ctx, ada_w, ada_b, norm1_g, norm2_g, w_in, w_out, na_qn_g, na_kn_g, na_rpb, gdn_conv_w, gdn_a_log, gdn_dt_bias, gdn_norm_g, moe_w_rg, moe_b_rg, moe_w_re, moe_b_re, moe_w1, moe_w3, moe_w2):
    x = (x_prompt.reshape(N_CTX, D_MODEL), x_sample.reshape(N_LAT, D_MODEL))
    cv = jnp.concatenate([c_ctx[None, :], c, jnp.zeros((N_MOD_PAD - N_MOD, D_MODEL), F32)], axis=0)
    mod_all = _modulation(cv, ada_w, ada_b).reshape(DEPTH, N_MOD_PAD, 6, D_MODEL)

    idx_tm = _tile_mod_index(TM)
    hh = np.arange(NA_W) // NA_HD
    ones_bd = jnp.asarray(hh[:, None] == hh[None, :], BF16)
    rope = _rope_tables()
    cache_k4 = cache_k.reshape(DEC_BATCH, DEPTH, PAST_LEN, NA_W)
    cache_v4 = cache_v.reshape(DEC_BATCH, DEPTH, PAST_LEN, NA_W)
    lane_pad = lambda a, at: jnp.zeros((1, LANE), F32).at[0, at:at + a.size].set(a.reshape(-1))
    zeros_state = jnp.zeros((BATCH, 2 * GDN_HEADS, GDN_DK, GDN_DK), F32)

    caches, ss, pending = None, [], None
    for l in range(DEPTH):
        mod = mod_all[l]
        w_main = w_in[l, :, :N_MAIN].astype(BF16)
        w_ba = jnp.zeros((D_MODEL, LANE), BF16).at[:, :N_IN - N_MAIN].set(w_in[l, :, N_MAIN:].astype(BF16))
        qg = jnp.tile(na_qn_g[l], NA_HEADS)[None, :]
        kg = jnp.tile(na_kn_g[l], NA_HEADS)[None, :]
        proj = (mod, norm1_g[l][None, :], w_main, w_ba, ones_bd, qg, kg, caches)
        if pending is None:
            q, k, v, zqkv, zg, zba, *caches = _inproj(l, idx_tm, x, *proj)
        else:
            x, q, k, v, zqkv, zg, zba, *caches = _inproj_gather(l, idx_tm, *pending, *proj)

        bias = _na_bias_table(na_rpb[l])
        o_na = _na_attention(l, q, k, v, cache_k4, cache_v4, bias, _ctx_attention(q, k, v))

        conv_w = jnp.zeros((8, 3 * GDN_W), F32).at[:CONV_K].set(gdn_conv_w[l])
        a_row = lane_pad(gdn_a_log[l], 2 * GDN_HEADS)
        dt_row = lane_pad(gdn_dt_bias[l], 2 * GDN_HEADS)
        a_c, *ops_c = _gdn_prep(False, zqkv, zba, conv_w, a_row, dt_row, None)
        a_l, *ops_l = _gdn_prep(True, zqkv, zba, conv_w, a_row, dt_row, rope)
        og_c, s_ctx = _gdn_scan(SEQ, BATCH, _tri_inverse(a_c), *ops_c, zeros_state)
        s0_lat = state_ssm[:, l].reshape(DEC_BATCH, N_STREAM, GDN_DK, GDN_DK)
        o_gdn, _ = _gdn_scan(DEC_SEQ, DEC_BATCH, _tri_inverse(a_l), *ops_l, s0_lat, og_c)

        w_r = jnp.zeros((D_MODEL, LANE), F32).at[:, :N_GROUPS].set(moe_w_rg[l])
        w_r = w_r.at[:, N_GROUPS:N_GROUPS + N_EXPERTS].set(moe_w_re[l])
        b_r = lane_pad(jnp.concatenate([moe_b_rg[l], moe_b_re[l]]), 0)
        x1, h2x, cls = _outproj(idx_tm, x, o_na, o_gdn, zg, mod, w_out[l].astype(BF16),
                                gdn_norm_g[l][None, :], norm2_g[l][None, :], w_r, b_r)
        slot, tab = _route_positions(cls[:, 0, :].reshape(CLS_ROWS, LANE))
        slot = slot.reshape(N_TOK)
        ys = _experts(l, tab, _permute(slot, tab, h2x), moe_w1, moe_w3, moe_w2)
        if l == DEPTH - 1:
            x = _unpermute(idx_tm, slot, x1, mod, ys)
        else:
            pending = (slot, x1, mod, ys)

        ss.append(s_ctx.reshape(BATCH, 2, GDN_HEADS, GDN_DK, GDN_DK))

    y_prompt = x[0].reshape(BATCH, SEQ, D_MODEL)
    y_sample = x[1].reshape(DEC_BATCH, DEC_SEQ, D_MODEL)
    new_k, new_v = (a.reshape(BATCH, DEPTH, SEQ, NA_HEADS, NA_HD) for a in caches)
    return (y_prompt, y_sample, new_k, new_v, jnp.stack(ss, axis=1))
```

```python
import functools

import jax
import jax.numpy as jnp
import numpy as np
from jax import lax
from jax.experimental import pallas as pl
from jax.experimental.pallas import tpu as pltpu

F32 = jnp.float32
BF16 = jnp.bfloat16

D_MODEL = 1024
BATCH = 16
SEQ = 256
DEPTH = 2
DEC_BATCH = 8
DEC_SEQ = 1024
PAST_LEN = 512
GRID_W = 64
GRID_ROWS = DEC_SEQ // GRID_W
NA_HEADS = 8
NA_HD = 64
NA_W = NA_HEADS * NA_HD
NA_KH = 8
NA_KW = 16
GDN_HEADS = 4
GDN_DK = 128
GDN_W = GDN_HEADS * GDN_DK
CONV_K = 5
CHUNK = 64
ROPE_BASE = 10000.0
N_GROUPS = 4
EXP_PER_GROUP = 4
N_EXPERTS = 16
EXPERT_FF = 512
EPS = 1e-6
N_IN = 3 * NA_W + 4 * GDN_W + 4 * GDN_HEADS

N_CTX = BATCH * SEQ
N_LAT = DEC_BATCH * DEC_SEQ
N_TOK = N_CTX + N_LAT
N_MOD = 1 + DEC_BATCH
N_MOD_PAD = 16
LANE = 128
N_MAIN = 3 * NA_W + 4 * GDN_W
TM = 256
VMEM_LIMIT = 56 * 1024 * 1024


def _dot(a, b):
    return jnp.dot(a, b, preferred_element_type=F32)


def _dot_nt(a, b):
    return lax.dot_general(a, b, (((1,), (1,)), ((), ())), preferred_element_type=F32)


def _dot_tn(a, b):
    return lax.dot_general(a, b, (((0,), (0,)), ((), ())), preferred_element_type=F32)


def _split2(x):
    hi = x.astype(BF16)
    lo = (x - hi.astype(F32)).astype(BF16)
    return hi, lo


def _split3(x):
    hi = x.astype(BF16)
    r = x - hi.astype(F32)
    mid = r.astype(BF16)
    lo = (r - mid.astype(F32)).astype(BF16)
    return hi, mid, lo


def _dot3(a, b):
    ah, al = _split2(a)
    bh, bl = _split2(b)
    return _dot(ah, bh) + (_dot(ah, bl) + _dot(al, bh))


def _dot_exact_lhs(mask_bf16, x):
    hi, mid, lo = _split3(x)
    return _dot(mask_bf16, hi) + (_dot(mask_bf16, mid) + _dot(mask_bf16, lo))


def _silu(x):
    return x * (1.0 / (1.0 + jnp.exp(-x)))


def _params(sem):
    return pltpu.CompilerParams(dimension_semantics=sem, vmem_limit_bytes=VMEM_LIMIT)


def _mod_kernel(cv_ref, w_ref, b_ref, o_ref):
    cv = cv_ref[...]
    s = _silu(cv)
    o_ref[...] = jnp.dot(s, w_ref[...], preferred_element_type=F32,
                         precision=lax.Precision.HIGHEST) + b_ref[...]


def _modulation(cv, ada_w, ada_b):
    nblk = 1024
    return pl.pallas_call(
        _mod_kernel,
        out_shape=jax.ShapeDtypeStruct((DEPTH, N_MOD_PAD, 6 * D_MODEL), F32),
        grid=(DEPTH, 6 * D_MODEL // nblk),
        in_specs=[
            pl.BlockSpec((N_MOD_PAD, D_MODEL), lambda l, j: (0, 0)),
            pl.BlockSpec((None, D_MODEL, nblk), lambda l, j: (l, 0, j)),
            pl.BlockSpec((None, 1, nblk), lambda l, j: (l, 0, j)),
        ],
        out_specs=pl.BlockSpec((None, N_MOD_PAD, nblk), lambda l, j: (l, 0, j)),
        compiler_params=_params(("parallel", "parallel")),
        name="modulation",
    )(cv, ada_w, ada_b.reshape(DEPTH, 1, 6 * D_MODEL))


def _token_specs(x):
    if not isinstance(x, tuple):
        return [pl.BlockSpec((TM, D_MODEL), lambda t, *_: (t, 0))]
    return [pl.BlockSpec((TM, D_MODEL), lambda t, *_: (jnp.minimum(t, N_CTX // TM - 1), 0)),
            pl.BlockSpec((TM, D_MODEL), lambda t, *_: (jnp.maximum(t - N_CTX // TM, 0), 0))]


def _token_tile(nx, refs):
    if nx == 1:
        return refs[0][...]
    return jnp.where(pl.program_id(0) < N_CTX // TM, refs[0][...], refs[1][...])


def _inproj_kernel(nx, idx_ref, *refs):
    del idx_ref
    _inproj_body(_token_tile(nx, refs), refs[nx:nx + 7], refs[-8:])


def _inproj_body(x, params, outs):
    mod_ref, n1g_ref, win_ref, wba_ref, ones_ref, qg_ref, kg_ref = params
    q_ref, k_ref, v_ref, zqkv_ref, zg_ref, zba_ref, kcache_ref, vcache_ref = outs
    y = x * lax.rsqrt(jnp.mean(x * x, axis=-1, keepdims=True) + EPS) * n1g_ref[...]
    m = mod_ref[...]
    h = (y * (1.0 + m[1:2, :]) + m[0:1, :]).astype(BF16)
    z = _dot(h, win_ref[...])
    zba_ref[...] = _dot(h, wba_ref[...])
    ones = ones_ref[...]

    def head_rms(zz, g):
        hi, lo = _split2(zz * zz)
        ms = (_dot(hi, ones) + _dot(lo, ones)) * (1.0 / NA_HD)
        return zz * lax.rsqrt(ms + EPS) * g

    q_ref[...] = head_rms(z[:, 0:NA_W], qg_ref[...]) * (NA_HD ** -0.5)
    k = head_rms(z[:, NA_W:2 * NA_W], kg_ref[...])
    v = z[:, 2 * NA_W:3 * NA_W]
    k_ref[...] = k
    v_ref[...] = v
    zqkv_ref[...] = z[:, 3 * NA_W:3 * NA_W + 3 * GDN_W]
    zg_ref[...] = z[:, 3 * NA_W + 3 * GDN_W:N_MAIN]

    @pl.when(pl.program_id(0) < BATCH)
    def _():
        kcache_ref[...] = k
        vcache_ref[...] = v


def _inproj(layer, tile_mod, x, mod, n1g, w_main, w_ba, ones_bd, qg, kg, caches):
    assert TM == SEQ
    tok = lambda w: pl.BlockSpec((TM, w), lambda t, idx: (t, 0))
    full = lambda a: pl.BlockSpec(a.shape, lambda t, idx: (0,) * a.ndim)
    out_w = (NA_W, NA_W, NA_W, 3 * GDN_W, GDN_W, LANE)
    cache_blk = pl.BlockSpec((None, None, SEQ, NA_W), lambda t, idx: (jnp.minimum(t, BATCH - 1), layer, 0, 0))
    cache_shape = jax.ShapeDtypeStruct((BATCH, DEPTH, SEQ, NA_W), F32)
    xs = list(x) if isinstance(x, tuple) else [x]
    ins = [tile_mod, *xs, mod, n1g, w_main, w_ba, ones_bd, qg, kg]
    specs = _token_specs(x) + [pl.BlockSpec((None, 6, D_MODEL), lambda t, idx: (idx[t], 0, 0)),
                               full(n1g), full(w_main), full(w_ba), full(ones_bd), full(qg), full(kg)]
    aliases = {}
    if caches is not None:
        aliases = {len(ins): len(out_w), len(ins) + 1: len(out_w) + 1}
        ins += list(caches)
        specs += [pl.BlockSpec(memory_space=pl.ANY)] * 2
    return pl.pallas_call(
        functools.partial(_inproj_kernel, len(xs)),
        out_shape=[jax.ShapeDtypeStruct((N_TOK, w), F32) for w in out_w] + [cache_shape, cache_shape],
        grid_spec=pltpu.PrefetchScalarGridSpec(
            num_scalar_prefetch=1,
            grid=(N_TOK // TM,),
            in_specs=specs,
            out_specs=[tok(w) for w in out_w] + [cache_blk, cache_blk],
        ),
        input_output_aliases=aliases,
        compiler_params=_params(("arbitrary",)),
        name="inproj",
    )(*ins)


def _row_copy(src, src_row, dst, dst_row, sem):
    return pltpu.make_async_copy(src.at[pl.ds(src_row, 1), :], dst.at[pl.ds(dst_row, 1), :], sem)


def _inproj_gather_kernel(idx_ref, slot_ref, x1_ref, modp_ref, ys_ref, *refs):
    del idx_ref
    params, outs, (buf_ref, sem_ref) = refs[:7], refs[-11:-2], refs[-2:]
    t = pl.program_id(0)
    last = pl.num_programs(0) - 1
    cur = t % 2

    def fetch(tile, which):
        for i in range(TM):
            _row_copy(ys_ref, slot_ref[tile * TM + i], buf_ref.at[which], i, sem_ref.at[which]).start()

    def drain(which):
        for i in range(TM):
            _row_copy(ys_ref, 0, buf_ref.at[which], 0, sem_ref.at[which]).wait()

    @pl.when(t == 0)
    def _():
        fetch(0, 0)

    fetch(jnp.minimum(t + 1, last), 1 - cur)
    drain(cur)
    x = x1_ref[...] + modp_ref[5:6, :] * buf_ref[cur]
    outs[0][...] = x
    _inproj_body(x, params, outs[1:])

    @pl.when(t == last)
    def _():
        drain(1 - cur)


def _inproj_gather(layer, tile_mod, slot, x1, mod_prev, ys, mod, n1g, w_main, w_ba, ones_bd, qg, kg, caches):
    tok = lambda w: pl.BlockSpec((TM, w), lambda t, idx, s: (t, 0))
    full = lambda a: pl.BlockSpec(a.shape, lambda t, idx, s: (0,) * a.ndim)
    modblk = pl.BlockSpec((None, 6, D_MODEL), lambda t, idx, s: (idx[t], 0, 0))
    out_w = (D_MODEL, NA_W, NA_W, NA_W, 3 * GDN_W, GDN_W, LANE)
    cache_blk = pl.BlockSpec((None, None, SEQ, NA_W), lambda t, idx, s: (jnp.minimum(t, BATCH - 1), layer, 0, 0))
    cache_shape = jax.ShapeDtypeStruct((BATCH, DEPTH, SEQ, NA_W), F32)
    ins = [tile_mod, slot, x1, mod_prev, ys, mod, n1g, w_main, w_ba, ones_bd, qg, kg, *caches]
    specs = [tok(D_MODEL), modblk, pl.BlockSpec(memory_space=pl.ANY), modblk,
             full(n1g), full(w_main), full(w_ba), full(ones_bd), full(qg), full(kg)]
    specs += [pl.BlockSpec(memory_space=pl.ANY)] * 2
    return pl.pallas_call(
        _inproj_gather_kernel,
        out_shape=[jax.ShapeDtypeStruct((N_TOK, w), F32) for w in out_w] + [cache_shape, cache_shape],
        grid_spec=pltpu.PrefetchScalarGridSpec(
            num_scalar_prefetch=2,
            grid=(N_TOK // TM,),
            in_specs=specs,
            out_specs=[tok(w) for w in out_w] + [cache_blk, cache_blk],
            scratch_shapes=[pltpu.VMEM((2, TM, D_MODEL), F32), pltpu.SemaphoreType.DMA((2,))],
        ),
        input_output_aliases={len(ins) - 2: len(out_w), len(ins) - 1: len(out_w) + 1},
        compiler_params=_params(("arbitrary",)),
        name="inproj_gather",
    )(*ins)


def _pair_masks(shape):
    lane = lax.broadcasted_iota(jnp.int32, shape, 1)
    return lane < NA_HD


def _ctx_attn_kernel(q_ref, k_ref, v_ref, o_ref):
    first = _pair_masks((SEQ, LANE))
    col = lambda p: slice(p * LANE, (p + 1) * LANE)
    heads = range(NA_HEADS)
    pairs = range(NA_HEADS // 2)
    kb = [k_ref[:, col(p)].astype(BF16) for p in pairs]
    vb = [v_ref[:, col(p)].astype(BF16) for p in pairs]
    qh = [jnp.where(first if h % 2 == 0 else jnp.logical_not(first), q_ref[:, col(h // 2)], 0.0).astype(BF16)
          for h in heads]
    s = [_dot_nt(qh[h], kb[h // 2]) for h in heads]
    e = [jnp.exp(s[h] - jnp.max(s[h], axis=-1, keepdims=True)) for h in heads]
    den = [jnp.sum(e[h], axis=-1, keepdims=True) for h in heads]
    o = [_dot(e[h].astype(BF16), vb[h // 2]) for h in heads]
    for p in pairs:
        o_ref[:, col(p)] = jnp.where(first, o[2 * p] / den[2 * p], o[2 * p + 1] / den[2 * p + 1])


def _ctx_attention(q, k, v):
    blk = pl.BlockSpec((SEQ, NA_W), lambda b: (b, 0))
    return pl.pallas_call(
        _ctx_attn_kernel,
        out_shape=jax.ShapeDtypeStruct((N_TOK, NA_W), F32),
        grid=(BATCH,),
        in_specs=[blk, blk, blk],
        out_specs=blk,
        compiler_params=_params(("parallel",)),
        name="ctx_attention",
    )(q, k, v)


NA_PAIR_GROUP = 4


def _na_kernel(q_ref, k_ref, v_ref, kc_ref, vc_ref, bias_ref, o_all_ref, o_ref, kb_ref, vb_ref, kcb_ref, vcb_ref):
    del o_all_ref
    r = pl.program_id(1)

    @pl.when(r == 0)
    def _():
        kb_ref[...] = k_ref[...].astype(BF16)
        vb_ref[...] = v_ref[...].astype(BF16)
        kcb_ref[...] = kc_ref[...].astype(BF16)
        vcb_ref[...] = vc_ref[...].astype(BF16)

    base = jnp.clip(r - NA_KH // 2, 0, GRID_ROWS - NA_KH)
    dr0 = base - r + (NA_KH - 1)
    row0 = pl.multiple_of(base * GRID_W, GRID_W)
    nwin = NA_KH * GRID_W
    first = _pair_masks((GRID_W, LANE))
    col = lambda p: slice(p * LANE, (p + 1) * LANE)
    for g0 in range(0, NA_HEADS // 2, NA_PAIR_GROUP):
        pairs = range(g0, g0 + NA_PAIR_GROUP)
        heads = range(2 * g0, 2 * (g0 + NA_PAIR_GROUP))
        qh = {h: jnp.where(first if h % 2 == 0 else jnp.logical_not(first), q_ref[:, col(h // 2)], 0.0).astype(BF16)
              for h in heads}
        kw = {p: kb_ref[pl.ds(row0, nwin), col(p)] for p in pairs}
        kc = {p: kcb_ref[:, col(p)] for p in pairs}
        vw = {p: vb_ref[pl.ds(row0, nwin), col(p)] for p in pairs}
        vc = {p: vcb_ref[:, col(p)] for p in pairs}
        s_loc = {h: _dot_nt(qh[h], kw[h // 2]) + bias_ref[h, dr0] for h in heads}
        s_ctx = {h: _dot_nt(qh[h], kc[h // 2]) for h in heads}
        mx = {h: jnp.maximum(jnp.max(s_loc[h], axis=-1, keepdims=True), jnp.max(s_ctx[h], axis=-1, keepdims=True))
              for h in heads}
        e_loc = {h: jnp.exp(s_loc[h] - mx[h]) for h in heads}
        e_ctx = {h: jnp.exp(s_ctx[h] - mx[h]) for h in heads}
        den = {h: jnp.sum(e_loc[h], axis=-1, keepdims=True) + jnp.sum(e_ctx[h], axis=-1, keepdims=True)
               for h in heads}
        o = {h: _dot(e_loc[h].astype(BF16), vw[h // 2]) + _dot(e_ctx[h].astype(BF16), vc[h // 2]) for h in heads}
        for p in pairs:
            o_ref[:, col(p)] = jnp.where(first, o[2 * p] / den[2 * p], o[2 * p + 1] / den[2 * p + 1])


def _na_attention(layer, q, k, v, cache_k, cache_v, bias, o_all):
    lat0 = N_CTX // DEC_SEQ
    qblk = pl.BlockSpec((GRID_W, NA_W), lambda b, r: (N_CTX // GRID_W + b * GRID_ROWS + r, 0))
    kvblk = pl.BlockSpec((DEC_SEQ, NA_W), lambda b, r: (lat0 + b, 0))
    cblk = pl.BlockSpec((None, None, PAST_LEN, NA_W), lambda b, r: (b, layer, 0, 0))
    return pl.pallas_call(
        _na_kernel,
        out_shape=jax.ShapeDtypeStruct((N_TOK, NA_W), F32),
        grid=(DEC_BATCH, GRID_ROWS),
        in_specs=[qblk, kvblk, kvblk, cblk, cblk,
                  pl.BlockSpec(bias.shape, lambda b, r: (0, 0, 0, 0)),
                  pl.BlockSpec(memory_space=pl.ANY)],
        out_specs=qblk,
        scratch_shapes=[pltpu.VMEM((DEC_SEQ, NA_W), BF16), pltpu.VMEM((DEC_SEQ, NA_W), BF16),
                        pltpu.VMEM((PAST_LEN, NA_W), BF16), pltpu.VMEM((PAST_LEN, NA_W), BF16)],
        input_output_aliases={6: 0},
        compiler_params=_params(("parallel", "arbitrary")),
        name="na_attention",
    )(q, k, v, cache_k, cache_v, bias, o_all)


N_DC = 2 * NA_KW - 1


def _na_bias_kernel(r_ref, o_ref):
    shape = (GRID_W, NA_KH * GRID_W)
    qc = lax.broadcasted_iota(jnp.int32, shape, 0)
    kc = lax.broadcasted_iota(jnp.int32, shape, 1) & (GRID_W - 1)
    ws = jnp.clip(qc - NA_KW // 2, 0, GRID_W - NA_KW)
    valid = (kc >= ws) & (kc < ws + NA_KW)
    dc = jnp.where(valid, kc - qc + (NA_KW - 1), -1)
    for d0 in range(NA_KH):
        acc = jnp.full(shape, -jnp.inf, F32)
        for d in range(N_DC):
            acc = jnp.where(dc == d, r_ref[d0, d:d + 1, :], acc)
        o_ref[d0] = acc


def _na_bias_table(rpb):
    win = jnp.stack([rpb[:, d0:d0 + NA_KH, :] for d0 in range(NA_KH)], axis=1)
    rexp = jnp.repeat(win.transpose(0, 1, 3, 2), GRID_W, axis=-1)
    rexp = jnp.pad(rexp, ((0, 0), (0, 0), (0, 32 - N_DC), (0, 0)))
    blk = lambda rows: pl.BlockSpec((None, NA_KH, rows, NA_KH * GRID_W), lambda h: (h, 0, 0, 0))
    return pl.pallas_call(
        _na_bias_kernel,
        out_shape=jax.ShapeDtypeStruct((NA_HEADS, NA_KH, GRID_W, NA_KH * GRID_W), F32),
        grid=(NA_HEADS,),
        in_specs=[blk(32)],
        out_specs=blk(GRID_W),
        compiler_params=_params(("parallel",)),
        name="na_bias",
    )(rexp)


def _rope_tables():
    t = np.arange(DEC_SEQ)
    half = GDN_DK // 2
    inv_freq = (np.float32(ROPE_BASE) ** (-np.arange(0, half, 2, dtype=np.float32) / np.float32(half)))
    ang_r = (t // GRID_W).astype(np.float32)[:, None] * inv_freq
    ang_c = (t % GRID_W).astype(np.float32)[:, None] * inv_freq
    cr, sr, cc, sc = np.cos(ang_r), np.sin(ang_r), np.cos(ang_c), np.sin(ang_c)
    z = np.zeros_like(sr)
    cos = np.concatenate([cr, cr, cc, cc], axis=1)
    s_up = np.concatenate([-sr, z, -sc, z], axis=1)
    s_dn = np.concatenate([z, sr, z, sc], axis=1)
    return (jnp.asarray(cos, F32), jnp.asarray(s_up, F32), jnp.asarray(s_dn, F32))


HALO = 8
PREP_CHUNKS = 4
N_STREAM = 2 * GDN_HEADS
LOG_COL = N_STREAM


def _gdn_prep_kernel(axial, seq, *refs):
    if axial:
        (zqkv_ref, zba_ref, cw_ref, alog_ref, dtb_ref, cos_ref, sup_ref, sdn_ref,
         a_ref, qk_ref, r_ref, qd_ref, kd_ref, gl_ref, xs_ref) = refs
    else:
        (zqkv_ref, zba_ref, cw_ref, alog_ref, dtb_ref,
         a_ref, qk_ref, r_ref, qd_ref, kd_ref, gl_ref, xs_ref) = refs
    width = 3 * GDN_W
    part = pl.program_id(1)

    @pl.when(part == 0)
    def _():
        xs_ref[0:HALO, :] = jnp.zeros((HALO, width), F32)
        xs_ref[HALO + seq:HALO + seq + HALO, :] = jnp.zeros((HALO, width), F32)
        xs_ref[HALO:HALO + seq, :] = zqkv_ref[...]

    ri = lax.broadcasted_iota(jnp.int32, (CHUNK, CHUNK), 0)
    ci = lax.broadcasted_iota(jnp.int32, (CHUNK, CHUNK), 1)
    incl = (ri >= ci, ri <= ci)
    strict = (ri > ci, ri < ci)
    lane = lax.broadcasted_iota(jnp.int32, (CHUNK, LANE), 1)

    def body(i, carry):
        r0 = pl.multiple_of((part * PREP_CHUNKS + i) * CHUNK, CHUNK)
        rows = pl.ds(r0, CHUNK)
        win = xs_ref[pl.ds(r0, CHUNK + 2 * HALO), :]
        acc = win[HALO:HALO + CHUNK, :] * cw_ref[CONV_K // 2:CONV_K // 2 + 1, :]
        for j in range(CONV_K):
            if j != CONV_K // 2:
                lo = HALO + j - CONV_K // 2
                acc = acc + win[lo:lo + CHUNK, :] * cw_ref[j:j + 1, :]
        act = _silu(acc)

        zba = zba_ref[rows, :]
        beta_all = 1.0 / (1.0 + jnp.exp(-zba))
        xa = zba + dtb_ref[...]
        softplus = jnp.maximum(xa, 0.0) + jnp.log(1.0 + jnp.exp(-jnp.abs(xa)))
        bg = jnp.where(lane < LOG_COL, beta_all, -jnp.exp(alog_ref[...]) * softplus)

        slabs = [act[:, s * GDN_DK:(s + 1) * GDN_DK] for s in range(2 * GDN_HEADS)]
        ssq = [jnp.sum(x * x, axis=-1, keepdims=True) for x in slabs]
        slabs = [x * lax.rsqrt(s + EPS) for x, s in zip(slabs, ssq)]
        if axial:
            cos, s_up, s_dn = cos_ref[rows, :], sup_ref[rows, :], sdn_ref[rows, :]
            up = [pltpu.roll(x, GDN_DK - GDN_DK // 4, 1) for x in slabs]
            dn = [pltpu.roll(x, GDN_DK // 4, 1) for x in slabs]
            slabs = [x * cos + u * s_up + w * s_dn for x, u, w in zip(slabs, up, dn)]
        qs = [x * (GDN_DK ** -0.5) for x in slabs[:GDN_HEADS]]
        ks = slabs[GDN_HEADS:]
        kbfs = [k.astype(BF16) for k in ks]
        kks = [_dot_nt(kb_, kb_) for kb_ in kbfs]
        qks = [_dot_nt(q.astype(BF16), kb_) for q, kb_ in zip(qs, kbfs)]
        heads = [(qs[hd], ks[hd], act[:, 2 * GDN_W + hd * GDN_DK:2 * GDN_W + (hd + 1) * GDN_DK], kks[hd], qks[hd])
                 for hd in range(GDN_HEADS)]

        gcums = [_dot_exact_lhs(incl[d].astype(BF16), bg) for d in range(2)]
        gcum_ts = [g.T for g in gcums]
        for d in range(2):
            gcum, gcum_t = gcums[d], gcum_ts[d]
            last = gcum[0:1, :] if d else gcum[CHUNK - 1:CHUNK, :]
            e_in = jnp.exp(gcum)
            e_out = jnp.exp(last - gcum)
            gl_ref[i, d:d + 1, :] = jnp.exp(last)
            for hd in range(GDN_HEADS):
                q, k, v, kk, qk = heads[hd]
                cb = d * GDN_HEADS + hd
                cg = LOG_COL + cb
                beta = bg[:, cb:cb + 1]
                eg = e_in[:, cg:cg + 1]
                el = e_out[:, cg:cg + 1]
                diff = gcum[:, cg:cg + 1] - gcum_t[cg:cg + 1, :]
                decay = jnp.where(incl[d], jnp.exp(jnp.where(incl[d], diff, 0.0)), 0.0)
                a_ref[d, i, hd] = jnp.where(strict[d], beta * kk * decay, 0.0)
                qk_ref[d, i, hd] = (qk * decay).astype(BF16)
                kb = k * beta
                r_ref[d, i, hd, :, 0:GDN_DK] = (v * beta).astype(BF16)
                r_ref[d, i, hd, :, GDN_DK:2 * GDN_DK] = (kb * eg).astype(BF16)
                qd_ref[d, i, hd] = (q * eg).astype(BF16)
                kd_ref[d, i, hd] = (k * el).astype(BF16)
        return carry

    lax.fori_loop(0, PREP_CHUNKS, body, 0)


def _gdn_prep(axial, zqkv, zba, conv_w, a_log, dt_bias, rope):
    seq = DEC_SEQ if axial else SEQ
    nseq = DEC_BATCH if axial else BATCH
    nparts = seq // (PREP_CHUNKS * CHUNK)
    nchunk = nseq * seq // CHUNK
    blk0 = N_CTX // DEC_SEQ if axial else 0
    tok = lambda w: pl.BlockSpec((seq, w), lambda b, p: (blk0 + b, 0))
    full = lambda a: pl.BlockSpec(a.shape, lambda b, p: (0,) * a.ndim)
    ins = [zqkv, zba, conv_w, a_log, dt_bias]
    specs = [tok(3 * GDN_W), tok(LANE), full(conv_w), full(a_log), full(dt_bias)]
    if axial:
        ins += list(rope)
        specs += [full(t) for t in rope]
    tile = lambda w: pl.BlockSpec((2, PREP_CHUNKS, GDN_HEADS, CHUNK, w), lambda b, p: (0, b * nparts + p, 0, 0, 0))
    shape = lambda w, dt: jax.ShapeDtypeStruct((2, nchunk, GDN_HEADS, CHUNK, w), dt)
    return pl.pallas_call(
        functools.partial(_gdn_prep_kernel, axial, seq),
        out_shape=[shape(CHUNK, F32), shape(CHUNK, BF16), shape(2 * GDN_DK, BF16),
                   shape(GDN_DK, BF16), shape(GDN_DK, BF16),
                   jax.ShapeDtypeStruct((nchunk, 2, LANE), F32)],
        grid=(nseq, nparts),
        in_specs=specs,
        out_specs=[tile(CHUNK), tile(CHUNK), tile(2 * GDN_DK), tile(GDN_DK), tile(GDN_DK),
                   pl.BlockSpec((PREP_CHUNKS, 2, LANE), lambda b, p: (b * nparts + p, 0, 0))],
        scratch_shapes=[pltpu.VMEM((seq + 2 * HALO, 3 * GDN_W), F32)],
        compiler_params=_params(("parallel", "arbitrary")),
        name="gdn_prep_lat" if axial else "gdn_prep_ctx",
    )(*ins)


TRI_BLK = 8
TRI_ELEMS = CHUNK * CHUNK


def _tri_inverse_kernel(groups_per_dir, a_ref, o_ref, at_ref, tt_ref):
    backward = pl.program_id(0) >= groups_per_dir
    for blk in range(TRI_ELEMS // LANE):
        cols = slice(blk * LANE, (blk + 1) * LANE)
        at_ref[cols, :] = a_ref[:, cols].T
    tt_ref[...] = jnp.zeros(tt_ref.shape, F32)
    nblk = CHUNK // TRI_BLK

    def substitute(mirror):
        phys = (lambda idx: CHUNK - 1 - idx) if mirror else (lambda idx: idx)
        for ib in range(nblk):
            c_lo, width = ((nblk - 1 - ib) * TRI_BLK if mirror else 0), (ib + 1) * TRI_BLK

            def row_body(ii, carry, ib=ib, c_lo=c_lo, width=width):
                i = phys(ib * TRI_BLK + ii)
                cidx = lax.broadcasted_iota(jnp.int32, (width, LANE), 0) + c_lo
                acc = (cidx == i).astype(F32)
                for jb in range(ib + 1):
                    w_j = (jb + 1) * TRI_BLK
                    cj_lo = (nblk - 1 - jb) * TRI_BLK if mirror else 0
                    sub = slice(cj_lo - c_lo, cj_lo - c_lo + w_j)
                    part = acc[sub]
                    for jl in range(jb * TRI_BLK, (jb + 1) * TRI_BLK):
                        j = phys(jl)
                        arow = at_ref[pl.ds(i * CHUNK + j, 1), :]
                        part = part - arow * tt_ref[j * CHUNK + cj_lo:j * CHUNK + cj_lo + w_j, :]
                    acc = part if w_j == width else (
                        jnp.concatenate([part, acc[w_j:]], axis=0) if not mirror
                        else jnp.concatenate([acc[:width - w_j], part], axis=0))
                tt_ref[pl.ds(pl.multiple_of(i * CHUNK + c_lo, TRI_BLK), width), :] = acc
                return carry

            lax.fori_loop(0, TRI_BLK, row_body, 0)

    @pl.when(jnp.logical_not(backward))
    def _():
        substitute(False)

    @pl.when(backward)
    def _():
        substitute(True)

    for blk in range(TRI_ELEMS // LANE):
        cols = slice(blk * LANE, (blk + 1) * LANE)
        o_ref[:, cols] = tt_ref[cols, :].T.astype(BF16)


def _tri_inverse(a):
    nprob = a.shape[1] * a.shape[2]
    groups_per_dir = nprob // LANE
    blk = pl.BlockSpec((LANE, TRI_ELEMS), lambda g: (g, 0))
    out = pl.pallas_call(
        functools.partial(_tri_inverse_kernel, groups_per_dir),
        out_shape=jax.ShapeDtypeStruct((2 * nprob, TRI_ELEMS), BF16),
        grid=(2 * groups_per_dir,),
        in_specs=[blk],
        out_specs=blk,
        scratch_shapes=[pltpu.VMEM((TRI_ELEMS, LANE), F32), pltpu.VMEM((TRI_ELEMS, LANE), F32)],
        compiler_params=_params(("parallel",)),
        name="tri_inverse",
    )(a.reshape(2 * nprob, TRI_ELEMS))
    return out.reshape(a.shape)


def _gdn_scan_kernel(nchunk, t_ref, qk_ref, r_ref, qd_ref, kd_ref, gl_ref, s0_ref, *rest):
    o_ref, sfin_ref, state_ref = rest[-3:]
    state_ref[...] = s0_ref[...]
    streams = [(d, hd) for d in range(2) for hd in range(GDN_HEADS)]

    def body(c, carry):
        cc = (c, nchunk - 1 - c)
        uws = [_dot(t_ref[d, cc[d], hd], r_ref[d, cc[d], hd]).astype(BF16) for d, hd in streams]
        mbs = [_dot_tn(kd_ref[d, cc[d], hd], uw) for (d, hd), uw in zip(streams, uws)]
        qos = [_dot(qk_ref[d, cc[d], hd], uw) for (d, hd), uw in zip(streams, uws)]
        sts = [state_ref[d * GDN_HEADS + hd] for d, hd in streams]
        sbs = [s.astype(BF16) for s in sts]
        for (d, hd), qo, sb in zip(streams, qos, sbs):
            rows = pl.ds(pl.multiple_of(cc[d] * CHUNK, CHUNK), CHUNK)
            qp = (qd_ref[d, cc[d], hd].astype(F32) - qo[:, GDN_DK:]).astype(BF16)
            o_ref[d, rows, hd * GDN_DK:(hd + 1) * GDN_DK] = _dot(qp, sb) + qo[:, :GDN_DK]
        for (d, hd), mb, s, sb in zip(streams, mbs, sts, sbs):
            sidx = d * GDN_HEADS + hd
            gl = gl_ref[cc[d]][d:d + 1, LOG_COL + sidx:LOG_COL + sidx + 1]
            state_ref[sidx] = s * gl - _dot(mb[:, GDN_DK:].astype(BF16), sb) + mb[:, :GDN_DK]
        return carry

    lax.fori_loop(0, nchunk, body, 0, unroll=2)
    sfin_ref[...] = state_ref[...]


def _gdn_scan(seq, nseq, tinv, qk, r, qd, kd, gl, s0, o_all=None):
    nc = seq // CHUNK
    blk0 = 0 if o_all is None else N_CTX // seq
    tile = lambda w: pl.BlockSpec((2, nc, GDN_HEADS, CHUNK, w), lambda b: (0, b, 0, 0, 0))
    sblk = pl.BlockSpec((None, N_STREAM, GDN_DK, GDN_DK), lambda b: (b, 0, 0, 0))
    oblk = pl.BlockSpec((2, seq, GDN_W), lambda b: (0, blk0 + b, 0))
    ins = [tinv, qk, r, qd, kd, gl, s0]
    specs = [tile(CHUNK), tile(CHUNK), tile(2 * GDN_DK), tile(GDN_DK), tile(GDN_DK),
             pl.BlockSpec((nc, 2, LANE), lambda b: (b, 0, 0)), sblk]
    aliases = {}
    if o_all is not None:
        ins.append(o_all)
        specs.append(pl.BlockSpec(memory_space=pl.ANY))
        aliases = {len(ins) - 1: 0}
    return pl.pallas_call(
        functools.partial(_gdn_scan_kernel, nc),
        out_shape=[jax.ShapeDtypeStruct((2, N_TOK, GDN_W), F32),
                   jax.ShapeDtypeStruct((nseq, N_STREAM, GDN_DK, GDN_DK), F32)],
        grid=(nseq,),
        in_specs=specs,
        out_specs=[oblk, sblk],
        scratch_shapes=[pltpu.VMEM((N_STREAM, GDN_DK, GDN_DK), F32)],
        input_output_aliases=aliases,
        compiler_params=_params(("parallel",)),
        name="gdn_scan_lat" if seq == DEC_SEQ else "gdn_scan_ctx",
    )(*ins)


def _first_max(vals):
    sel = []
    taken = None
    for a, va in enumerate(vals):
        is_max = None
        for b, vb in enumerate(vals):
            if a == b:
                continue
            c = va >= vb
            is_max = c if is_max is None else (is_max & c)
        if taken is not None:
            is_max = is_max & jnp.logical_not(taken)
        sel.append(is_max)
        taken = is_max if taken is None else (taken | is_max)
    return sel


PAIR_SLOTS = ((0, 1), (0, 2), (0, 3), (1, 3), (1, 2), (3, 2))
N_CLASS = N_GROUPS * len(PAIR_SLOTS)
CLASS_EXPERTS = tuple((g * EXP_PER_GROUP + a, g * EXP_PER_GROUP + b)
                      for g in range(N_GROUPS) for a, b in PAIR_SLOTS)


def _route_rows(lt):
    gl = [lt[a:a + 1, :] for a in range(N_GROUPS)]
    gsel = _first_max(gl)
    gmax = functools.reduce(jnp.maximum, gl)
    gden = functools.reduce(lambda x, y: x + y, [jnp.exp(x - gmax) for x in gl])
    g_w = 1.0 / gden
    el = []
    for kx in range(EXP_PER_GROUP):
        acc = None
        for a in range(N_GROUPS):
            row = N_GROUPS + a * EXP_PER_GROUP + kx
            term = jnp.where(gsel[a], lt[row:row + 1, :], 0.0)
            acc = term if acc is None else acc + term
        el.append(acc)
    emax = functools.reduce(jnp.maximum, el)
    ee = [jnp.exp(x - emax) for x in el]
    eden = functools.reduce(lambda x, y: x + y, ee)
    ep = [x / eden for x in ee]
    top1 = _first_max(ep)
    ep2 = [jnp.where(top1[kx], -1.0, ep[kx]) for kx in range(EXP_PER_GROUP)]
    top2 = _first_max(ep2)
    chosen = [top1[kx] | top2[kx] for kx in range(EXP_PER_GROUP)]
    wsum = functools.reduce(lambda x, y: x + y,
                            [jnp.where(chosen[kx], ep[kx], 0.0) for kx in range(EXP_PER_GROUP)])
    within = [jnp.where(chosen[kx], ep[kx] / wsum, 0.0) for kx in range(EXP_PER_GROUP)]
    cls = jnp.zeros(g_w.shape, jnp.int32)
    w_a = jnp.zeros(g_w.shape, F32)
    w_b = jnp.zeros(g_w.shape, F32)
    for g in range(N_GROUPS):
        for kx, (a, b) in enumerate(PAIR_SLOTS):
            hit = gsel[g] & chosen[a] & chosen[b]
            cls = jnp.where(hit, g * len(PAIR_SLOTS) + kx, cls)
            w_a = jnp.where(hit, g_w * within[a], w_a)
            w_b = jnp.where(hit, g_w * within[b], w_b)
    return cls, w_a, w_b


H2W = D_MODEL + LANE
COL_WA = D_MODEL
COL_WB = D_MODEL + 1


def _outproj_kernel(nx, idx_ref, *refs):
    x = _token_tile(nx, refs)
    (ona_ref, og_ref, zg_ref, mod_ref, wo_ref, gng_ref, n2g_ref, wr_ref, br_ref,
     x1_ref, h2x_ref, cls_ref) = refs[nx:]
    del idx_ref
    m = mod_ref[...]
    og = og_ref[0] + og_ref[1]
    zg = zg_ref[...]
    parts = []
    for hd in range(GDN_HEADS):
        cols = slice(hd * GDN_DK, (hd + 1) * GDN_DK)
        oh = og[:, cols]
        oh = oh * lax.rsqrt(jnp.mean(oh * oh, axis=-1, keepdims=True) + EPS) * gng_ref[...]
        parts.append((oh * _silu(zg[:, cols])).astype(BF16))
    mix = _dot(ona_ref[...].astype(BF16), wo_ref[0:NA_W, :])
    for hd in range(GDN_HEADS):
        r0 = NA_W + hd * GDN_DK
        mix = mix + _dot(parts[hd], wo_ref[r0:r0 + GDN_DK, :])
    x1 = x + m[2:3, :] * mix
    x1_ref[...] = x1
    y = x1 * lax.rsqrt(jnp.mean(x1 * x1, axis=-1, keepdims=True) + EPS) * n2g_ref[...]
    h2 = y * (1.0 + m[4:5, :]) + m[3:4, :]
    h2x_ref[:, 0:D_MODEL] = h2.astype(BF16).astype(F32)
    logits = _dot3(h2, wr_ref[...]) + br_ref[...]
    cls, w_a, w_b = _route_rows(logits.T)
    rid = lax.broadcasted_iota(jnp.int32, (8, TM), 0)
    wt = jnp.where(rid == COL_WA - D_MODEL, w_a, jnp.where(rid == COL_WB - D_MODEL, w_b, 0.0))
    wt = jnp.concatenate([wt, jnp.zeros((LANE - 8, TM), F32)], axis=0)
    h2x_ref[:, D_MODEL:H2W] = wt.T
    cls_ref[...] = jnp.broadcast_to(cls, (8, TM))


def _outproj(tile_mod, x, o_na, o_gdn, zg, mod, w_out, gng, n2g, w_r, b_r):
    tok = lambda w: pl.BlockSpec((TM, w), lambda t, idx: (t, 0))
    full = lambda a: pl.BlockSpec(a.shape, lambda t, idx: (0,) * a.ndim)
    xs = list(x) if isinstance(x, tuple) else [x]
    return pl.pallas_call(
        functools.partial(_outproj_kernel, len(xs)),
        out_shape=[jax.ShapeDtypeStruct((N_TOK, D_MODEL), F32),
                   jax.ShapeDtypeStruct((N_TOK, H2W), F32),
                   jax.ShapeDtypeStruct((N_TOK // TM, 8, TM), jnp.int32)],
        grid_spec=pltpu.PrefetchScalarGridSpec(
            num_scalar_prefetch=1,
            grid=(N_TOK // TM,),
            in_specs=_token_specs(x) + [
                tok(NA_W),
                pl.BlockSpec((2, TM, GDN_W), lambda t, idx: (0, t, 0)), tok(GDN_W),
                pl.BlockSpec((None, 6, D_MODEL), lambda t, idx: (idx[t], 0, 0)),
                full(w_out), full(gng), full(n2g), full(w_r), full(b_r),
            ],
            out_specs=[tok(D_MODEL), tok(H2W), pl.BlockSpec((None, 8, TM), lambda t, idx: (t, 0, 0))],
        ),
        compiler_params=_params(("parallel",)),
        name="outproj",
    )(tile_mod, *xs, o_na, o_gdn, zg, mod, w_out, gng, n2g, w_r, b_r)


TM_X = 256
N_XTILE = N_TOK // TM_X + N_CLASS
N_SLOT = N_XTILE * TM_X
CLS_ROWS = N_TOK // LANE
TAB_EA, TAB_EB, TAB_USED, TAB_LAST = 0, 1, 2, 3


def _route_pos_kernel(cls_ref, slot_ref, tab_ref):
    cls = cls_ref[...]
    li = lax.broadcasted_iota(jnp.int32, (LANE, LANE), 0)
    lj = lax.broadcasted_iota(jnp.int32, (LANE, LANE), 1)
    before_lane = (li < lj).astype(BF16)
    ri = lax.broadcasted_iota(jnp.int32, (CLS_ROWS, CLS_ROWS), 0)
    rj = lax.broadcasted_iota(jnp.int32, (CLS_ROWS, CLS_ROWS), 1)
    before_row = (rj < ri).astype(BF16)
    lane = lax.broadcasted_iota(jnp.int32, (1, LANE), 1)
    tile_start = lane * TM_X
    off = jnp.zeros((1, 1), jnp.int32)
    slot = jnp.zeros(cls.shape, jnp.int32)
    tab_a = jnp.zeros((1, LANE), jnp.int32)
    tab_b = jnp.zeros((1, LANE), jnp.int32)
    tab_last = jnp.full((1, LANE), -1, jnp.int32)
    for c in range(N_CLASS):
        hit = cls == c
        one = hit.astype(F32)
        in_row = _dot(one.astype(BF16), before_lane)
        row_tot = jnp.broadcast_to(jnp.sum(one, axis=1, keepdims=True), one.shape)
        rank = in_row + _dot(before_row, row_tot.astype(BF16))
        count = jnp.sum(row_tot[:, 0:1], axis=0, keepdims=True).astype(jnp.int32)
        slot = jnp.where(hit, off + rank.astype(jnp.int32), slot)
        nxt = off + (((count + (TM_X - 1)) >> 8) << 8)
        mine = (tile_start >= off) & (tile_start < nxt)
        tab_a = jnp.where(mine, CLASS_EXPERTS[c][0], tab_a)
        tab_b = jnp.where(mine, CLASS_EXPERTS[c][1], tab_b)
        tab_last = jnp.where((lane == c) & (nxt > off), (nxt >> 8) - 1, tab_last)
        off = nxt
    slot_ref[...] = slot
    row = lax.broadcasted_iota(jnp.int32, (8, LANE), 0)
    used = jnp.broadcast_to(off >> 8, (8, LANE))
    tab_ref[...] = jnp.where(row == TAB_EA, tab_a, jnp.where(row == TAB_EB, tab_b,
                                                               jnp.where(row == TAB_LAST, tab_last, used)))


def _route_positions(cls):
    assert TM_X == 256
    return pl.pallas_call(
        _route_pos_kernel,
        out_shape=[jax.ShapeDtypeStruct((CLS_ROWS, LANE), jnp.int32),
                   jax.ShapeDtypeStruct((8, LANE), jnp.int32)],
        name="route_positions",
    )(cls)


ROW_UNROLL = 16


def _permute_kernel(slot_ref, tab_ref, h_ref, xs_ref, buf_ref, zero_ref, sem_ref, zsem_ref):
    t = pl.program_id(0)
    cur = t % 2

    @pl.when(t == 0)
    def _():
        zero_ref[...] = jnp.zeros(zero_ref.shape, F32)

        def tile_copy(c):
            return pltpu.make_async_copy(
                zero_ref, xs_ref.at[pl.ds(pl.multiple_of(tab_ref[TAB_LAST, c] * TM_X, TM_X), TM_X), :], zsem_ref.at[0])

        for c in range(N_CLASS):
            @pl.when(tab_ref[TAB_LAST, c] >= 0)
            def _():
                tile_copy(c).start()
        for c in range(N_CLASS):
            @pl.when(tab_ref[TAB_LAST, c] >= 0)
            def _():
                tile_copy(c).wait()

    buf_ref[cur] = h_ref[...]

    def issue(i, carry):
        _row_copy(buf_ref.at[cur], i, xs_ref, slot_ref[t * TM + i], sem_ref.at[cur]).start()
        return carry

    lax.fori_loop(0, TM, issue, 0, unroll=ROW_UNROLL)

    def drain(which):
        def wait(i, carry):
            _row_copy(buf_ref.at[which], 0, xs_ref, 0, sem_ref.at[which]).wait()
            return carry
        lax.fori_loop(0, TM, wait, 0, unroll=ROW_UNROLL)

    @pl.when(t > 0)
    def _():
        drain(1 - cur)

    @pl.when(t == pl.num_programs(0) - 1)
    def _():
        drain(cur)


def _permute(slot, tab, h2x):
    assert TM_X == TM
    return pl.pallas_call(
        _permute_kernel,
        out_shape=jax.ShapeDtypeStruct((N_SLOT, H2W), F32),
        grid_spec=pltpu.PrefetchScalarGridSpec(
            num_scalar_prefetch=2,
            grid=(N_TOK // TM,),
            in_specs=[pl.BlockSpec((TM, H2W), lambda t, s, tab: (t, 0))],
            out_specs=pl.BlockSpec(memory_space=pl.ANY),
            scratch_shapes=[pltpu.VMEM((2, TM, H2W), F32), pltpu.VMEM((TM_X, H2W), F32),
                            pltpu.SemaphoreType.DMA((2,)), pltpu.SemaphoreType.DMA((1,))],
        ),
        compiler_params=_params(("arbitrary",)),
        name="moe_permute",
    )(slot, tab, h2x)


def _experts_kernel(tab_ref, xs_ref, w1a_ref, w3a_ref, w2a_ref, w1b_ref, w3b_ref, w2b_ref, ys_ref,
                    c1a_ref, c3a_ref, c2a_ref, c1b_ref, c3b_ref, c2b_ref):
    t = pl.program_id(0)
    prev = jnp.maximum(t - 1, 0)

    @pl.when(t < tab_ref[TAB_USED, 0])
    def _():
        @pl.when((t == 0) | (tab_ref[TAB_EA, t] != tab_ref[TAB_EA, prev]))
        def _():
            c1a_ref[...] = w1a_ref[...].astype(BF16)
            c3a_ref[...] = w3a_ref[...].astype(BF16)
            c2a_ref[...] = w2a_ref[...].astype(BF16)

        @pl.when((t == 0) | (tab_ref[TAB_EB, t] != tab_ref[TAB_EB, prev]))
        def _():
            c1b_ref[...] = w1b_ref[...].astype(BF16)
            c3b_ref[...] = w3b_ref[...].astype(BF16)
            c2b_ref[...] = w2b_ref[...].astype(BF16)

        x = xs_ref[:, 0:D_MODEL].astype(BF16)
        gates = xs_ref[:, D_MODEL:H2W]
        w_a = gates[:, COL_WA - D_MODEL:COL_WA - D_MODEL + 1]
        w_b = gates[:, COL_WB - D_MODEL:COL_WB - D_MODEL + 1]
        a1 = _dot(x, c1a_ref[...])
        a3 = _dot(x, c3a_ref[...])
        b1 = _dot(x, c1b_ref[...])
        b3 = _dot(x, c3b_ref[...])
        sa = (_silu(a1) * a3 * w_a).astype(BF16)
        sb = (_silu(b1) * b3 * w_b).astype(BF16)
        ys_ref[...] = _dot(sa, c2a_ref[...]) + _dot(sb, c2b_ref[...])


def _experts(layer, tab, xs, w1, w3, w2):
    last = lambda t, tab: jnp.minimum(t, tab[TAB_USED, 0] - 1)
    tile = lambda w: pl.BlockSpec((TM_X, w), lambda t, tab: (last(t, tab), 0))
    wspec = lambda shape, row: pl.BlockSpec((None, None) + shape,
                                            lambda t, tab: (layer, tab[row, last(t, tab)], 0, 0))
    up, down = (D_MODEL, EXPERT_FF), (EXPERT_FF, D_MODEL)
    return pl.pallas_call(
        _experts_kernel,
        out_shape=jax.ShapeDtypeStruct((N_SLOT, D_MODEL), F32),
        grid_spec=pltpu.PrefetchScalarGridSpec(
            num_scalar_prefetch=1,
            grid=(N_XTILE,),
            in_specs=[tile(H2W),
                      wspec(up, TAB_EA), wspec(up, TAB_EA), wspec(down, TAB_EA),
                      wspec(up, TAB_EB), wspec(up, TAB_EB), wspec(down, TAB_EB)],
            out_specs=tile(D_MODEL),
            scratch_shapes=[pltpu.VMEM(up, BF16), pltpu.VMEM(up, BF16), pltpu.VMEM(down, BF16),
                            pltpu.VMEM(up, BF16), pltpu.VMEM(up, BF16), pltpu.VMEM(down, BF16)],
        ),
        compiler_params=_params(("arbitrary",)),
        name="moe_experts",
    )(tab, xs, w1, w3, w2, w1, w3, w2)


def _unpermute_kernel(idx_ref, slot_ref, x1_ref, mod_ref, ys_ref, *rest):
    outs, (buf_ref, sem_ref) = rest[:-2], rest[-2:]
    del idx_ref
    t = pl.program_id(0)
    cur = t % 2

    def fetch(tile, which):
        def issue(i, carry):
            _row_copy(ys_ref, slot_ref[tile * TM + i], buf_ref.at[which], i, sem_ref.at[which]).start()
            return carry
        lax.fori_loop(0, TM, issue, 0, unroll=ROW_UNROLL)

    @pl.when(t == 0)
    def _():
        fetch(0, 0)

    @pl.when(t + 1 < pl.num_programs(0))
    def _():
        fetch(t + 1, 1 - cur)

    def wait(i, carry):
        _row_copy(ys_ref, 0, buf_ref.at[cur], 0, sem_ref.at[cur]).wait()
        return carry

    lax.fori_loop(0, TM, wait, 0, unroll=ROW_UNROLL)
    y = x1_ref[...] + mod_ref[5:6, :] * buf_ref[cur]

    @pl.when(t < N_CTX // TM)
    def _():
        outs[0][...] = y

    @pl.when(t >= N_CTX // TM)
    def _():
        outs[1][...] = y


def _unpermute(tile_mod, slot, x1, mod, ys):
    tok = pl.BlockSpec((TM, D_MODEL), lambda t, idx, s: (t, 0))
    out_shape = [jax.ShapeDtypeStruct((N_CTX, D_MODEL), F32), jax.ShapeDtypeStruct((N_LAT, D_MODEL), F32)]
    out_specs = _token_specs((None, None))
    return pl.pallas_call(
        _unpermute_kernel,
        out_shape=out_shape,
        grid_spec=pltpu.PrefetchScalarGridSpec(
            num_scalar_prefetch=2,
            grid=(N_TOK // TM,),
            in_specs=[tok, pl.BlockSpec((None, 6, D_MODEL), lambda t, idx, s: (idx[t], 0, 0)),
                      pl.BlockSpec(memory_space=pl.ANY)],
            out_specs=out_specs,
            scratch_shapes=[pltpu.VMEM((2, TM, D_MODEL), F32), pltpu.SemaphoreType.DMA((2,))],
        ),
        compiler_params=_params(("arbitrary",)),
        name="moe_unpermute",
    )(tile_mod, slot, x1, mod, ys)


def _tile_mod_index(tile):
    t = np.arange(N_TOK // tile) * tile
    return jnp.asarray(np.where(t < N_CTX, 0, 1 + (t - N_CTX) // DEC_SEQ), jnp.int32)


def kernel(x_prompt, x_sample, c, cache_k, cache_v, state_ssm, c_ctx, ada_w, ada_b, norm1_g, norm2_g, w_in, w_out, na_qn_g, na_kn_g, na_rpb, gdn_conv_w, gdn_a_log, gdn_dt_bias, gdn_norm_g, moe_w_rg, moe_b_rg, moe_w_re, moe_b_re, moe_w1, moe_w3, moe_w2):
    x = (x_prompt.reshape(N_CTX, D_MODEL), x_sample.reshape(N_LAT, D_MODEL))
    cv = jnp.concatenate([c_ctx[None, :], c, jnp.zeros((N_MOD_PAD - N_MOD, D_MODEL), F32)], axis=0)
    mod_all = _modulation(cv, ada_w, ada_b).reshape(DEPTH, N_MOD_PAD, 6, D_MODEL)

    idx_tm = _tile_mod_index(TM)
    hh = np.arange(NA_W) // NA_HD
    ones_bd = jnp.asarray(hh[:, None] == hh[None, :], BF16)
    rope = _rope_tables()
    cache_k4 = cache_k.reshape(DEC_BATCH, DEPTH, PAST_LEN, NA_W)
    cache_v4 = cache_v.reshape(DEC_BATCH, DEPTH, PAST_LEN, NA_W)
    lane_pad = lambda a, at: jnp.zeros((1, LANE), F32).at[0, at:at + a.size].set(a.reshape(-1))
    zeros_state = jnp.zeros((BATCH, 2 * GDN_HEADS, GDN_DK, GDN_DK), F32)

    caches, ss, pending = None, [], None
    for l in range(DEPTH):
        mod = mod_all[l]
        w_main = w_in[l, :, :N_MAIN].astype(BF16)
        w_ba = jnp.zeros((D_MODEL, LANE), BF16).at[:, :N_IN - N_MAIN].set(w_in[l, :, N_MAIN:].astype(BF16))
        qg = jnp.tile(na_qn_g[l], NA_HEADS)[None, :]
        kg = jnp.tile(na_kn_g[l], NA_HEADS)[None, :]
        proj = (mod, norm1_g[l][None, :], w_main, w_ba, ones_bd, qg, kg, caches)
        if pending is None:
            q, k, v, zqkv, zg, zba, *caches = _inproj(l, idx_tm, x, *proj)
        else:
            x, q, k, v, zqkv, zg, zba, *caches = _inproj_gather(l, idx_tm, *pending, *proj)

        bias = _na_bias_table(na_rpb[l])
        o_na = _na_attention(l, q, k, v, cache_k4, cache_v4, bias, _ctx_attention(q, k, v))

        conv_w = jnp.zeros((8, 3 * GDN_W), F32).at[:CONV_K].set(gdn_conv_w[l])
        a_row = lane_pad(gdn_a_log[l], 2 * GDN_HEADS)
        dt_row = lane_pad(gdn_dt_bias[l], 2 * GDN_HEADS)
        a_c, *ops_c = _gdn_prep(False, zqkv, zba, conv_w, a_row, dt_row, None)
        a_l, *ops_l = _gdn_prep(True, zqkv, zba, conv_w, a_row, dt_row, rope)
        og_c, s_ctx = _gdn_scan(SEQ, BATCH, _tri_inverse(a_c), *ops_c, zeros_state)
        s0_lat = state_ssm[:, l].reshape(DEC_BATCH, N_STREAM, GDN_DK, GDN_DK)
        o_gdn, _ = _gdn_scan(DEC_SEQ, DEC_BATCH, _tri_inverse(a_l), *ops_l, s0_lat, og_c)

        w_r = jnp.zeros((D_MODEL, LANE), F32).at[:, :N_GROUPS].set(moe_w_rg[l])
        w_r = w_r.at[:, N_GROUPS:N_GROUPS + N_EXPERTS].set(moe_w_re[l])
        b_r = lane_pad(jnp.concatenate([moe_b_rg[l], moe_b_re[l]]), 0)
        x1, h2x, cls = _outproj(idx_tm, x, o_na, o_gdn, zg, mod, w_out[l].astype(BF16),
                                gdn_norm_g[l][None, :], norm2_g[l][None, :], w_r, b_r)
        slot, tab = _route_positions(cls[:, 0, :].reshape(CLS_ROWS, LANE))
        slot = slot.reshape(N_TOK)
        ys = _experts(l, tab, _permute(slot, tab, h2x), moe_w1, moe_w3, moe_w2)
        if l == DEPTH - 1:
            x = _unpermute(idx_tm, slot, x1, mod, ys)
        else:
            pending = (slot, x1, mod, ys)

        ss.append(s_ctx.reshape(BATCH, 2, GDN_HEADS, GDN_DK, GDN_DK))

    y_prompt = x[0].reshape(BATCH, SEQ, D_MODEL)
    y_sample = x[1].reshape(DEC_BATCH, DEC_SEQ, D_MODEL)
    new_k, new_v = (a.reshape(BATCH, DEPTH, SEQ, NA_HEADS, NA_HD) for a in caches)
    return (y_prompt, y_sample, new_k, new_v, jnp.stack(ss, axis=1))
```

```python
import functools

import jax
import jax.numpy as jnp
import numpy as np
from jax import lax
from jax.experimental import pallas as pl
from jax.experimental.pallas import tpu as pltpu

F32 = jnp.float32
BF16 = jnp.bfloat16

D_MODEL = 1024
BATCH = 16
SEQ = 256
DEPTH = 2
DEC_BATCH = 8
DEC_SEQ = 1024
PAST_LEN = 512
GRID_W = 64
GRID_ROWS = DEC_SEQ // GRID_W
NA_HEADS = 8
NA_HD = 64
NA_W = NA_HEADS * NA_HD
NA_KH = 8
NA_KW = 16
GDN_HEADS = 4
GDN_DK = 128
GDN_W = GDN_HEADS * GDN_DK
CONV_K = 5
CHUNK = 64
ROPE_BASE = 10000.0
N_GROUPS = 4
EXP_PER_GROUP = 4
N_EXPERTS = 16
EXPERT_FF = 512
EPS = 1e-6
N_IN = 3 * NA_W + 4 * GDN_W + 4 * GDN_HEADS

N_CTX = BATCH * SEQ
N_LAT = DEC_BATCH * DEC_SEQ
N_TOK = N_CTX + N_LAT
N_MOD = 1 + DEC_BATCH
N_MOD_PAD = 16
LANE = 128
N_MAIN = 3 * NA_W + 4 * GDN_W
TM = 256
VMEM_LIMIT = 56 * 1024 * 1024


def _dot(a, b):
    return jnp.dot(a, b, preferred_element_type=F32)


def _dot_nt(a, b):
    return lax.dot_general(a, b, (((1,), (1,)), ((), ())), preferred_element_type=F32)


def _dot_tn(a, b):
    return lax.dot_general(a, b, (((0,), (0,)), ((), ())), preferred_element_type=F32)


def _split2(x):
    hi = x.astype(BF16)
    lo = (x - hi.astype(F32)).astype(BF16)
    return hi, lo


def _split3(x):
    hi = x.astype(BF16)
    r = x - hi.astype(F32)
    mid = r.astype(BF16)
    lo = (r - mid.astype(F32)).astype(BF16)
    return hi, mid, lo


def _dot3(a, b):
    ah, al = _split2(a)
    bh, bl = _split2(b)
    return _dot(ah, bh) + (_dot(ah, bl) + _dot(al, bh))


def _dot_exact_lhs(mask_bf16, x):
    hi, mid, lo = _split3(x)
    return _dot(mask_bf16, hi) + (_dot(mask_bf16, mid) + _dot(mask_bf16, lo))


def _silu(x):
    return x * (1.0 / (1.0 + jnp.exp(-x)))


def _params(sem):
    return pltpu.CompilerParams(dimension_semantics=sem, vmem_limit_bytes=VMEM_LIMIT)


def _mod_kernel(cv_ref, w_ref, b_ref, o_ref):
    cv = cv_ref[...]
    s = _silu(cv)
    o_ref[...] = jnp.dot(s, w_ref[...], preferred_element_type=F32,
                         precision=lax.Precision.HIGHEST) + b_ref[...]


def _modulation(cv, ada_w, ada_b):
    nblk = 1024
    return pl.pallas_call(
        _mod_kernel,
        out_shape=jax.ShapeDtypeStruct((DEPTH, N_MOD_PAD, 6 * D_MODEL), F32),
        grid=(DEPTH, 6 * D_MODEL // nblk),
        in_specs=[
            pl.BlockSpec((N_MOD_PAD, D_MODEL), lambda l, j: (0, 0)),
            pl.BlockSpec((None, D_MODEL, nblk), lambda l, j: (l, 0, j)),
            pl.BlockSpec((None, 1, nblk), lambda l, j: (l, 0, j)),
        ],
        out_specs=pl.BlockSpec((None, N_MOD_PAD, nblk), lambda l, j: (l, 0, j)),
        compiler_params=_params(("parallel", "parallel")),
        name="modulation",
    )(cv, ada_w, ada_b.reshape(DEPTH, 1, 6 * D_MODEL))


def _token_specs(x):
    if not isinstance(x, tuple):
        return [pl.BlockSpec((TM, D_MODEL), lambda t, *_: (t, 0))]
    return [pl.BlockSpec((TM, D_MODEL), lambda t, *_: (jnp.minimum(t, N_CTX // TM - 1), 0)),
            pl.BlockSpec((TM, D_MODEL), lambda t, *_: (jnp.maximum(t - N_CTX // TM, 0), 0))]


def _token_tile(nx, refs):
    if nx == 1:
        return refs[0][...]
    return jnp.where(pl.program_id(0) < N_CTX // TM, refs[0][...], refs[1][...])


def _inproj_kernel(nx, idx_ref, *refs):
    del idx_ref
    _inproj_body(_token_tile(nx, refs), refs[nx:nx + 7], refs[-8:])


def _inproj_body(x, params, outs):
    mod_ref, n1g_ref, win_ref, wba_ref, ones_ref, qg_ref, kg_ref = params
    q_ref, k_ref, v_ref, zqkv_ref, zg_ref, zba_ref, kcache_ref, vcache_ref = outs
    y = x * lax.rsqrt(jnp.mean(x * x, axis=-1, keepdims=True) + EPS) * n1g_ref[...]
    m = mod_ref[...]
    h = (y * (1.0 + m[1:2, :]) + m[0:1, :]).astype(BF16)
    z = _dot(h, win_ref[...])
    zba_ref[...] = _dot(h, wba_ref[...])
    ones = ones_ref[...]

    def head_rms(zz, g):
        hi, lo = _split2(zz * zz)
        ms = (_dot(hi, ones) + _dot(lo, ones)) * (1.0 / NA_HD)
        return zz * lax.rsqrt(ms + EPS) * g

    q_ref[...] = head_rms(z[:, 0:NA_W], qg_ref[...]) * (NA_HD ** -0.5)
    k = head_rms(z[:, NA_W:2 * NA_W], kg_ref[...])
    v = z[:, 2 * NA_W:3 * NA_W]
    k_ref[...] = k
    v_ref[...] = v
    zqkv_ref[...] = z[:, 3 * NA_W:3 * NA_W + 3 * GDN_W]
    zg_ref[...] = z[:, 3 * NA_W + 3 * GDN_W:N_MAIN]

    @pl.when(pl.program_id(0) < BATCH)
    def _():
        kcache_ref[...] = k
        vcache_ref[...] = v


def _inproj(layer, tile_mod, x, mod, n1g, w_main, w_ba, ones_bd, qg, kg, caches):
    assert TM == SEQ
    tok = lambda w: pl.BlockSpec((TM, w), lambda t, idx: (t, 0))
    full = lambda a: pl.BlockSpec(a.shape, lambda t, idx: (0,) * a.ndim)
    out_w = (NA_W, NA_W, NA_W, 3 * GDN_W, GDN_W, LANE)
    cache_blk = pl.BlockSpec((None, None, SEQ, NA_W), lambda t, idx: (jnp.minimum(t, BATCH - 1), layer, 0, 0))
    cache_shape = jax.ShapeDtypeStruct((BATCH, DEPTH, SEQ, NA_W), F32)
    xs = list(x) if isinstance(x, tuple) else [x]
    ins = [tile_mod, *xs, mod, n1g, w_main, w_ba, ones_bd, qg, kg]
    specs = _token_specs(x) + [pl.BlockSpec((None, 6, D_MODEL), lambda t, idx: (idx[t], 0, 0)),
                               full(n1g), full(w_main), full(w_ba), full(ones_bd), full(qg), full(kg)]
    aliases = {}
    if caches is not None:
        aliases = {len(ins): len(out_w), len(ins) + 1: len(out_w) + 1}
        ins += list(caches)
        specs += [pl.BlockSpec(memory_space=pl.ANY)] * 2
    return pl.pallas_call(
        functools.partial(_inproj_kernel, len(xs)),
        out_shape=[jax.ShapeDtypeStruct((N_TOK, w), F32) for w in out_w] + [cache_shape, cache_shape],
        grid_spec=pltpu.PrefetchScalarGridSpec(
            num_scalar_prefetch=1,
            grid=(N_TOK // TM,),
            in_specs=specs,
            out_specs=[tok(w) for w in out_w] + [cache_blk, cache_blk],
        ),
        input_output_aliases=aliases,
        compiler_params=_params(("arbitrary",)),
        name="inproj",
    )(*ins)


def _row_copy(src, src_row, dst, dst_row, sem):
    return pltpu.make_async_copy(src.at[pl.ds(src_row, 1), :], dst.at[pl.ds(dst_row, 1), :], sem)


def _inproj_gather_kernel(idx_ref, slot_ref, x1_ref, modp_ref, ys_ref, *refs):
    del idx_ref
    params, outs, (buf_ref, sem_ref) = refs[:7], refs[-11:-2], refs[-2:]
    t = pl.program_id(0)
    last = pl.num_programs(0) - 1
    cur = t % 2

    def fetch(tile, which):
        for i in range(TM):
            _row_copy(ys_ref, slot_ref[tile * TM + i], buf_ref.at[which], i, sem_ref.at[which]).start(priority=i % 2)

    def drain(which):
        for i in range(TM):
            _row_copy(ys_ref, 0, buf_ref.at[which], 0, sem_ref.at[which]).wait()

    @pl.when(t == 0)
    def _():
        fetch(0, 0)

    fetch(jnp.minimum(t + 1, last), 1 - cur)
    drain(cur)
    x = x1_ref[...] + modp_ref[5:6, :] * buf_ref[cur]
    outs[0][...] = x
    _inproj_body(x, params, outs[1:])

    @pl.when(t == last)
    def _():
        drain(1 - cur)


def _inproj_gather(layer, tile_mod, slot, x1, mod_prev, ys, mod, n1g, w_main, w_ba, ones_bd, qg, kg, caches):
    tok = lambda w: pl.BlockSpec((TM, w), lambda t, idx, s: (t, 0))
    full = lambda a: pl.BlockSpec(a.shape, lambda t, idx, s: (0,) * a.ndim)
    modblk = pl.BlockSpec((None, 6, D_MODEL), lambda t, idx, s: (idx[t], 0, 0))
    out_w = (D_MODEL, NA_W, NA_W, NA_W, 3 * GDN_W, GDN_W, LANE)
    cache_blk = pl.BlockSpec((None, None, SEQ, NA_W), lambda t, idx, s: (jnp.minimum(t, BATCH - 1), layer, 0, 0))
    cache_shape = jax.ShapeDtypeStruct((BATCH, DEPTH, SEQ, NA_W), F32)
    ins = [tile_mod, slot, x1, mod_prev, ys, mod, n1g, w_main, w_ba, ones_bd, qg, kg, *caches]
    specs = [tok(D_MODEL), modblk, pl.BlockSpec(memory_space=pl.ANY), modblk,
             full(n1g), full(w_main), full(w_ba), full(ones_bd), full(qg), full(kg)]
    specs += [pl.BlockSpec(memory_space=pl.ANY)] * 2
    return pl.pallas_call(
        _inproj_gather_kernel,
        out_shape=[jax.ShapeDtypeStruct((N_TOK, w), F32) for w in out_w] + [cache_shape, cache_shape],
        grid_spec=pltpu.PrefetchScalarGridSpec(
            num_scalar_prefetch=2,
            grid=(N_TOK // TM,),
            in_specs=specs,
            out_specs=[tok(w) for w in out_w] + [cache_blk, cache_blk],
            scratch_shapes=[pltpu.VMEM((2, TM, D_MODEL), F32), pltpu.SemaphoreType.DMA((2,))],
        ),
        input_output_aliases={len(ins) - 2: len(out_w), len(ins) - 1: len(out_w) + 1},
        compiler_params=_params(("arbitrary",)),
        name="inproj_gather",
    )(*ins)


def _pair_masks(shape):
    lane = lax.broadcasted_iota(jnp.int32, shape, 1)
    return lane < NA_HD


def _ctx_attn_kernel(q_ref, k_ref, v_ref, o_ref):
    first = _pair_masks((SEQ, LANE))
    col = lambda p: slice(p * LANE, (p + 1) * LANE)
    heads = range(NA_HEADS)
    pairs = range(NA_HEADS // 2)
    kb = [k_ref[:, col(p)].astype(BF16) for p in pairs]
    vb = [v_ref[:, col(p)].astype(BF16) for p in pairs]
    qh = [jnp.where(first if h % 2 == 0 else jnp.logical_not(first), q_ref[:, col(h // 2)], 0.0).astype(BF16)
          for h in heads]
    s = [_dot_nt(qh[h], kb[h // 2]) for h in heads]
    e = [jnp.exp(s[h] - jnp.max(s[h], axis=-1, keepdims=True)) for h in heads]
    den = [jnp.sum(e[h], axis=-1, keepdims=True) for h in heads]
    o = [_dot(e[h].astype(BF16), vb[h // 2]) for h in heads]
    for p in pairs:
        o_ref[:, col(p)] = jnp.where(first, o[2 * p] / den[2 * p], o[2 * p + 1] / den[2 * p + 1])


def _ctx_attention(q, k, v):
    blk = pl.BlockSpec((SEQ, NA_W), lambda b: (b, 0))
    return pl.pallas_call(
        _ctx_attn_kernel,
        out_shape=jax.ShapeDtypeStruct((N_TOK, NA_W), F32),
        grid=(BATCH,),
        in_specs=[blk, blk, blk],
        out_specs=blk,
        compiler_params=_params(("parallel",)),
        name="ctx_attention",
    )(q, k, v)


NA_PAIR_GROUP = 4


def _na_kernel(q_ref, k_ref, v_ref, kc_ref, vc_ref, bias_ref, o_all_ref, o_ref, kb_ref, vb_ref, kcb_ref, vcb_ref):
    del o_all_ref
    r = pl.program_id(1)

    @pl.when(r == 0)
    def _():
        kb_ref[...] = k_ref[...].astype(BF16)
        vb_ref[...] = v_ref[...].astype(BF16)
        kcb_ref[...] = kc_ref[...].astype(BF16)
        vcb_ref[...] = vc_ref[...].astype(BF16)

    base = jnp.clip(r - NA_KH // 2, 0, GRID_ROWS - NA_KH)
    dr0 = base - r + (NA_KH - 1)
    row0 = pl.multiple_of(base * GRID_W, GRID_W)
    nwin = NA_KH * GRID_W
    first = _pair_masks((GRID_W, LANE))
    col = lambda p: slice(p * LANE, (p + 1) * LANE)
    for g0 in range(0, NA_HEADS // 2, NA_PAIR_GROUP):
        pairs = range(g0, g0 + NA_PAIR_GROUP)
        heads = range(2 * g0, 2 * (g0 + NA_PAIR_GROUP))
        qh = {h: jnp.where(first if h % 2 == 0 else jnp.logical_not(first), q_ref[:, col(h // 2)], 0.0).astype(BF16)
              for h in heads}
        kw = {p: kb_ref[pl.ds(row0, nwin), col(p)] for p in pairs}
        kc = {p: kcb_ref[:, col(p)] for p in pairs}
        vw = {p: vb_ref[pl.ds(row0, nwin), col(p)] for p in pairs}
        vc = {p: vcb_ref[:, col(p)] for p in pairs}
        s_loc = {h: _dot_nt(qh[h], kw[h // 2]) + bias_ref[h, dr0] for h in heads}
        s_ctx = {h: _dot_nt(qh[h], kc[h // 2]) for h in heads}
        mx = {h: jnp.maximum(jnp.max(s_loc[h], axis=-1, keepdims=True), jnp.max(s_ctx[h], axis=-1, keepdims=True))
              for h in heads}
        e_loc = {h: jnp.exp(s_loc[h] - mx[h]) for h in heads}
        e_ctx = {h: jnp.exp(s_ctx[h] - mx[h]) for h in heads}
        den = {h: jnp.sum(e_loc[h], axis=-1, keepdims=True) + jnp.sum(e_ctx[h], axis=-1, keepdims=True)
               for h in heads}
        o = {h: _dot(e_loc[h].astype(BF16), vw[h // 2]) + _dot(e_ctx[h].astype(BF16), vc[h // 2]) for h in heads}
        for p in pairs:
            o_ref[:, col(p)] = jnp.where(first, o[2 * p] / den[2 * p], o[2 * p + 1] / den[2 * p + 1])


def _na_attention(layer, q, k, v, cache_k, cache_v, bias, o_all):
    lat0 = N_CTX // DEC_SEQ
    qblk = pl.BlockSpec((GRID_W, NA_W), lambda b, r: (N_CTX // GRID_W + b * GRID_ROWS + r, 0))
    kvblk = pl.BlockSpec((DEC_SEQ, NA_W), lambda b, r: (lat0 + b, 0))
    cblk = pl.BlockSpec((None, None, PAST_LEN, NA_W), lambda b, r: (b, layer, 0, 0))
    return pl.pallas_call(
        _na_kernel,
        out_shape=jax.ShapeDtypeStruct((N_TOK, NA_W), F32),
        grid=(DEC_BATCH, GRID_ROWS),
        in_specs=[qblk, kvblk, kvblk, cblk, cblk,
                  pl.BlockSpec(bias.shape, lambda b, r: (0, 0, 0, 0)),
                  pl.BlockSpec(memory_space=pl.ANY)],
        out_specs=qblk,
        scratch_shapes=[pltpu.VMEM((DEC_SEQ, NA_W), BF16), pltpu.VMEM((DEC_SEQ, NA_W), BF16),
                        pltpu.VMEM((PAST_LEN, NA_W), BF16), pltpu.VMEM((PAST_LEN, NA_W), BF16)],
        input_output_aliases={6: 0},
        compiler_params=_params(("parallel", "arbitrary")),
        name="na_attention",
    )(q, k, v, cache_k, cache_v, bias, o_all)


N_DC = 2 * NA_KW - 1


def _na_bias_kernel(r_ref, o_ref):
    shape = (GRID_W, NA_KH * GRID_W)
    qc = lax.broadcasted_iota(jnp.int32, shape, 0)
    kc = lax.broadcasted_iota(jnp.int32, shape, 1) & (GRID_W - 1)
    ws = jnp.clip(qc - NA_KW // 2, 0, GRID_W - NA_KW)
    valid = (kc >= ws) & (kc < ws + NA_KW)
    dc = jnp.where(valid, kc - qc + (NA_KW - 1), -1)
    for d0 in range(NA_KH):
        acc = jnp.full(shape, -jnp.inf, F32)
        for d in range(N_DC):
            acc = jnp.where(dc == d, r_ref[d0, d:d + 1, :], acc)
        o_ref[d0] = acc


def _na_bias_table(rpb):
    win = jnp.stack([rpb[:, d0:d0 + NA_KH, :] for d0 in range(NA_KH)], axis=1)
    rexp = jnp.repeat(win.transpose(0, 1, 3, 2), GRID_W, axis=-1)
    rexp = jnp.pad(rexp, ((0, 0), (0, 0), (0, 32 - N_DC), (0, 0)))
    blk = lambda rows: pl.BlockSpec((None, NA_KH, rows, NA_KH * GRID_W), lambda h: (h, 0, 0, 0))
    return pl.pallas_call(
        _na_bias_kernel,
        out_shape=jax.ShapeDtypeStruct((NA_HEADS, NA_KH, GRID_W, NA_KH * GRID_W), F32),
        grid=(NA_HEADS,),
        in_specs=[blk(32)],
        out_specs=blk(GRID_W),
        compiler_params=_params(("parallel",)),
        name="na_bias",
    )(rexp)


def _rope_tables():
    t = np.arange(DEC_SEQ)
    half = GDN_DK // 2
    inv_freq = (np.float32(ROPE_BASE) ** (-np.arange(0, half, 2, dtype=np.float32) / np.float32(half)))
    ang_r = (t // GRID_W).astype(np.float32)[:, None] * inv_freq
    ang_c = (t % GRID_W).astype(np.float32)[:, None] * inv_freq
    cr, sr, cc, sc = np.cos(ang_r), np.sin(ang_r), np.cos(ang_c), np.sin(ang_c)
    z = np.zeros_like(sr)
    cos = np.concatenate([cr, cr, cc, cc], axis=1)
    s_up = np.concatenate([-sr, z, -sc, z], axis=1)
    s_dn = np.concatenate([z, sr, z, sc], axis=1)
    return (jnp.asarray(cos, F32), jnp.asarray(s_up, F32), jnp.asarray(s_dn, F32))


HALO = 8
PREP_CHUNKS = 4
N_STREAM = 2 * GDN_HEADS
LOG_COL = N_STREAM


def _gdn_prep_kernel(axial, seq, *refs):
    if axial:
        (zqkv_ref, zba_ref, cw_ref, alog_ref, dtb_ref, cos_ref, sup_ref, sdn_ref,
         a_ref, qk_ref, r_ref, qd_ref, kd_ref, gl_ref, xs_ref) = refs
    else:
        (zqkv_ref, zba_ref, cw_ref, alog_ref, dtb_ref,
         a_ref, qk_ref, r_ref, qd_ref, kd_ref, gl_ref, xs_ref) = refs
    width = 3 * GDN_W
    part = pl.program_id(1)

    @pl.when(part == 0)
    def _():
        xs_ref[0:HALO, :] = jnp.zeros((HALO, width), F32)
        xs_ref[HALO + seq:HALO + seq + HALO, :] = jnp.zeros((HALO, width), F32)
        xs_ref[HALO:HALO + seq, :] = zqkv_ref[...]

    ri = lax.broadcasted_iota(jnp.int32, (CHUNK, CHUNK), 0)
    ci = lax.broadcasted_iota(jnp.int32, (CHUNK, CHUNK), 1)
    incl = (ri >= ci, ri <= ci)
    strict = (ri > ci, ri < ci)
    lane = lax.broadcasted_iota(jnp.int32, (CHUNK, LANE), 1)

    def body(i, carry):
        r0 = pl.multiple_of((part * PREP_CHUNKS + i) * CHUNK, CHUNK)
        rows = pl.ds(r0, CHUNK)
        win = xs_ref[pl.ds(r0, CHUNK + 2 * HALO), :]
        acc = win[HALO:HALO + CHUNK, :] * cw_ref[CONV_K // 2:CONV_K // 2 + 1, :]
        for j in range(CONV_K):
            if j != CONV_K // 2:
                lo = HALO + j - CONV_K // 2
                acc = acc + win[lo:lo + CHUNK, :] * cw_ref[j:j + 1, :]
        act = _silu(acc)

        zba = zba_ref[rows, :]
        beta_all = 1.0 / (1.0 + jnp.exp(-zba))
        xa = zba + dtb_ref[...]
        softplus = jnp.maximum(xa, 0.0) + jnp.log(1.0 + jnp.exp(-jnp.abs(xa)))
        bg = jnp.where(lane < LOG_COL, beta_all, -jnp.exp(alog_ref[...]) * softplus)

        slabs = [act[:, s * GDN_DK:(s + 1) * GDN_DK] for s in range(2 * GDN_HEADS)]
        ssq = [jnp.sum(x * x, axis=-1, keepdims=True) for x in slabs]
        slabs = [x * lax.rsqrt(s + EPS) for x, s in zip(slabs, ssq)]
        if axial:
            cos, s_up, s_dn = cos_ref[rows, :], sup_ref[rows, :], sdn_ref[rows, :]
            up = [pltpu.roll(x, GDN_DK - GDN_DK // 4, 1) for x in slabs]
            dn = [pltpu.roll(x, GDN_DK // 4, 1) for x in slabs]
            slabs = [x * cos + u * s_up + w * s_dn for x, u, w in zip(slabs, up, dn)]
        qs = [x * (GDN_DK ** -0.5) for x in slabs[:GDN_HEADS]]
        ks = slabs[GDN_HEADS:]
        kbfs = [k.astype(BF16) for k in ks]
        kks = [_dot_nt(kb_, kb_) for kb_ in kbfs]
        qks = [_dot_nt(q.astype(BF16), kb_) for q, kb_ in zip(qs, kbfs)]
        heads = [(qs[hd], ks[hd], act[:, 2 * GDN_W + hd * GDN_DK:2 * GDN_W + (hd + 1) * GDN_DK], kks[hd], qks[hd])
                 for hd in range(GDN_HEADS)]

        gcums = [_dot_exact_lhs(incl[d].astype(BF16), bg) for d in range(2)]
        gcum_ts = [g.T for g in gcums]
        for d in range(2):
            gcum, gcum_t = gcums[d], gcum_ts[d]
            last = gcum[0:1, :] if d else gcum[CHUNK - 1:CHUNK, :]
            e_in = jnp.exp(gcum)
            e_out = jnp.exp(last - gcum)
            gl_ref[i, d:d + 1, :] = jnp.exp(last)
            for hd in range(GDN_HEADS):
                q, k, v, kk, qk = heads[hd]
                cb = d * GDN_HEADS + hd
                cg = LOG_COL + cb
                beta = bg[:, cb:cb + 1]
                eg = e_in[:, cg:cg + 1]
                el = e_out[:, cg:cg + 1]
                diff = gcum[:, cg:cg + 1] - gcum_t[cg:cg + 1, :]
                decay = jnp.where(incl[d], jnp.exp(jnp.where(incl[d], diff, 0.0)), 0.0)
                a_ref[d, i, hd] = jnp.where(strict[d], beta * kk * decay, 0.0)
                qk_ref[d, i, hd] = (qk * decay).astype(BF16)
                kb = k * beta
                r_ref[d, i, hd, :, 0:GDN_DK] = (v * beta).astype(BF16)
                r_ref[d, i, hd, :, GDN_DK:2 * GDN_DK] = (kb * eg).astype(BF16)
                qd_ref[d, i, hd] = (q * eg).astype(BF16)
                kd_ref[d, i, hd] = (k * el).astype(BF16)
        return carry

    lax.fori_loop(0, PREP_CHUNKS, body, 0)


def _gdn_prep(axial, zqkv, zba, conv_w, a_log, dt_bias, rope):
    seq = DEC_SEQ if axial else SEQ
    nseq = DEC_BATCH if axial else BATCH
    nparts = seq // (PREP_CHUNKS * CHUNK)
    nchunk = nseq * seq // CHUNK
    blk0 = N_CTX // DEC_SEQ if axial else 0
    tok = lambda w: pl.BlockSpec((seq, w), lambda b, p: (blk0 + b, 0))
    full = lambda a: pl.BlockSpec(a.shape, lambda b, p: (0,) * a.ndim)
    ins = [zqkv, zba, conv_w, a_log, dt_bias]
    specs = [tok(3 * GDN_W), tok(LANE), full(conv_w), full(a_log), full(dt_bias)]
    if axial:
        ins += list(rope)
        specs += [full(t) for t in rope]
    tile = lambda w: pl.BlockSpec((2, PREP_CHUNKS, GDN_HEADS, CHUNK, w), lambda b, p: (0, b * nparts + p, 0, 0, 0))
    shape = lambda w, dt: jax.ShapeDtypeStruct((2, nchunk, GDN_HEADS, CHUNK, w), dt)
    return pl.pallas_call(
        functools.partial(_gdn_prep_kernel, axial, seq),
        out_shape=[shape(CHUNK, F32), shape(CHUNK, BF16), shape(2 * GDN_DK, BF16),
                   shape(GDN_DK, BF16), shape(GDN_DK, BF16),
                   jax.ShapeDtypeStruct((nchunk, 2, LANE), F32)],
        grid=(nseq, nparts),
        in_specs=specs,
        out_specs=[tile(CHUNK), tile(CHUNK), tile(2 * GDN_DK), tile(GDN_DK), tile(GDN_DK),
                   pl.BlockSpec((PREP_CHUNKS, 2, LANE), lambda b, p: (b * nparts + p, 0, 0))],
        scratch_shapes=[pltpu.VMEM((seq + 2 * HALO, 3 * GDN_W), F32)],
        compiler_params=_params(("parallel", "arbitrary")),
        name="gdn_prep_lat" if axial else "gdn_prep_ctx",
    )(*ins)


TRI_BLK = 8
TRI_ELEMS = CHUNK * CHUNK


def _tri_inverse_kernel(groups_per_dir, a_ref, o_ref, at_ref, tt_ref):
    backward = pl.program_id(0) >= groups_per_dir
    for blk in range(TRI_ELEMS // LANE):
        cols = slice(blk * LANE, (blk + 1) * LANE)
        at_ref[cols, :] = a_ref[:, cols].T
    tt_ref[...] = jnp.zeros(tt_ref.shape, F32)
    nblk = CHUNK // TRI_BLK

    def substitute(mirror):
        phys = (lambda idx: CHUNK - 1 - idx) if mirror else (lambda idx: idx)
        for ib in range(nblk):
            c_lo, width = ((nblk - 1 - ib) * TRI_BLK if mirror else 0), (ib + 1) * TRI_BLK

            def row_body(ii, carry, ib=ib, c_lo=c_lo, width=width):
                i = phys(ib * TRI_BLK + ii)
                cidx = lax.broadcasted_iota(jnp.int32, (width, LANE), 0) + c_lo
                acc = (cidx == i).astype(F32)
                for jb in range(ib + 1):
                    w_j = (jb + 1) * TRI_BLK
                    cj_lo = (nblk - 1 - jb) * TRI_BLK if mirror else 0
                    sub = slice(cj_lo - c_lo, cj_lo - c_lo + w_j)
                    part = acc[sub]
                    for jl in range(jb * TRI_BLK, (jb + 1) * TRI_BLK):
                        j = phys(jl)
                        arow = at_ref[pl.ds(i * CHUNK + j, 1), :]
                        part = part - arow * tt_ref[j * CHUNK + cj_lo:j * CHUNK + cj_lo + w_j, :]
                    acc = part if w_j == width else (
                        jnp.concatenate([part, acc[w_j:]], axis=0) if not mirror
                        else jnp.concatenate([acc[:width - w_j], part], axis=0))
                tt_ref[pl.ds(pl.multiple_of(i * CHUNK + c_lo, TRI_BLK), width), :] = acc
                return carry

            lax.fori_loop(0, TRI_BLK, row_body, 0)

    @pl.when(jnp.logical_not(backward))
    def _():
        substitute(False)

    @pl.when(backward)
    def _():
        substitute(True)

    for blk in range(TRI_ELEMS // LANE):
        cols = slice(blk * LANE, (blk + 1) * LANE)
        o_ref[:, cols] = tt_ref[cols, :].T.astype(BF16)


def _tri_inverse(a):
    nprob = a.shape[1] * a.shape[2]
    groups_per_dir = nprob // LANE
    blk = pl.BlockSpec((LANE, TRI_ELEMS), lambda g: (g, 0))
    out = pl.pallas_call(
        functools.partial(_tri_inverse_kernel, groups_per_dir),
        out_shape=jax.ShapeDtypeStruct((2 * nprob, TRI_ELEMS), BF16),
        grid=(2 * groups_per_dir,),
        in_specs=[blk],
        out_specs=blk,
        scratch_shapes=[pltpu.VMEM((TRI_ELEMS, LANE), F32), pltpu.VMEM((TRI_ELEMS, LANE), F32)],
        compiler_params=_params(("parallel",)),
        name="tri_inverse",
    )(a.reshape(2 * nprob, TRI_ELEMS))
    return out.reshape(a.shape)


def _gdn_scan_kernel(nchunk, t_ref, qk_ref, r_ref, qd_ref, kd_ref, gl_ref, s0_ref, *rest):
    o_ref, sfin_ref, state_ref = rest[-3:]
    state_ref[...] = s0_ref[...]
    streams = [(d, hd) for d in range(2) for hd in range(GDN_HEADS)]

    def body(c, carry):
        cc = (c, nchunk - 1 - c)
        uws = [_dot(t_ref[d, cc[d], hd], r_ref[d, cc[d], hd]).astype(BF16) for d, hd in streams]
        mbs = [_dot_tn(kd_ref[d, cc[d], hd], uw) for (d, hd), uw in zip(streams, uws)]
        qos = [_dot(qk_ref[d, cc[d], hd], uw) for (d, hd), uw in zip(streams, uws)]
        sts = [state_ref[d * GDN_HEADS + hd] for d, hd in streams]
        sbs = [s.astype(BF16) for s in sts]
        for (d, hd), qo, sb in zip(streams, qos, sbs):
            rows = pl.ds(pl.multiple_of(cc[d] * CHUNK, CHUNK), CHUNK)
            qp = (qd_ref[d, cc[d], hd].astype(F32) - qo[:, GDN_DK:]).astype(BF16)
            o_ref[d, rows, hd * GDN_DK:(hd + 1) * GDN_DK] = _dot(qp, sb) + qo[:, :GDN_DK]
        for (d, hd), mb, s, sb in zip(streams, mbs, sts, sbs):
            sidx = d * GDN_HEADS + hd
            gl = gl_ref[cc[d]][d:d + 1, LOG_COL + sidx:LOG_COL + sidx + 1]
            state_ref[sidx] = s * gl - _dot(mb[:, GDN_DK:].astype(BF16), sb) + mb[:, :GDN_DK]
        return carry

    lax.fori_loop(0, nchunk, body, 0, unroll=2)
    sfin_ref[...] = state_ref[...]


def _gdn_scan(seq, nseq, tinv, qk, r, qd, kd, gl, s0, o_all=None):
    nc = seq // CHUNK
    blk0 = 0 if o_all is None else N_CTX // seq
    tile = lambda w: pl.BlockSpec((2, nc, GDN_HEADS, CHUNK, w), lambda b: (0, b, 0, 0, 0))
    sblk = pl.BlockSpec((None, N_STREAM, GDN_DK, GDN_DK), lambda b: (b, 0, 0, 0))
    oblk = pl.BlockSpec((2, seq, GDN_W), lambda b: (0, blk0 + b, 0))
    ins = [tinv, qk, r, qd, kd, gl, s0]
    specs = [tile(CHUNK), tile(CHUNK), tile(2 * GDN_DK), tile(GDN_DK), tile(GDN_DK),
             pl.BlockSpec((nc, 2, LANE), lambda b: (b, 0, 0)), sblk]
    aliases = {}
    if o_all is not None:
        ins.append(o_all)
        specs.append(pl.BlockSpec(memory_space=pl.ANY))
        aliases = {len(ins) - 1: 0}
    return pl.pallas_call(
        functools.partial(_gdn_scan_kernel, nc),
        out_shape=[jax.ShapeDtypeStruct((2, N_TOK, GDN_W), F32),
                   jax.ShapeDtypeStruct((nseq, N_STREAM, GDN_DK, GDN_DK), F32)],
        grid=(nseq,),
        in_specs=specs,
        out_specs=[oblk, sblk],
        scratch_shapes=[pltpu.VMEM((N_STREAM, GDN_DK, GDN_DK), F32)],
        input_output_aliases=aliases,
        compiler_params=_params(("parallel",)),
        name="gdn_scan_lat" if seq == DEC_SEQ else "gdn_scan_ctx",
    )(*ins)


def _first_max(vals):
    sel = []
    taken = None
    for a, va in enumerate(vals):
        is_max = None
        for b, vb in enumerate(vals):
            if a == b:
                continue
            c = va >= vb
            is_max = c if is_max is None else (is_max & c)
        if taken is not None:
            is_max = is_max & jnp.logical_not(taken)
        sel.append(is_max)
        taken = is_max if taken is None else (taken | is_max)
    return sel


PAIR_SLOTS = ((0, 1), (0, 2), (0, 3), (1, 3), (1, 2), (3, 2))
N_CLASS = N_GROUPS * len(PAIR_SLOTS)
CLASS_EXPERTS = tuple((g * EXP_PER_GROUP + a, g * EXP_PER_GROUP + b)
                      for g in range(N_GROUPS) for a, b in PAIR_SLOTS)


def _route_rows(lt):
    gl = [lt[a:a + 1, :] for a in range(N_GROUPS)]
    gsel = _first_max(gl)
    gmax = functools.reduce(jnp.maximum, gl)
    gden = functools.reduce(lambda x, y: x + y, [jnp.exp(x - gmax) for x in gl])
    g_w = 1.0 / gden
    el = []
    for kx in range(EXP_PER_GROUP):
        acc = None
        for a in range(N_GROUPS):
            row = N_GROUPS + a * EXP_PER_GROUP + kx
            term = jnp.where(gsel[a], lt[row:row + 1, :], 0.0)
            acc = term if acc is None else acc + term
        el.append(acc)
    emax = functools.reduce(jnp.maximum, el)
    ee = [jnp.exp(x - emax) for x in el]
    eden = functools.reduce(lambda x, y: x + y, ee)
    ep = [x / eden for x in ee]
    top1 = _first_max(ep)
    ep2 = [jnp.where(top1[kx], -1.0, ep[kx]) for kx in range(EXP_PER_GROUP)]
    top2 = _first_max(ep2)
    chosen = [top1[kx] | top2[kx] for kx in range(EXP_PER_GROUP)]
    wsum = functools.reduce(lambda x, y: x + y,
                            [jnp.where(chosen[kx], ep[kx], 0.0) for kx in range(EXP_PER_GROUP)])
    within = [jnp.where(chosen[kx], ep[kx] / wsum, 0.0) for kx in range(EXP_PER_GROUP)]
    cls = jnp.zeros(g_w.shape, jnp.int32)
    w_a = jnp.zeros(g_w.shape, F32)
    w_b = jnp.zeros(g_w.shape, F32)
    for g in range(N_GROUPS):
        for kx, (a, b) in enumerate(PAIR_SLOTS):
            hit = gsel[g] & chosen[a] & chosen[b]
            cls = jnp.where(hit, g * len(PAIR_SLOTS) + kx, cls)
            w_a = jnp.where(hit, g_w * within[a], w_a)
            w_b = jnp.where(hit, g_w * within[b], w_b)
    return cls, w_a, w_b


H2W = D_MODEL + LANE
COL_WA = D_MODEL
COL_WB = D_MODEL + 1


def _outproj_kernel(nx, idx_ref, *refs):
    x = _token_tile(nx, refs)
    (ona_ref, og_ref, zg_ref, mod_ref, wo_ref, gng_ref, n2g_ref, wr_ref, br_ref,
     x1_ref, h2x_ref, cls_ref) = refs[nx:]
    del idx_ref
    m = mod_ref[...]
    og = og_ref[0] + og_ref[1]
    zg = zg_ref[...]
    parts = []
    for hd in range(GDN_HEADS):
        cols = slice(hd * GDN_DK, (hd + 1) * GDN_DK)
        oh = og[:, cols]
        oh = oh * lax.rsqrt(jnp.mean(oh * oh, axis=-1, keepdims=True) + EPS) * gng_ref[...]
        parts.append((oh * _silu(zg[:, cols])).astype(BF16))
    mix = _dot(ona_ref[...].astype(BF16), wo_ref[0:NA_W, :])
    for hd in range(GDN_HEADS):
        r0 = NA_W + hd * GDN_DK
        mix = mix + _dot(parts[hd], wo_ref[r0:r0 + GDN_DK, :])
    x1 = x + m[2:3, :] * mix
    x1_ref[...] = x1
    y = x1 * lax.rsqrt(jnp.mean(x1 * x1, axis=-1, keepdims=True) + EPS) * n2g_ref[...]
    h2 = y * (1.0 + m[4:5, :]) + m[3:4, :]
    h2x_ref[:, 0:D_MODEL] = h2.astype(BF16).astype(F32)
    logits = _dot3(h2, wr_ref[...]) + br_ref[...]
    cls, w_a, w_b = _route_rows(logits.T)
    rid = lax.broadcasted_iota(jnp.int32, (8, TM), 0)
    wt = jnp.where(rid == COL_WA - D_MODEL, w_a, jnp.where(rid == COL_WB - D_MODEL, w_b, 0.0))
    wt = jnp.concatenate([wt, jnp.zeros((LANE - 8, TM), F32)], axis=0)
    h2x_ref[:, D_MODEL:H2W] = wt.T
    cls_ref[...] = jnp.broadcast_to(cls, (8, TM))


def _outproj(tile_mod, x, o_na, o_gdn, zg, mod, w_out, gng, n2g, w_r, b_r):
    tok = lambda w: pl.BlockSpec((TM, w), lambda t, idx: (t, 0))
    full = lambda a: pl.BlockSpec(a.shape, lambda t, idx: (0,) * a.ndim)
    xs = list(x) if isinstance(x, tuple) else [x]
    return pl.pallas_call(
        functools.partial(_outproj_kernel, len(xs)),
        out_shape=[jax.ShapeDtypeStruct((N_TOK, D_MODEL), F32),
                   jax.ShapeDtypeStruct((N_TOK, H2W), F32),
                   jax.ShapeDtypeStruct((N_TOK // TM, 8, TM), jnp.int32)],
        grid_spec=pltpu.PrefetchScalarGridSpec(
            num_scalar_prefetch=1,
            grid=(N_TOK // TM,),
            in_specs=_token_specs(x) + [
                tok(NA_W),
                pl.BlockSpec((2, TM, GDN_W), lambda t, idx: (0, t, 0)), tok(GDN_W),
                pl.BlockSpec((None, 6, D_MODEL), lambda t, idx: (idx[t], 0, 0)),
                full(w_out), full(gng), full(n2g), full(w_r), full(b_r),
            ],
            out_specs=[tok(D_MODEL), tok(H2W), pl.BlockSpec((None, 8, TM), lambda t, idx: (t, 0, 0))],
        ),
        compiler_params=_params(("parallel",)),
        name="outproj",
    )(tile_mod, *xs, o_na, o_gdn, zg, mod, w_out, gng, n2g, w_r, b_r)


TM_X = 256
N_XTILE = N_TOK // TM_X + N_CLASS
N_SLOT = N_XTILE * TM_X
CLS_ROWS = N_TOK // LANE
TAB_EA, TAB_EB, TAB_USED, TAB_LAST = 0, 1, 2, 3


def _route_pos_kernel(cls_ref, slot_ref, tab_ref):
    cls = cls_ref[...]
    li = lax.broadcasted_iota(jnp.int32, (LANE, LANE), 0)
    lj = lax.broadcasted_iota(jnp.int32, (LANE, LANE), 1)
    before_lane = (li < lj).astype(BF16)
    ri = lax.broadcasted_iota(jnp.int32, (CLS_ROWS, CLS_ROWS), 0)
    rj = lax.broadcasted_iota(jnp.int32, (CLS_ROWS, CLS_ROWS), 1)
    before_row = (rj < ri).astype(BF16)
    lane = lax.broadcasted_iota(jnp.int32, (1, LANE), 1)
    tile_start = lane * TM_X
    off = jnp.zeros((1, 1), jnp.int32)
    slot = jnp.zeros(cls.shape, jnp.int32)
    tab_a = jnp.zeros((1, LANE), jnp.int32)
    tab_b = jnp.zeros((1, LANE), jnp.int32)
    tab_last = jnp.full((1, LANE), -1, jnp.int32)
    for c in range(N_CLASS):
        hit = cls == c
        one = hit.astype(F32)
        in_row = _dot(one.astype(BF16), before_lane)
        row_tot = jnp.broadcast_to(jnp.sum(one, axis=1, keepdims=True), one.shape)
        rank = in_row + _dot(before_row, row_tot.astype(BF16))
        count = jnp.sum(row_tot[:, 0:1], axis=0, keepdims=True).astype(jnp.int32)
        slot = jnp.where(hit, off + rank.astype(jnp.int32), slot)
        nxt = off + (((count + (TM_X - 1)) >> 8) << 8)
        mine = (tile_start >= off) & (tile_start < nxt)
        tab_a = jnp.where(mine, CLASS_EXPERTS[c][0], tab_a)
        tab_b = jnp.where(mine, CLASS_EXPERTS[c][1], tab_b)
        tab_last = jnp.where((lane == c) & (nxt > off), (nxt >> 8) - 1, tab_last)
        off = nxt
    slot_ref[...] = slot
    row = lax.broadcasted_iota(jnp.int32, (8, LANE), 0)
    used = jnp.broadcast_to(off >> 8, (8, LANE))
    tab_ref[...] = jnp.where(row == TAB_EA, tab_a, jnp.where(row == TAB_EB, tab_b,
                                                               jnp.where(row == TAB_LAST, tab_last, used)))


def _route_positions(cls):
    assert TM_X == 256
    return pl.pallas_call(
        _route_pos_kernel,
        out_shape=[jax.ShapeDtypeStruct((CLS_ROWS, LANE), jnp.int32),
                   jax.ShapeDtypeStruct((8, LANE), jnp.int32)],
        name="route_positions",
    )(cls)


ROW_UNROLL = 16


def _permute_kernel(slot_ref, tab_ref, h_ref, xs_ref, buf_ref, zero_ref, sem_ref, zsem_ref):
    t = pl.program_id(0)
    cur = t % 2

    @pl.when(t == 0)
    def _():
        zero_ref[...] = jnp.zeros(zero_ref.shape, F32)

        def tile_copy(c):
            return pltpu.make_async_copy(
                zero_ref, xs_ref.at[pl.ds(pl.multiple_of(tab_ref[TAB_LAST, c] * TM_X, TM_X), TM_X), :], zsem_ref.at[0])

        for c in range(N_CLASS):
            @pl.when(tab_ref[TAB_LAST, c] >= 0)
            def _():
                tile_copy(c).start()
        for c in range(N_CLASS):
            @pl.when(tab_ref[TAB_LAST, c] >= 0)
            def _():
                tile_copy(c).wait()

    buf_ref[cur] = h_ref[...]

    def issue(blk, carry):
        for j in range(ROW_UNROLL):
            i = blk * ROW_UNROLL + j
            _row_copy(buf_ref.at[cur], i, xs_ref, slot_ref[t * TM + i], sem_ref.at[cur]).start(priority=j % 2)
        return carry

    lax.fori_loop(0, TM // ROW_UNROLL, issue, 0)

    def drain(which):
        def wait(i, carry):
            _row_copy(buf_ref.at[which], 0, xs_ref, 0, sem_ref.at[which]).wait()
            return carry
        lax.fori_loop(0, TM, wait, 0, unroll=ROW_UNROLL)

    @pl.when(t > 0)
    def _():
        drain(1 - cur)

    @pl.when(t == pl.num_programs(0) - 1)
    def _():
        drain(cur)


def _permute(slot, tab, h2x):
    assert TM_X == TM
    return pl.pallas_call(
        _permute_kernel,
        out_shape=jax.ShapeDtypeStruct((N_SLOT, H2W), F32),
        grid_spec=pltpu.PrefetchScalarGridSpec(
            num_scalar_prefetch=2,
            grid=(N_TOK // TM,),
            in_specs=[pl.BlockSpec((TM, H2W), lambda t, s, tab: (t, 0))],
            out_specs=pl.BlockSpec(memory_space=pl.ANY),
            scratch_shapes=[pltpu.VMEM((2, TM, H2W), F32), pltpu.VMEM((TM_X, H2W), F32),
                            pltpu.SemaphoreType.DMA((2,)), pltpu.SemaphoreType.DMA((1,))],
        ),
        compiler_params=_params(("arbitrary",)),
        name="moe_permute",
    )(slot, tab, h2x)


def _experts_kernel(tab_ref, xs_ref, w1a_ref, w3a_ref, w2a_ref, w1b_ref, w3b_ref, w2b_ref, ys_ref,
                    c1a_ref, c3a_ref, c2a_ref, c1b_ref, c3b_ref, c2b_ref):
    t = pl.program_id(0)
    prev = jnp.maximum(t - 1, 0)

    @pl.when(t < tab_ref[TAB_USED, 0])
    def _():
        @pl.when((t == 0) | (tab_ref[TAB_EA, t] != tab_ref[TAB_EA, prev]))
        def _():
            c1a_ref[...] = w1a_ref[...].astype(BF16)
            c3a_ref[...] = w3a_ref[...].astype(BF16)
            c2a_ref[...] = w2a_ref[...].astype(BF16)

        @pl.when((t == 0) | (tab_ref[TAB_EB, t] != tab_ref[TAB_EB, prev]))
        def _():
            c1b_ref[...] = w1b_ref[...].astype(BF16)
            c3b_ref[...] = w3b_ref[...].astype(BF16)
            c2b_ref[...] = w2b_ref[...].astype(BF16)

        x = xs_ref[:, 0:D_MODEL].astype(BF16)
        gates = xs_ref[:, D_MODEL:H2W]
        w_a = gates[:, COL_WA - D_MODEL:COL_WA - D_MODEL + 1]
        w_b = gates[:, COL_WB - D_MODEL:COL_WB - D_MODEL + 1]
        a1 = _dot(x, c1a_ref[...])
        a3 = _dot(x, c3a_ref[...])
        b1 = _dot(x, c1b_ref[...])
        b3 = _dot(x, c3b_ref[...])
        sa = (_silu(a1) * a3 * w_a).astype(BF16)
        sb = (_silu(b1) * b3 * w_b).astype(BF16)
        ys_ref[...] = _dot(sa, c2a_ref[...]) + _dot(sb, c2b_ref[...])


def _experts(layer, tab, xs, w1, w3, w2):
    last = lambda t, tab: jnp.minimum(t, tab[TAB_USED, 0] - 1)
    tile = lambda w: pl.BlockSpec((TM_X, w), lambda t, tab: (last(t, tab), 0))
    wspec = lambda shape, row: pl.BlockSpec((None, None) + shape,
                                            lambda t, tab: (layer, tab[row, last(t, tab)], 0, 0))
    up, down = (D_MODEL, EXPERT_FF), (EXPERT_FF, D_MODEL)
    return pl.pallas_call(
        _experts_kernel,
        out_shape=jax.ShapeDtypeStruct((N_SLOT, D_MODEL), F32),
        grid_spec=pltpu.PrefetchScalarGridSpec(
            num_scalar_prefetch=1,
            grid=(N_XTILE,),
            in_specs=[tile(H2W),
                      wspec(up, TAB_EA), wspec(up, TAB_EA), wspec(down, TAB_EA),
                      wspec(up, TAB_EB), wspec(up, TAB_EB), wspec(down, TAB_EB)],
            out_specs=tile(D_MODEL),
            scratch_shapes=[pltpu.VMEM(up, BF16), pltpu.VMEM(up, BF16), pltpu.VMEM(down, BF16),
                            pltpu.VMEM(up, BF16), pltpu.VMEM(up, BF16), pltpu.VMEM(down, BF16)],
        ),
        compiler_params=_params(("arbitrary",)),
        name="moe_experts",
    )(tab, xs, w1, w3, w2, w1, w3, w2)


def _unpermute_kernel(idx_ref, slot_ref, x1_ref, mod_ref, ys_ref, *rest):
    outs, (buf_ref, sem_ref) = rest[:-2], rest[-2:]
    del idx_ref
    t = pl.program_id(0)
    cur = t % 2

    def fetch(tile, which):
        def issue(blk, carry):
            for j in range(ROW_UNROLL):
                i = blk * ROW_UNROLL + j
                _row_copy(ys_ref, slot_ref[tile * TM + i], buf_ref.at[which], i,
                          sem_ref.at[which]).start(priority=j % 2)
            return carry
        lax.fori_loop(0, TM // ROW_UNROLL, issue, 0)

    @pl.when(t == 0)
    def _():
        fetch(0, 0)

    @pl.when(t + 1 < pl.num_programs(0))
    def _():
        fetch(t + 1, 1 - cur)

    def wait(i, carry):
        _row_copy(ys_ref, 0, buf_ref.at[cur], 0, sem_ref.at[cur]).wait()
        return carry

    lax.fori_loop(0, TM, wait, 0, unroll=ROW_UNROLL)
    y = x1_ref[...] + mod_ref[5:6, :] * buf_ref[cur]

    @pl.when(t < N_CTX // TM)
    def _():
        outs[0][...] = y

    @pl.when(t >= N_CTX // TM)
    def _():
        outs[1][...] = y


def _unpermute(tile_mod, slot, x1, mod, ys):
    tok = pl.BlockSpec((TM, D_MODEL), lambda t, idx, s: (t, 0))
    out_shape = [jax.ShapeDtypeStruct((N_CTX, D_MODEL), F32), jax.ShapeDtypeStruct((N_LAT, D_MODEL), F32)]
    out_specs = _token_specs((None, None))
    return pl.pallas_call(
        _unpermute_kernel,
        out_shape=out_shape,
        grid_spec=pltpu.PrefetchScalarGridSpec(
            num_scalar_prefetch=2,
            grid=(N_TOK // TM,),
            in_specs=[tok, pl.BlockSpec((None, 6, D_MODEL), lambda t, idx, s: (idx[t], 0, 0)),
                      pl.BlockSpec(memory_space=pl.ANY)],
            out_specs=out_specs,
            scratch_shapes=[pltpu.VMEM((2, TM, D_MODEL), F32), pltpu.SemaphoreType.DMA((2,))],
        ),
        compiler_params=_params(("arbitrary",)),
        name="moe_unpermute",
    )(tile_mod, slot, x1, mod, ys)


def _tile_mod_index(tile):
    t = np.arange(N_TOK // tile) * tile
    return jnp.asarray(np.where(t < N_CTX, 0, 1 + (t - N_CTX) // DEC_SEQ), jnp.int32)


def kernel(x_prompt, x_sample, c, cache_k, cache_v, state_ssm, c_ctx, ada_w, ada_b, norm1_g, norm2_g, w_in, w_out, na_qn_g, na_kn_g, na_rpb, gdn_conv_w, gdn_a_log, gdn_dt_bias, gdn_norm_g, moe_w_rg, moe_b_rg, moe_w_re, moe_b_re, moe_w1, moe_w3, moe_w2):
    x = (x_prompt.reshape(N_CTX, D_MODEL), x_sample.reshape(N_LAT, D_MODEL))
    cv = jnp.concatenate([c_ctx[None, :], c, jnp.zeros((N_MOD_PAD - N_MOD, D_MODEL), F32)], axis=0)
    mod_all = _modulation(cv, ada_w, ada_b).reshape(DEPTH, N_MOD_PAD, 6, D_MODEL)

    idx_tm = _tile_mod_index(TM)
    hh = np.arange(NA_W) // NA_HD
    ones_bd = jnp.asarray(hh[:, None] == hh[None, :], BF16)
    rope = _rope_tables()
    cache_k4 = cache_k.reshape(DEC_BATCH, DEPTH, PAST_LEN, NA_W)
    cache_v4 = cache_v.reshape(DEC_BATCH, DEPTH, PAST_LEN, NA_W)
    lane_pad = lambda a, at: jnp.zeros((1, LANE), F32).at[0, at:at + a.size].set(a.reshape(-1))
    zeros_state = jnp.zeros((BATCH, 2 * GDN_HEADS, GDN_DK, GDN_DK), F32)

    caches, ss, pending = None, [], None
    for l in range(DEPTH):
        mod = mod_all[l]
        w_main = w_in[l, :, :N_MAIN].astype(BF16)
        w_ba = jnp.zeros((D_MODEL, LANE), BF16).at[:, :N_IN - N_MAIN].set(w_in[l, :, N_MAIN:].astype(BF16))
        qg = jnp.tile(na_qn_g[l], NA_HEADS)[None, :]
        kg = jnp.tile(na_kn_g[l], NA_HEADS)[None, :]
        proj = (mod, norm1_g[l][None, :], w_main, w_ba, ones_bd, qg, kg, caches)
        if pending is None:
            q, k, v, zqkv, zg, zba, *caches = _inproj(l, idx_tm, x, *proj)
        else:
            x, q, k, v, zqkv, zg, zba, *caches = _inproj_gather(l, idx_tm, *pending, *proj)

        bias = _na_bias_table(na_rpb[l])
        o_na = _na_attention(l, q, k, v, cache_k4, cache_v4, bias, _ctx_attention(q, k, v))

        conv_w = jnp.zeros((8, 3 * GDN_W), F32).at[:CONV_K].set(gdn_conv_w[l])
        a_row = lane_pad(gdn_a_log[l], 2 * GDN_HEADS)
        dt_row = lane_pad(gdn_dt_bias[l], 2 * GDN_HEADS)
        a_c, *ops_c = _gdn_prep(False, zqkv, zba, conv_w, a_row, dt_row, None)
        a_l, *ops_l = _gdn_prep(True, zqkv, zba, conv_w, a_row, dt_row, rope)
        og_c, s_ctx = _gdn_scan(SEQ, BATCH, _tri_inverse(a_c), *ops_c, zeros_state)
        s0_lat = state_ssm[:, l].reshape(DEC_BATCH, N_STREAM, GDN_DK, GDN_DK)
        o_gdn, _ = _gdn_scan(DEC_SEQ, DEC_BATCH, _tri_inverse(a_l), *ops_l, s0_lat, og_c)

        w_r = jnp.zeros((D_MODEL, LANE), F32).at[:, :N_GROUPS].set(moe_w_rg[l])
        w_r = w_r.at[:, N_GROUPS:N_GROUPS + N_EXPERTS].set(moe_w_re[l])
        b_r = lane_pad(jnp.concatenate([moe_b_rg[l], moe_b_re[l]]), 0)
        x1, h2x, cls = _outproj(idx_tm, x, o_na, o_gdn, zg, mod, w_out[l].astype(BF16),
                                gdn_norm_g[l][None, :], norm2_g[l][None, :], w_r, b_r)
        slot, tab = _route_positions(cls[:, 0, :].reshape(CLS_ROWS, LANE))
        slot = slot.reshape(N_TOK)
        ys = _experts(l, tab, _permute(slot, tab, h2x), moe_w1, moe_w3, moe_w2)
        if l == DEPTH - 1:
            x = _unpermute(idx_tm, slot, x1, mod, ys)
        else:
            pending = (slot, x1, mod, ys)

        ss.append(s_ctx.reshape(BATCH, 2, GDN_HEADS, GDN_DK, GDN_DK))

    y_prompt = x[0].reshape(BATCH, SEQ, D_MODEL)
    y_sample = x[1].reshape(DEC_BATCH, DEC_SEQ, D_MODEL)
    new_k, new_v = (a.reshape(BATCH, DEPTH, SEQ, NA_HEADS, NA_HD) for a in caches)
    return (y_prompt, y_sample, new_k, new_v, jnp.stack(ss, axis=1))
```
